```python
import jax, jax.numpy as jnp
from jax import lax
import numpy as np

D_MODEL = 2048
BATCH = 2
SEQ = 4096
DEPTH = 2
DEC_BATCH = 32
DEC_SEQ = 1
PAST_LEN = 8192
PAGE_SIZE = 128

D_MIX = D_MODEL
D_A = D_MIX // 4
D_B = D_MIX // 4
D_C = D_MIX // 4
NSA_HEADS = 8
HEAD_DIM = (D_MIX // 4) // NSA_HEADS
D_D = NSA_HEADS * HEAD_DIM
KV_HEADS = 2
HEADS_PER_KV = NSA_HEADS // KV_HEADS
D_KV = KV_HEADS * HEAD_DIM
CONV_A_WIDTH = 31
CONV_C_WIDTH = 3
CHUNK = 128
GMLP_HEADS = 4
CMP_BLOCK = 32
SEL_BLOCK = 64
N_SELECT = 16
WINDOW = 512
Q_BLOCK = 128
N_GROUPS = 4
EXPERTS_PER_GROUP = 8
N_EXPERTS = N_GROUPS * EXPERTS_PER_GROUP
TOP_K = 2
D_EXPERT = D_MODEL // 4
LN_EPS = 1e-5
NEG_INF = -1e30
FORCE_SCORE = 1e9
SPLIT_SIZES = (D_A, D_A, D_B, D_B, D_C, D_C, D_C, D_D, 6 * D_KV, 3 * NSA_HEADS)
SPLIT_POINTS = tuple(sum(SPLIT_SIZES[: i + 1]) for i in range(len(SPLIT_SIZES) - 1))
N_IN = sum(SPLIT_SIZES)

kernel_name = 'hymba_style_conv_gmlp_shortconv_nsa_hmoe_step'


def layer_norm(x, g, b):
    xf = x.astype(jnp.float32)
    mu = jnp.mean(xf, axis=-1, keepdims=True)
    var = jnp.mean(jnp.square(xf - mu), axis=-1, keepdims=True)
    return ((xf - mu) * lax.rsqrt(var + LN_EPS)).astype(x.dtype) * g + b


def last_rows(x, n):
    return x[:, x.shape[1] - n:]


def alibi_slopes():
    return jnp.exp2(-8.0 * jnp.arange(1, NSA_HEADS + 1, dtype=jnp.float32) / NSA_HEADS)


def causal_depthwise_conv(x, prev, w):
    xc = jnp.concatenate([prev, x], axis=1)
    y = lax.conv_general_dilated(xc, w[:, None, :], window_strides=(1,), padding='VALID',
                                 dimension_numbers=('NWC', 'WIO', 'NWC'), feature_group_count=x.shape[-1])
    return y, xc[:, xc.shape[1] - (w.shape[0] - 1):]


def chunk_spatial_gate(v, spatial_w, spatial_b):
    B, T, C = v.shape
    n_chunks = -(-T // CHUNK)
    vp = jnp.pad(v, ((0, 0), (0, n_chunks * CHUNK - T), (0, 0)))
    vp = vp.reshape(B, n_chunks, CHUNK, GMLP_HEADS, C // GMLP_HEADS)
    causal = jnp.tril(jnp.ones((CHUNK, CHUNK), dtype=bool))
    w = jnp.where(causal, spatial_w, jnp.zeros((), spatial_w.dtype))
    s = jnp.einsum('hij,bcjhd->bcihd', w, vp) + spatial_b.T[None, None, :, :, None]
    return s.reshape(B, n_chunks * CHUNK, C)[:, :T]


def nsa_attention(q, kv_c, kv_s, kv_w, gates, cmp_pos_k, cmp_pos_v, cmp_proj_k, cmp_proj_v, q_pos0):
    B, T = q.shape[:2]
    L = kv_c.shape[1]
    dt = q.dtype
    scale = HEAD_DIM ** -0.5
    slopes = alibi_slopes().reshape(KV_HEADS, HEADS_PER_KV)
    n_cmp = L // CMP_BLOCK
    cblk = kv_c[:, : n_cmp * CMP_BLOCK].reshape(B, n_cmp, CMP_BLOCK, 2, KV_HEADS, HEAD_DIM)
    k_cmp = jnp.einsum('bnigd,ig,gde->bnge', cblk[:, :, :, 0], cmp_pos_k, cmp_proj_k)
    v_cmp = jnp.einsum('bnigd,ig,gde->bnge', cblk[:, :, :, 1], cmp_pos_v, cmp_proj_v)
    cmp_start = jnp.arange(n_cmp) * CMP_BLOCK
    cmp_end = cmp_start + (CMP_BLOCK - 1)
    cmp_center = cmp_start.astype(jnp.float32) + 0.5 * (CMP_BLOCK - 1)
    n_sel = -(-L // SEL_BLOCK)
    ratio = SEL_BLOCK // CMP_BLOCK
    sblk = jnp.pad(kv_s, ((0, 0), (0, n_sel * SEL_BLOCK - L), (0, 0), (0, 0), (0, 0)))
    sblk = sblk.reshape(B, n_sel, SEL_BLOCK, 2, KV_HEADS, HEAD_DIM).transpose(0, 4, 1, 2, 3, 5)
    n_top = min(N_SELECT, n_sel)
    sel_ids = jnp.arange(n_sel)
    b_ix = jnp.arange(B)[:, None, None, None]
    g_ix = jnp.arange(KV_HEADS)[None, :, None, None]
    qb = min(Q_BLOCK, T)
    n_qb = -(-T // qb)
    t_pad = n_qb * qb - T
    qp = jnp.pad(q, ((0, 0), (0, t_pad), (0, 0), (0, 0)))
    gp = jnp.pad(gates, ((0, 0), (0, t_pad), (0, 0), (0, 0)))
    kvw = jnp.pad(kv_w, ((0, 0), (0, t_pad), (0, 0), (0, 0), (0, 0)))
    win_off = jnp.arange(WINDOW + qb)

    def query_block(i):
        q0 = i * qb
        qi = lax.dynamic_slice_in_dim(qp, q0, qb, axis=1).reshape(B, qb, KV_HEADS, HEADS_PER_KV, HEAD_DIM)
        gi = lax.dynamic_slice_in_dim(gp, q0, qb, axis=1).reshape(B, qb, 3, KV_HEADS, HEADS_PER_KV, 1)
        t = q_pos0 + q0 + jnp.arange(qb)
        s_c = jnp.einsum('bqghd,bngd->bghqn', qi, k_cmp).astype(jnp.float32) * scale
        s_c = s_c - slopes[None, :, :, None, None] * (t[:, None].astype(jnp.float32) - cmp_center[None, :])
        ok_c = cmp_end[None, :] <= t[:, None]
        p_c = jax.nn.softmax(jnp.where(ok_c, s_c, NEG_INF), axis=-1) * ok_c
        o_c = jnp.einsum('bghqn,bngd->bqghd', p_c.astype(dt), v_cmp)
        imp = jnp.pad(p_c.sum(axis=2), ((0, 0), (0, 0), (0, 0), (0, n_sel * ratio - n_cmp)))
        imp = imp.reshape(B, KV_HEADS, qb, n_sel, ratio).sum(axis=-1)
        cur = t // SEL_BLOCK
        ok_s = sel_ids[None, :] <= cur[:, None]
        forced = (sel_ids[None, :] == 0) | (sel_ids[None, :] >= cur[:, None] - 1)
        score = jnp.where(ok_s & forced, FORCE_SCORE, jnp.where(ok_s, imp, NEG_INF))
        _, idx = lax.top_k(score, n_top)
        kv_sel = sblk[b_ix, g_ix, idx]
        pos = idx[..., None] * SEL_BLOCK + jnp.arange(SEL_BLOCK)
        dist_s = (t[None, None, :, None, None] - pos)[:, :, None]
        s_s = jnp.einsum('bqghd,bgqksd->bghqks', qi, kv_sel[..., 0, :]).astype(jnp.float32) * scale
        s_s = jnp.where(dist_s >= 0, s_s - slopes[None, :, :, None, None, None] * dist_s.astype(jnp.float32), NEG_INF)
        p_s = jax.nn.softmax(s_s.reshape(B, KV_HEADS, HEADS_PER_KV, qb, n_top * SEL_BLOCK), axis=-1)
        v_sel = kv_sel[..., 1, :].reshape(B, KV_HEADS, qb, n_top * SEL_BLOCK, HEAD_DIM)
        o_s = jnp.einsum('bghqk,bgqkd->bqghd', p_s.astype(dt), v_sel)
        kwi = lax.dynamic_slice_in_dim(kvw, q0, WINDOW + qb, axis=1)
        wpos = q_pos0 + q0 - WINDOW + win_off
        dist_w = t[:, None] - wpos[None, :]
        ok_w = (dist_w >= 0) & (dist_w <= WINDOW) & (wpos[None, :] >= 0)
        s_w = jnp.einsum('bqghd,bkgd->bghqk', qi, kwi[:, :, 0]).astype(jnp.float32) * scale
        s_w = s_w - slopes[None, :, :, None, None] * dist_w.astype(jnp.float32)
        p_w = jax.nn.softmax(jnp.where(ok_w, s_w, NEG_INF), axis=-1)
        o_w = jnp.einsum('bghqk,bkgd->bqghd', p_w.astype(dt), kwi[:, :, 1])
        o = gi[:, :, 0] * o_c + gi[:, :, 1] * o_s + gi[:, :, 2] * o_w
        return o.reshape(B, qb, D_D)

    out = lax.map(query_block, jnp.arange(n_qb))
    return out.transpose(1, 0, 2, 3).reshape(B, n_qb * qb, D_D)[:, :T]


def hier_moe(x, w_group, b_group, w_router, b_router, w_gate_e, w_up_e, w_down_e):
    B, T, D = x.shape
    xt = x.reshape(B * T, D)
    g_logits = (xt @ w_group + b_group).astype(jnp.float32)
    g_sel = jnp.argmax(g_logits, axis=-1)
    g_gate = jnp.take_along_axis(jax.nn.softmax(g_logits, axis=-1), g_sel[:, None], axis=-1)
    e_logits = (xt @ w_router + b_router).astype(jnp.float32).reshape(-1, N_GROUPS, EXPERTS_PER_GROUP)
    e_in = jnp.take_along_axis(e_logits, g_sel[:, None, None], axis=1)[:, 0]
    top_v, top_i = lax.top_k(e_in, TOP_K)
    w_top = jax.nn.softmax(top_v, axis=-1) * g_gate
    w_grp = jnp.einsum('nk,nke->ne', w_top, jax.nn.one_hot(top_i, EXPERTS_PER_GROUP, dtype=jnp.float32))
    combine = (jax.nn.one_hot(g_sel, N_GROUPS, dtype=jnp.float32)[:, :, None] * w_grp[:, None, :]).reshape(-1, N_EXPERTS)
    hidden = jax.nn.silu(jnp.einsum('nd,edf->nef', xt, w_gate_e)) * jnp.einsum('nd,edf->nef', xt, w_up_e)
    y = jnp.einsum('nef,efd->nd', hidden * combine[:, :, None].astype(hidden.dtype), w_down_e)
    return y.reshape(B, T, D)


def hybrid_layer(x, conv_a_prev, conv_c_prev, past_c, past_s, win_prev, q_pos0, p):
    B, T, _ = x.shape
    h = jnp.einsum('btd,dn->btn', x, p['w_in']) + p['b_in']
    a_val, a_gate, b_u, b_v, c_b, c_c, c_x, q, kv, g = jnp.split(h, SPLIT_POINTS, axis=-1)
    a_in = a_val * jax.nn.sigmoid(a_gate)
    a_conv, conv_a_new = causal_depthwise_conv(a_in, conv_a_prev, p['conv_a_w'])
    a_out = jax.nn.silu(layer_norm(a_conv + p['conv_a_b'], p['ln_a_g'], p['ln_a_b']))
    v = layer_norm(jax.nn.gelu(b_v), p['ln_v_g'], p['ln_v_b'])
    b_out = jax.nn.gelu(b_u) * chunk_spatial_gate(v, p['spatial_w'], p['spatial_b'])
    c_conv, conv_c_new = causal_depthwise_conv(c_c * c_x, conv_c_prev, p['conv_c_w'])
    c_out = c_b * c_conv
    kv = kv.reshape(B, T, 3, 2, KV_HEADS, HEAD_DIM)
    kv_c = jnp.concatenate([past_c, kv[:, :, 0]], axis=1)
    kv_s = jnp.concatenate([past_s, kv[:, :, 1]], axis=1)
    kv_w = jnp.concatenate([win_prev, kv[:, :, 2]], axis=1)
    gates = jax.nn.sigmoid(g).reshape(B, T, 3, NSA_HEADS)
    d_out = nsa_attention(q.reshape(B, T, NSA_HEADS, HEAD_DIM), kv_c, kv_s, kv_w, gates,
                          p['cmp_pos_k'], p['cmp_pos_v'], p['cmp_proj_k'], p['cmp_proj_v'], q_pos0)
    mix = jnp.einsum('btc,cd->btd', jnp.concatenate([a_out, b_out, c_out, d_out], axis=-1), p['w_out'])
    alpha = (2.0 * DEPTH) ** 0.25
    x = layer_norm(alpha * x + mix, p['ln1_g'], p['ln1_b'])
    moe = hier_moe(x, p['w_group'], p['b_group'], p['w_router'], p['b_router'], p['w_gate_e'], p['w_up_e'], p['w_down_e'])
    x = layer_norm(alpha * x + moe, p['ln2_g'], p['ln2_b'])
    new_state = (kv[:, :, 0], kv[:, :, 1], last_rows(kv_w, min(WINDOW, PAST_LEN)), conv_a_new, conv_c_new, v)
    return x, new_state


def setup_inputs(seed: int = 0) -> dict:
    key = jax.random.key(seed)
    keys = iter(jax.random.split(key, 48))

    def nrm(shape, scale):
        return jax.random.normal(next(keys), shape, jnp.float32) * scale

    beta = (8.0 * DEPTH) ** -0.25
    n_pages = PAST_LEN // PAGE_SIZE
    n_used = DEC_BATCH * n_pages
    n_pool = n_used + max(1, n_used // 4)
    w_buf = min(WINDOW, PAST_LEN)
    page_shape = (DEPTH, n_pool, PAGE_SIZE, 2, KV_HEADS, HEAD_DIM)
    return {
        'x_prompt': nrm((BATCH, SEQ, D_MODEL), 1.0),
        'x_sample': nrm((DEC_BATCH, DEC_SEQ, D_MODEL), 1.0),
        'cache_cmp_kv': nrm(page_shape, 1.0),
        'cache_sel_kv': nrm(page_shape, 1.0),
        'state_win_kv': nrm((DEPTH, DEC_BATCH, w_buf, 2, KV_HEADS, HEAD_DIM), 1.0),
        'state_conv_a': nrm((DEPTH, DEC_BATCH, CONV_A_WIDTH - 1, D_A), 0.5),
        'state_conv_c': nrm((DEPTH, DEC_BATCH, CONV_C_WIDTH - 1, D_C), 0.5),
        'page_table': jax.random.permutation(next(keys), n_pool)[:n_used].reshape(DEC_BATCH, n_pages).astype(jnp.int32),
        'w_in': nrm((DEPTH, D_MODEL, N_IN), D_MODEL ** -0.5),
        'b_in': nrm((DEPTH, N_IN), 0.02),
        'conv_a_w': nrm((DEPTH, CONV_A_WIDTH, D_A), CONV_A_WIDTH ** -0.5),
        'conv_a_b': nrm((DEPTH, D_A), 0.02),
        'ln_a_g': 1.0 + nrm((DEPTH, D_A), 0.02),
        'ln_a_b': nrm((DEPTH, D_A), 0.02),
        'ln_v_g': 1.0 + nrm((DEPTH, D_B), 0.02),
        'ln_v_b': nrm((DEPTH, D_B), 0.02),
        'spatial_w': nrm((DEPTH, GMLP_HEADS, CHUNK, CHUNK), CHUNK ** -0.5),
        'spatial_b': 1.0 + nrm((DEPTH, GMLP_HEADS, CHUNK), 0.02),
        'conv_c_w': nrm((DEPTH, CONV_C_WIDTH, D_C), CONV_C_WIDTH ** -0.5),
        'cmp_pos_k': nrm((DEPTH, CMP_BLOCK, KV_HEADS), CMP_BLOCK ** -0.5),
        'cmp_pos_v': nrm((DEPTH, CMP_BLOCK, KV_HEADS), CMP_BLOCK ** -0.5),
        'cmp_proj_k': nrm((DEPTH, KV_HEADS, HEAD_DIM, HEAD_DIM), HEAD_DIM ** -0.5),
        'cmp_proj_v': nrm((DEPTH, KV_HEADS, HEAD_DIM, HEAD_DIM), HEAD_DIM ** -0.5),
        'w_out': nrm((DEPTH, D_MIX, D_MODEL), beta * D_MIX ** -0.5),
        'ln1_g': 1.0 + nrm((DEPTH, D_MODEL), 0.02),
        'ln1_b': nrm((DEPTH, D_MODEL), 0.02),
        'ln2_g': 1.0 + nrm((DEPTH, D_MODEL), 0.02),
        'ln2_b': nrm((DEPTH, D_MODEL), 0.02),
        'w_group': nrm((DEPTH, D_MODEL, N_GROUPS), D_MODEL ** -0.5),
        'b_group': nrm((DEPTH, N_GROUPS), 0.01),
        'w_router': nrm((DEPTH, D_MODEL, N_EXPERTS), D_MODEL ** -0.5),
        'b_router': nrm((DEPTH, N_EXPERTS), 0.01),
        'w_gate_e': nrm((DEPTH, N_EXPERTS, D_MODEL, D_EXPERT), D_MODEL ** -0.5),
        'w_up_e': nrm((DEPTH, N_EXPERTS, D_MODEL, D_EXPERT), D_MODEL ** -0.5),
        'w_down_e': nrm((DEPTH, N_EXPERTS, D_EXPERT, D_MODEL), beta * D_EXPERT ** -0.5),
    }


def reference(x_prompt, x_sample, cache_cmp_kv, cache_sel_kv, state_win_kv, state_conv_a, state_conv_c, page_table,
              w_in, b_in, conv_a_w, conv_a_b, ln_a_g, ln_a_b, ln_v_g, ln_v_b, spatial_w, spatial_b, conv_c_w,
              cmp_pos_k, cmp_pos_v, cmp_proj_k, cmp_proj_v, w_out, ln1_g, ln1_b, ln2_g, ln2_b,
              w_group, b_group, w_router, b_router, w_gate_e, w_up_e, w_down_e):
    bp = x_prompt.shape[0]
    bs = x_sample.shape[0]
    n_pages = PAST_LEN // PAGE_SIZE
    w_buf = min(WINDOW, PAST_LEN)
    dt = x_prompt.dtype
    hp, hs = x_prompt, x_sample
    st_p, st_s = [], []
    for l in range(DEPTH):
        p = {'w_in': w_in[l], 'b_in': b_in[l], 'conv_a_w': conv_a_w[l], 'conv_a_b': conv_a_b[l],
             'ln_a_g': ln_a_g[l], 'ln_a_b': ln_a_b[l], 'ln_v_g': ln_v_g[l], 'ln_v_b': ln_v_b[l],
             'spatial_w': spatial_w[l], 'spatial_b': spatial_b[l], 'conv_c_w': conv_c_w[l],
             'cmp_pos_k': cmp_pos_k[l], 'cmp_pos_v': cmp_pos_v[l], 'cmp_proj_k': cmp_proj_k[l], 'cmp_proj_v': cmp_proj_v[l],
             'w_out': w_out[l], 'ln1_g': ln1_g[l], 'ln1_b': ln1_b[l], 'ln2_g': ln2_g[l], 'ln2_b': ln2_b[l],
             'w_group': w_group[l], 'b_group': b_group[l], 'w_router': w_router[l], 'b_router': b_router[l],
             'w_gate_e': w_gate_e[l], 'w_up_e': w_up_e[l], 'w_down_e': w_down_e[l]}
        empty_kv = jnp.zeros((bp, 0, 2, KV_HEADS, HEAD_DIM), dt)
        hp, st = hybrid_layer(hp,
                              jnp.zeros((bp, CONV_A_WIDTH - 1, D_A), dt),
                              jnp.zeros((bp, CONV_C_WIDTH - 1, D_C), dt),
                              empty_kv, empty_kv,
                              jnp.zeros((bp, WINDOW, 2, KV_HEADS, HEAD_DIM), dt),
                              0, p)
        st_p.append(st)
        past_c = cache_cmp_kv[l][page_table].reshape(bs, n_pages * PAGE_SIZE, 2, KV_HEADS, HEAD_DIM)
        past_s = cache_sel_kv[l][page_table].reshape(bs, n_pages * PAGE_SIZE, 2, KV_HEADS, HEAD_DIM)
        win_prev = jnp.concatenate([jnp.zeros((bs, WINDOW - w_buf, 2, KV_HEADS, HEAD_DIM), state_win_kv.dtype),
                                    state_win_kv[l]], axis=1)
        hs, st = hybrid_layer(hs, state_conv_a[l], state_conv_c[l], past_c, past_s, win_prev, PAST_LEN, p)
        st_s.append(st)
    return (hp, hs,
            jnp.stack([s[0] for s in st_p]), jnp.stack([s[1] for s in st_p]), jnp.stack([s[2] for s in st_p]),
            jnp.stack([s[3] for s in st_p]), jnp.stack([s[4] for s in st_p]),
            jnp.stack([s[0] for s in st_s]), jnp.stack([s[1] for s in st_s]), jnp.stack([s[2] for s in st_s]),
            jnp.stack([s[3] for s in st_s]), jnp.stack([s[4] for s in st_s]), jnp.stack([s[5] for s in st_s]))
```

```python
import functools
import math

import jax
import jax.numpy as jnp
from jax import lax
from jax.experimental import pallas as pl
from jax.experimental.pallas import tpu as pltpu

F32 = jnp.float32
BF16 = jnp.bfloat16

D_MODEL = 2048
D_GRP = 512
NSA_HEADS = 8
HEAD_DIM = 64
KV_HEADS = 2
HEADS_PER_KV = 4
CONV_A_WIDTH = 31
CONV_C_WIDTH = 3
CHUNK = 128
GMLP_HEADS = 4
CMP_BLOCK = 32
SEL_BLOCK = 64
N_SELECT = 16
WINDOW = 512
PAGE_SIZE = 128
N_GROUPS = 4
EXPERTS_PER_GROUP = 8
N_EXPERTS = 32
D_EXPERT = 512
LN_EPS = 1e-5
NEG_INF = -1e30
FORCE_SCORE = 1e9
NOT_A_BLOCK = -3e38
N_IN = 4888
COL_Q = 3584
COL_KV = 4096
COL_G = 4864
SCALE = HEAD_DIM ** -0.5
SLOPES = tuple(2.0 ** (-(h + 1)) for h in range(NSA_HEADS))

LANES = 128
SUBLANES = 8
VMEM_LIMIT = 56 * 1024 * 1024

TOK_TILE = 256
INPROJ_TM = 768
INPROJ_TN = 512
MIX_TT = 256
Q_TILE = 128
KEY_CHUNK = 512
PAGES_PER_STEP = 16
MOE_TM = 256


def _dot(a, b):
    return jnp.dot(a, b, preferred_element_type=F32)


def _dot_nt(a, b):
    return lax.dot_general(a, b, (((1,), (1,)), ((), ())), preferred_element_type=F32)


def _layer_norm(x, g, b):
    mu = jnp.mean(x, axis=-1, keepdims=True)
    xc = x - mu
    var = jnp.mean(xc * xc, axis=-1, keepdims=True)
    return xc * lax.rsqrt(var + LN_EPS) * g + b


def _split3(x):
    hi = x.astype(BF16)
    r = x - hi.astype(F32)
    mid = r.astype(BF16)
    lo = (r - mid.astype(F32)).astype(BF16)
    return hi, mid, lo


def _params(sem):
    return pltpu.CompilerParams(dimension_semantics=sem, vmem_limit_bytes=VMEM_LIMIT)


def _inproj_kernel(x_ref, w_ref, b_ref, o_ref, xb_ref):
    @pl.when(pl.program_id(1) == 0)
    def _():
        xb_ref[...] = x_ref[...].astype(BF16)

    o_ref[...] = _dot(xb_ref[...], w_ref[...].astype(BF16)) + b_ref[...]


def _inproj(x, w_in, b_in, layer):
    nt, d = x.shape
    n_in = w_in.shape[2]
    tm, tn = INPROJ_TM, INPROJ_TN
    return pl.pallas_call(
        _inproj_kernel,
        grid=(nt // tm, pl.cdiv(n_in, tn)),
        in_specs=[
            pl.BlockSpec((tm, d), lambda i, j: (i, 0)),
            pl.BlockSpec((None, d, tn), lambda i, j: (layer, 0, j)),
            pl.BlockSpec((1, tn), lambda i, j: (0, j)),
        ],
        out_specs=pl.BlockSpec((tm, tn), lambda i, j: (i, j)),
        out_shape=jax.ShapeDtypeStruct((nt, n_in), F32),
        scratch_shapes=[pltpu.VMEM((tm, d), BF16)],
        compiler_params=_params(("parallel", "arbitrary")),
        name="inproj",
    )(x, w_in, b_in[layer][None, :])


def _mix_abc_kernel(aval, agate, bu, bv, cb, cc, cx, caw, cab, lag, lab, lvg, lvb, sw, sb, ccw,
                    out_ref, sta_ref, stc_ref, abuf, cbuf, *, tt):
    t = pl.program_id(1)
    halo_a = 32
    halo_c = 8

    @pl.when(t == 0)
    def _():
        abuf[0:halo_a, :] = jnp.zeros((halo_a, D_GRP), F32)
        cbuf[0:halo_c, :] = jnp.zeros((halo_c, D_GRP), F32)

    abuf[halo_a:halo_a + tt, :] = aval[...] * jax.nn.sigmoid(agate[...])
    rc = 64
    off_a = halo_a - (CONV_A_WIDTH - 1)
    for r in range(tt // rc):
        acc = jnp.zeros((rc, D_GRP), F32)
        for k in range(CONV_A_WIDTH):
            acc = acc + caw[k:k + 1, :] * abuf[r * rc + off_a + k:r * rc + off_a + k + rc, :]
        y = _layer_norm(acc + cab[...], lag[...], lab[...])
        out_ref[r * rc:(r + 1) * rc, 0:D_GRP] = (y * jax.nn.sigmoid(y)).astype(BF16)
    sta_ref[...] = abuf[tt:tt + halo_a, :]
    abuf[0:halo_a, :] = abuf[tt:tt + halo_a, :]

    row = lax.broadcasted_iota(jnp.int32, (CHUNK, CHUNK), 0)
    col = lax.broadcasted_iota(jnp.int32, (CHUNK, CHUNK), 1)
    for c in range(tt // CHUNK):
        rows = slice(c * CHUNK, (c + 1) * CHUNK)
        v = _layer_norm(jax.nn.gelu(bv[rows, :]), lvg[...], lvb[...]).astype(BF16)
        gu = jax.nn.gelu(bu[rows, :])
        for h in range(GMLP_HEADS):
            lanes = slice(h * CHUNK, (h + 1) * CHUNK)
            wh = jnp.where(col <= row, sw[h], 0.0).astype(BF16)
            s = _dot(wh, v[:, lanes]) + sb[:, lanes]
            out_ref[rows, D_GRP + h * CHUNK:D_GRP + (h + 1) * CHUNK] = (gu[:, lanes] * s).astype(BF16)

    cbuf[halo_c:halo_c + tt, :] = cc[...] * cx[...]
    off_c = halo_c - (CONV_C_WIDTH - 1)
    conv = jnp.zeros((tt, D_GRP), F32)
    for k in range(CONV_C_WIDTH):
        conv = conv + ccw[k:k + 1, :] * cbuf[off_c + k:off_c + k + tt, :]
    out_ref[:, 2 * D_GRP:3 * D_GRP] = (cb[...] * conv).astype(BF16)
    stc_ref[...] = cbuf[tt:tt + halo_c, :]
    cbuf[0:halo_c, :] = cbuf[tt:tt + halo_c, :]


def _mix_abc_prompt(h, p, batch, seq):
    tt = MIX_TT
    nt = seq // tt
    col_spec = lambda c: pl.BlockSpec((tt, D_GRP), lambda b, t, c=c: (b * nt + t, c))
    full = lambda a: pl.BlockSpec(a.shape, lambda b, t: (0,) * a.ndim)
    small = [p["conv_a_w"], p["conv_a_b"], p["ln_a_g"], p["ln_a_b"], p["ln_v_g"], p["ln_v_b"],
             p["spatial_w"], p["spatial_b_rows"], p["conv_c_w"]]
    return pl.pallas_call(
        functools.partial(_mix_abc_kernel, tt=tt),
        grid=(batch, nt),
        in_specs=[col_spec(c) for c in range(7)] + [full(a) for a in small],
        out_specs=[
            pl.BlockSpec((tt, 3 * D_GRP), lambda b, t: (b * nt + t, 0)),
            pl.BlockSpec((None, 32, D_GRP), lambda b, t: (b, 0, 0)),
            pl.BlockSpec((None, 8, D_GRP), lambda b, t: (b, 0, 0)),
        ],
        out_shape=[
            jax.ShapeDtypeStruct((batch * seq, 3 * D_GRP), BF16),
            jax.ShapeDtypeStruct((batch, 32, D_GRP), F32),
            jax.ShapeDtypeStruct((batch, 8, D_GRP), F32),
        ],
        scratch_shapes=[pltpu.VMEM((32 + tt, D_GRP), F32), pltpu.VMEM((8 + tt, D_GRP), F32)],
        compiler_params=_params(("parallel", "arbitrary")),
        name="mix_abc_prompt",
    )(*([h] * 7), *small)


def _place_head(q_ref, hh, g, lane_half):
    tile = q_ref[:, (hh // 2) * LANES:(hh // 2 + 1) * LANES]
    if hh % 2 != g:
        tile = pltpu.roll(tile, HEAD_DIM, axis=1)
    return jnp.where(lane_half == g, tile * SCALE, 0.0).astype(BF16)


def _nsa_prompt_kernel(q_ref, g_ref, kvc_ref, kvs_ref, kvw_ref, posk_ref, posv_ref, pk_ref, pv_ref, pair_ref,
                       out_ref,
                       kcmp, vcmp, ks, vs, kw, vw, m_s, l_s, acc_s, sc_s, *, seq):
    i = pl.program_id(1)
    tq_n = Q_TILE
    n_cmp = seq // CMP_BLOCK
    n_sel = seq // SEL_BLOCK
    rows_all = HEADS_PER_KV * tq_n

    @pl.when(i == 0)
    def _():
        step = 256
        for r in range(seq // step):
            rows = slice(r * step, (r + 1) * step)
            blk = step // CMP_BLOCK
            kb = (kvc_ref[rows, 0:LANES].reshape(blk, CMP_BLOCK, LANES) * posk_ref[...][None]).sum(axis=1)
            vb = (kvc_ref[rows, LANES:2 * LANES].reshape(blk, CMP_BLOCK, LANES) * posv_ref[...][None]).sum(axis=1)
            kcmp[r * blk:(r + 1) * blk, :] = _dot(kb.astype(BF16), pk_ref[...].astype(BF16)).astype(BF16)
            vcmp[r * blk:(r + 1) * blk, :] = _dot(vb.astype(BF16), pv_ref[...].astype(BF16)).astype(BF16)
            ks[rows, :] = kvs_ref[rows, 0:LANES].astype(BF16)
            vs[rows, :] = kvs_ref[rows, LANES:2 * LANES].astype(BF16)
            kw[WINDOW + r * step:WINDOW + (r + 1) * step, :] = kvw_ref[rows, 0:LANES].astype(BF16)
            vw[WINDOW + r * step:WINDOW + (r + 1) * step, :] = kvw_ref[rows, LANES:2 * LANES].astype(BF16)
        kw[0:WINDOW, :] = jnp.zeros((WINDOW, LANES), BF16)
        vw[0:WINDOW, :] = jnp.zeros((WINDOW, LANES), BF16)

    t0 = i * tq_n
    tq = t0 + lax.broadcasted_iota(jnp.int32, (tq_n, 1), 0)
    lane_half = lax.broadcasted_iota(jnp.int32, (tq_n, LANES), 1) // HEAD_DIM
    gate = jax.nn.sigmoid(g_ref[...])

    n_idx = lax.broadcasted_iota(jnp.int32, (1, n_cmp), 1)
    ok_c = (n_idx * CMP_BLOCK + (CMP_BLOCK - 1)) <= tq
    dist_c = tq.astype(F32) - (n_idx.astype(F32) * CMP_BLOCK + 0.5 * (CMP_BLOCK - 1))
    tq_l = t0 + lax.broadcasted_iota(jnp.int32, (1, tq_n), 1)
    cur = tq_l // SEL_BLOCK
    j_idx = lax.broadcasted_iota(jnp.int32, (n_sel, tq_n), 0)
    ok_s = j_idx <= cur
    forced = (j_idx == 0) | (j_idx >= cur - 1)
    n_win = WINDOW + tq_n
    off_w = lax.broadcasted_iota(jnp.int32, (1, n_win), 1)
    dist_w = lax.broadcasted_iota(jnp.int32, (tq_n, 1), 0) + WINDOW - off_w
    ok_w = (dist_w >= 0) & (dist_w <= WINDOW) & ((t0 - WINDOW + off_w) >= 0)
    dist_wf = dist_w.astype(F32)

    for g in range(KV_HEADS):
        qg = jnp.concatenate([_place_head(q_ref, g * HEADS_PER_KV + h, g, lane_half)
                              for h in range(HEADS_PER_KV)], axis=0)

        s_all = _dot_nt(qg, kcmp[...])
        psum = jnp.zeros((tq_n, n_cmp), F32)
        p_list = []
        for h in range(HEADS_PER_KV):
            s = s_all[h * tq_n:(h + 1) * tq_n] - SLOPES[g * HEADS_PER_KV + h] * dist_c
            s = jnp.where(ok_c, s, NEG_INF)
            e = jnp.exp(s - jnp.max(s, axis=1, keepdims=True))
            p = jnp.where(ok_c, e / jnp.sum(e, axis=1, keepdims=True), 0.0)
            psum = psum + p
            p_list.append(p.astype(BF16))
        o_c = _dot(jnp.concatenate(p_list, axis=0), vcmp[...])

        hi, mid, lo = _split3(psum)
        pair = pair_ref[...]
        imp_t = _dot_nt(pair, hi) + _dot_nt(pair, mid) + _dot_nt(pair, lo)
        score = jnp.where(ok_s & forced, FORCE_SCORE, jnp.where(ok_s, imp_t, NEG_INF))
        sc_s[...] = score
        rank = jnp.zeros((n_sel, tq_n), F32)
        for k in range(n_sel):
            rk = sc_s[k:k + 1, :]
            beats = (rk > score) | ((rk == score) & (j_idx > k))
            rank = rank + jnp.where(beats, 1.0, 0.0)
        sel_t = jnp.where(rank < float(min(N_SELECT, n_sel)), 1.0, 0.0)
        sel = jnp.concatenate([sel_t, jnp.zeros((LANES - n_sel, tq_n), F32)], axis=0).T.astype(BF16)

        m_s[...] = jnp.full((rows_all, LANES), NEG_INF, F32)
        l_s[...] = jnp.zeros((rows_all, LANES), F32)
        acc_s[...] = jnp.zeros((rows_all, LANES), F32)
        n_chunks = (t0 + tq_n + KEY_CHUNK - 1) // KEY_CHUNK
        blocks_per_chunk = KEY_CHUNK // SEL_BLOCK

        def chunk_body(c, carry):
            k0 = pl.multiple_of(c * KEY_CHUNK, KEY_CHUNK)
            kc = ks[pl.ds(k0, KEY_CHUNK), :]
            vc = vs[pl.ds(k0, KEY_CHUNK), :]
            s_all = _dot_nt(qg, kc)
            jj = lax.broadcasted_iota(jnp.int32, (LANES, KEY_CHUNK), 0)
            ss = lax.broadcasted_iota(jnp.int32, (LANES, KEY_CHUNK), 1)
            expand = jnp.where(jj == c * blocks_per_chunk + ss // SEL_BLOCK, 1.0, 0.0).astype(BF16)
            in_sel = _dot(sel, expand)
            spos = k0 + lax.broadcasted_iota(jnp.int32, (1, KEY_CHUNK), 1)
            ndist = spos - tq
            valid = (in_sel > 0.5) & (ndist <= 0)
            ndist_f = ndist.astype(F32)
            for h in range(HEADS_PER_KV):
                rows = slice(h * tq_n, (h + 1) * tq_n)
                s = s_all[rows] + SLOPES[g * HEADS_PER_KV + h] * ndist_f
                s = jnp.where(valid, s, NEG_INF)
                m_old = m_s[rows, 0:1]
                m_new = jnp.maximum(m_old, jnp.max(s, axis=1, keepdims=True))
                alpha = jnp.exp(m_old - m_new)
                p = jnp.where(valid, jnp.exp(s - m_new), 0.0)
                l_s[rows, :] = jnp.broadcast_to(alpha * l_s[rows, 0:1] + jnp.sum(p, axis=1, keepdims=True),
                                                (tq_n, LANES))
                acc_s[rows, :] = alpha * acc_s[rows, :] + _dot(p.astype(BF16), vc)
                m_s[rows, :] = jnp.broadcast_to(m_new, (tq_n, LANES))
            return carry

        lax.fori_loop(0, n_chunks, chunk_body, 0)
        o_s = acc_s[...] / l_s[...]

        w0 = pl.multiple_of(t0, Q_TILE)
        s_all = _dot_nt(qg, kw[pl.ds(w0, n_win), :])
        p_list = []
        for h in range(HEADS_PER_KV):
            s = s_all[h * tq_n:(h + 1) * tq_n] - SLOPES[g * HEADS_PER_KV + h] * dist_wf
            s = jnp.where(ok_w, s, NEG_INF)
            e = jnp.exp(s - jnp.max(s, axis=1, keepdims=True))
            p_list.append((e / jnp.sum(e, axis=1, keepdims=True)).astype(BF16))
        o_w = _dot(jnp.concatenate(p_list, axis=0), vw[pl.ds(w0, n_win), :])

        for h in range(HEADS_PER_KV):
            hh = g * HEADS_PER_KV + h
            rows = slice(h * tq_n, (h + 1) * tq_n)
            o = (gate[:, hh:hh + 1] * o_c[rows]
                 + gate[:, NSA_HEADS + hh:NSA_HEADS + hh + 1] * o_s[rows]
                 + gate[:, 2 * NSA_HEADS + hh:2 * NSA_HEADS + hh + 1] * o_w[rows])
            o = o[:, g * HEAD_DIM:(g + 1) * HEAD_DIM]
            out_ref[:, hh * HEAD_DIM:(hh + 1) * HEAD_DIM] = o.astype(BF16)


def _nsa_prompt(h, p, batch, seq):
    nq = seq // Q_TILE
    n_cmp = seq // CMP_BLOCK
    n_sel = seq // SEL_BLOCK
    full = lambda a: pl.BlockSpec(a.shape, lambda b, i: (0,) * a.ndim)
    pair = (jnp.arange(n_cmp)[None, :] // (SEL_BLOCK // CMP_BLOCK) == jnp.arange(n_sel)[:, None]).astype(BF16)
    small = [p["posk_rows"], p["posv_rows"], p["projk_bd"], p["projv_bd"], pair]
    kv_spec = lambda br: pl.BlockSpec((seq, 2 * LANES), lambda b, i, br=br: (b, COL_KV // (2 * LANES) + br))
    return pl.pallas_call(
        functools.partial(_nsa_prompt_kernel, seq=seq),
        grid=(batch, nq),
        in_specs=[
            pl.BlockSpec((Q_TILE, D_GRP), lambda b, i: (b * nq + i, COL_Q // D_GRP)),
            pl.BlockSpec((Q_TILE, LANES), lambda b, i: (b * nq + i, COL_G // LANES)),
            kv_spec(0), kv_spec(1), kv_spec(2),
        ] + [full(a) for a in small],
        out_specs=pl.BlockSpec((Q_TILE, D_GRP), lambda b, i: (b * nq + i, 0)),
        out_shape=jax.ShapeDtypeStruct((batch * seq, D_GRP), BF16),
        scratch_shapes=[
            pltpu.VMEM((n_cmp, LANES), BF16), pltpu.VMEM((n_cmp, LANES), BF16),
            pltpu.VMEM((seq, LANES), BF16), pltpu.VMEM((seq, LANES), BF16),
            pltpu.VMEM((seq + WINDOW, LANES), BF16), pltpu.VMEM((seq + WINDOW, LANES), BF16),
            pltpu.VMEM((HEADS_PER_KV * Q_TILE, LANES), F32),
            pltpu.VMEM((HEADS_PER_KV * Q_TILE, LANES), F32),
            pltpu.VMEM((HEADS_PER_KV * Q_TILE, LANES), F32),
            pltpu.VMEM((n_sel, Q_TILE), F32),
        ],
        compiler_params=_params(("parallel", "arbitrary")),
        name="nsa_prompt",
    )(h, h, h, h, h, *small)


def _mix_abc_sample_kernel(hs, sta, stc, caw, cab, lag, lab, lvg, lvb, sw0, sb0, ccw,
                           out_ref, ain_ref, ccx_ref, v_ref):
    g = D_GRP
    a_in = hs[:, 0:g] * jax.nn.sigmoid(hs[:, g:2 * g])
    acc = caw[CONV_A_WIDTH - 1:CONV_A_WIDTH, :] * a_in
    for k in range(CONV_A_WIDTH - 1):
        acc = acc + caw[k:k + 1, :] * sta[k]
    y = _layer_norm(acc + cab[...], lag[...], lab[...])
    out_ref[:, 0:g] = (y * jax.nn.sigmoid(y)).astype(BF16)
    ain_ref[...] = a_in

    v = _layer_norm(jax.nn.gelu(hs[:, 3 * g:4 * g]), lvg[...], lvb[...])
    v_ref[...] = v
    s = sw0[...].astype(BF16).astype(F32) * v.astype(BF16).astype(F32) + sb0[...]
    out_ref[:, g:2 * g] = (jax.nn.gelu(hs[:, 2 * g:3 * g]) * s).astype(BF16)

    ccx = hs[:, 5 * g:6 * g] * hs[:, 6 * g:7 * g]
    ccx_ref[...] = ccx
    conv = ccw[CONV_C_WIDTH - 1:CONV_C_WIDTH, :] * ccx
    for k in range(CONV_C_WIDTH - 1):
        conv = conv + ccw[k:k + 1, :] * stc[k]
    out_ref[:, 2 * g:3 * g] = (hs[:, 4 * g:5 * g] * conv).astype(BF16)


def _mix_abc_sample(hs_abc, sta_t, stc_t, p):
    bs = hs_abc.shape[0]
    args = [hs_abc, sta_t, stc_t, p["conv_a_w"], p["conv_a_b"], p["ln_a_g"], p["ln_a_b"], p["ln_v_g"], p["ln_v_b"],
            p["spatial_w00"], p["spatial_b0"], p["conv_c_w"]]
    full = lambda a: pl.BlockSpec(a.shape, lambda i: (0,) * a.ndim)
    return pl.pallas_call(
        _mix_abc_sample_kernel,
        grid=(1,),
        in_specs=[full(a) for a in args],
        out_specs=[pl.BlockSpec((bs, 3 * D_GRP), lambda i: (0, 0))] + [pl.BlockSpec((bs, D_GRP), lambda i: (0, 0))] * 3,
        out_shape=[jax.ShapeDtypeStruct((bs, 3 * D_GRP), BF16)] + [jax.ShapeDtypeStruct((bs, D_GRP), F32)] * 3,
        compiler_params=_params(("arbitrary",)),
        name="mix_abc_sample",
    )(*args)


def _nsa_dec_cmp_kernel(pt_ref, qm_ref, slope_ref, posk_ref, posv_ref, pk_ref, pv_ref, pair_ref, *rest,
                        past_len, pages_per_step):
    pages = rest[:pages_per_step]
    oc_ref, sel_ref, kbar, vbar = rest[pages_per_step:]
    pc = pl.program_id(1)
    n_cmp = past_len // CMP_BLOCK
    n_sel = past_len // SEL_BLOCK + 1
    blk_pp = PAGE_SIZE // CMP_BLOCK
    for kk in range(pages_per_step // 2):
        two_k = jnp.concatenate([pages[2 * kk][:, 0:LANES], pages[2 * kk + 1][:, 0:LANES]], axis=0)
        two_v = jnp.concatenate([pages[2 * kk][:, LANES:2 * LANES], pages[2 * kk + 1][:, LANES:2 * LANES]], axis=0)
        kb = (two_k.reshape(2 * blk_pp, CMP_BLOCK, LANES) * posk_ref[...][None]).sum(axis=1)
        vb = (two_v.reshape(2 * blk_pp, CMP_BLOCK, LANES) * posv_ref[...][None]).sum(axis=1)
        r0 = pl.multiple_of(pc * (pages_per_step * blk_pp) + kk * 2 * blk_pp, SUBLANES)
        kbar[pl.ds(r0, 2 * blk_pp), :] = kb
        vbar[pl.ds(r0, 2 * blk_pp), :] = vb

    @pl.when(pc == pl.num_programs(1) - 1)
    def _():
        kc = _dot(kbar[...].astype(BF16), pk_ref[...].astype(BF16)).astype(BF16)
        vc = _dot(vbar[...].astype(BF16), pv_ref[...].astype(BF16)).astype(BF16)
        q = (qm_ref[...] * SCALE).astype(BF16)
        slope = slope_ref[:, 0:1]
        n_idx = lax.broadcasted_iota(jnp.int32, (1, n_cmp), 1)
        center = n_idx.astype(F32) * CMP_BLOCK + 0.5 * (CMP_BLOCK - 1)
        ok_c = (n_idx * CMP_BLOCK + (CMP_BLOCK - 1)) <= past_len
        s = _dot_nt(q, kc) - slope * (float(past_len) - center)
        s = jnp.where(ok_c, s, NEG_INF)
        e = jnp.exp(s - jnp.max(s, axis=1, keepdims=True))
        p = jnp.where(ok_c, e / jnp.sum(e, axis=1, keepdims=True), 0.0)
        oc_ref[...] = _dot(p.astype(BF16), vc)
        psum = jnp.concatenate(
            [jnp.broadcast_to(jnp.sum(p[g * HEADS_PER_KV:(g + 1) * HEADS_PER_KV], axis=0, keepdims=True),
                              (HEADS_PER_KV, n_cmp)) for g in range(KV_HEADS)], axis=0)
        hi, mid, lo = _split3(psum)
        pair = pair_ref[...]
        imp = _dot(hi, pair) + _dot(mid, pair) + _dot(lo, pair)
        n_lanes = imp.shape[1]
        j_idx = lax.broadcasted_iota(jnp.int32, (1, n_lanes), 1)
        cur = past_len // SEL_BLOCK
        real = j_idx < n_sel
        ok_s = j_idx <= cur
        forced = (j_idx == 0) | (j_idx >= cur - 1)
        score = jnp.where(ok_s & forced, FORCE_SCORE, jnp.where(ok_s, imp, NEG_INF))
        score = jnp.where(real, score, NOT_A_BLOCK)
        rank = jnp.zeros(score.shape, F32)
        for k in range(n_sel):
            sk = score[:, k:k + 1]
            beats = (sk > score) | ((sk == score) & (j_idx > k))
            rank = rank + jnp.where(beats, 1.0, 0.0)
        sel_ref[...] = jnp.where((rank < float(min(N_SELECT, n_sel))) & real, 1.0, 0.0)


def _nsa_dec_cmp(page_table, qm, cache, p, layer, past_len):
    bs = qm.shape[0]
    n_pages = past_len // PAGE_SIZE
    pps = PAGES_PER_STEP
    n_cmp = past_len // CMP_BLOCK
    n_sel = past_len // SEL_BLOCK + 1
    n_lanes = -(-n_sel // LANES) * LANES
    pair = (jnp.arange(n_cmp)[:, None] // (SEL_BLOCK // CMP_BLOCK) == jnp.arange(n_lanes)[None, :]).astype(BF16)
    small = [p["slope_rows"], p["posk_rows"], p["posv_rows"], p["projk_bd"], p["projv_bd"], pair]
    full = lambda a: pl.BlockSpec(a.shape, lambda b, pc, pt: (0,) * a.ndim)
    page_spec = lambda k: pl.BlockSpec((None, None, PAGE_SIZE, 2 * LANES),
                                       lambda b, pc, pt, k=k: (layer, pt[b, pc * pps + k], 0, 0))
    grid_spec = pltpu.PrefetchScalarGridSpec(
        num_scalar_prefetch=1,
        grid=(bs, n_pages // pps),
        in_specs=[pl.BlockSpec((None, NSA_HEADS, LANES), lambda b, pc, pt: (b, 0, 0))]
                 + [full(a) for a in small] + [page_spec(k) for k in range(pps)],
        out_specs=[pl.BlockSpec((None, NSA_HEADS, LANES), lambda b, pc, pt: (b, 0, 0)),
                   pl.BlockSpec((None, NSA_HEADS, n_lanes), lambda b, pc, pt: (b, 0, 0))],
        scratch_shapes=[pltpu.VMEM((n_cmp, LANES), F32), pltpu.VMEM((n_cmp, LANES), F32)],
    )
    return pl.pallas_call(
        functools.partial(_nsa_dec_cmp_kernel, past_len=past_len, pages_per_step=pps),
        grid_spec=grid_spec,
        out_shape=[jax.ShapeDtypeStruct((bs, NSA_HEADS, LANES), F32),
                   jax.ShapeDtypeStruct((bs, NSA_HEADS, n_lanes), F32)],
        compiler_params=_params(("parallel", "arbitrary")),
        name="nsa_dec_cmp",
    )(page_table, qm, *small, *([cache] * pps))


def _nsa_dec_sel_kernel(pt_ref, qm_ref, slope_ref, selr_ref, selnew_ref, knew_ref, vnew_ref, kwnew_ref, vwnew_ref,
                        win_ref, oc_ref, graw_ref, *rest, past_len, pages_per_step):
    pages = rest[:pages_per_step]
    out_ref, m_s, l_s, acc_s = rest[pages_per_step:]
    pc = pl.program_id(1)
    keys_per_step = pages_per_step * PAGE_SIZE
    q = (qm_ref[...] * SCALE).astype(BF16)
    qf = q.astype(F32)
    slope = slope_ref[:, 0:1]

    @pl.when(pc == 0)
    def _():
        m_s[...] = jnp.full(m_s.shape, NEG_INF, F32)
        l_s[...] = jnp.zeros(l_s.shape, F32)
        acc_s[...] = jnp.zeros(acc_s.shape, F32)

    lane = lax.broadcasted_iota(jnp.int32, (1, PAGE_SIZE), 1)
    selr = selr_ref[...]
    blocks_pp = PAGE_SIZE // SEL_BLOCK
    s_parts, ok_parts = [], []
    for k in range(pages_per_step):
        kpage = pages[k][:, 0:LANES].astype(BF16)
        s = _dot_nt(q, kpage)
        spos = pc * keys_per_step + k * PAGE_SIZE + lane
        dist = past_len - spos
        in_sel = selr[:, blocks_pp * k:blocks_pp * k + 1]
        for bb in range(1, blocks_pp):
            in_sel = jnp.where(lane // SEL_BLOCK == bb, selr[:, blocks_pp * k + bb:blocks_pp * k + bb + 1], in_sel)
        s_parts.append(s - slope * dist.astype(F32))
        ok_parts.append((in_sel > 0.5) & (dist >= 0))
    ok = jnp.concatenate(ok_parts, axis=1)
    s = jnp.where(ok, jnp.concatenate(s_parts, axis=1), NEG_INF)
    m_old = m_s[:, 0:1]
    m_new = jnp.maximum(m_old, jnp.max(s, axis=1, keepdims=True))
    alpha = jnp.exp(m_old - m_new)
    p = jnp.where(ok, jnp.exp(s - m_new), 0.0)
    l_new = alpha * l_s[:, 0:1] + jnp.sum(p, axis=1, keepdims=True)
    pb = p.astype(BF16)
    acc = alpha * acc_s[...]
    for k in range(pages_per_step):
        acc = acc + _dot(pb[:, k * PAGE_SIZE:(k + 1) * PAGE_SIZE], pages[k][:, LANES:2 * LANES].astype(BF16))
    acc_s[...] = acc
    m_s[...] = jnp.broadcast_to(m_new, m_s.shape)
    l_s[...] = jnp.broadcast_to(l_new, l_s.shape)

    @pl.when(pc == pl.num_programs(1) - 1)
    def _():
        kn = knew_ref[...].astype(BF16).astype(F32)
        vn = vnew_ref[...].astype(BF16).astype(F32)
        s_n = jnp.sum(qf * kn, axis=1, keepdims=True)
        ok_n = selnew_ref[:, 0:1] > 0.5
        s_n = jnp.where(ok_n, s_n, NEG_INF)
        m_o = m_s[:, 0:1]
        m_f = jnp.maximum(m_o, s_n)
        a_f = jnp.exp(m_o - m_f)
        p_n = jnp.where(ok_n, jnp.exp(s_n - m_f), 0.0)
        l_f = a_f * l_s[:, 0:1] + p_n
        o_s = (a_f * acc_s[...] + p_n.astype(BF16).astype(F32) * vn) / l_f

        w_buf = win_ref.shape[0]
        idx = lax.broadcasted_iota(jnp.int32, (1, w_buf), 1)
        dist_w = w_buf - idx
        ok_w = (dist_w <= WINDOW) & ((past_len - dist_w) >= 0)
        s_w = _dot_nt(q, win_ref[:, 0:LANES].astype(BF16)) - slope * dist_w.astype(F32)
        s_w = jnp.where(ok_w, s_w, NEG_INF)
        kwn = kwnew_ref[...].astype(BF16).astype(F32)
        vwn = vwnew_ref[...].astype(BF16).astype(F32)
        s_wn = jnp.sum(qf * kwn, axis=1, keepdims=True)
        m_w = jnp.maximum(jnp.max(s_w, axis=1, keepdims=True), s_wn)
        e_w = jnp.exp(s_w - m_w)
        e_wn = jnp.exp(s_wn - m_w)
        den = jnp.sum(e_w, axis=1, keepdims=True) + e_wn
        p_w = (e_w / den).astype(BF16)
        p_wn = (e_wn / den).astype(BF16).astype(F32)
        o_w = _dot(p_w, win_ref[:, LANES:2 * LANES].astype(BF16)) + p_wn * vwn

        gate = jax.nn.sigmoid(graw_ref[...])
        out_ref[...] = gate[:, 0:1] * oc_ref[...] + gate[:, 1:2] * o_s + gate[:, 2:3] * o_w


def _nsa_dec_sel(page_table, qm, selr, selnew, knew_s, vnew_s, knew_w, vnew_w, win, oc, graw, cache, p, layer,
                 past_len):
    bs = qm.shape[0]
    n_pages = past_len // PAGE_SIZE
    pps = PAGES_PER_STEP
    w_buf = win.shape[2]
    per_b = lambda a: pl.BlockSpec((None,) + a.shape[1:], lambda b, pc, pt: (b,) + (0,) * (a.ndim - 1))
    page_spec = lambda k: pl.BlockSpec((None, None, PAGE_SIZE, 2 * LANES),
                                       lambda b, pc, pt, k=k: (layer, pt[b, pc * pps + k], 0, 0))
    slope = p["slope_rows"]
    grid_spec = pltpu.PrefetchScalarGridSpec(
        num_scalar_prefetch=1,
        grid=(bs, n_pages // pps),
        in_specs=[
            per_b(qm),
            pl.BlockSpec(slope.shape, lambda b, pc, pt: (0, 0)),
            pl.BlockSpec((None, None, NSA_HEADS, LANES), lambda b, pc, pt: (b, pc, 0, 0)),
            per_b(selnew), per_b(knew_s), per_b(vnew_s), per_b(knew_w), per_b(vnew_w),
            pl.BlockSpec((None, None, w_buf, 2 * LANES), lambda b, pc, pt: (layer, b, 0, 0)),
            per_b(oc), per_b(graw),
        ] + [page_spec(k) for k in range(pps)],
        out_specs=pl.BlockSpec((None, NSA_HEADS, LANES), lambda b, pc, pt: (b, 0, 0)),
        scratch_shapes=[pltpu.VMEM((NSA_HEADS, LANES), F32)] * 3,
    )
    return pl.pallas_call(
        functools.partial(_nsa_dec_sel_kernel, past_len=past_len, pages_per_step=pps),
        grid_spec=grid_spec,
        out_shape=jax.ShapeDtypeStruct((bs, NSA_HEADS, LANES), F32),
        compiler_params=_params(("parallel", "arbitrary")),
        name="nsa_dec_sel",
    )(page_table, qm, slope, selr, selnew, knew_s, vnew_s, knew_w, vnew_w, win, oc, graw, *([cache] * pps))


def _outproj_kernel(mabc, md, x, wo, g1, b1, wr, br, x1_ref, x1b_ref, ids_ref, wts_ref, *, alpha):
    k_abc = mabc.shape[1]
    mix = _dot(mabc[...], wo[0:k_abc, :]) + _dot(md[...], wo[k_abc:, :])
    x1 = _layer_norm(alpha * x[...] + mix, g1[...], b1[...])
    x1_ref[...] = x1
    x1b = x1.astype(BF16)
    x1b_ref[...] = x1b
    logits = _dot(x1b, wr[...]) + br[...]
    lane = lax.broadcasted_iota(jnp.int32, logits.shape, 1)
    is_g = lane < N_GROUPS
    gl = jnp.where(is_g, logits, -jnp.inf)
    gmax = jnp.max(gl, axis=1, keepdims=True)
    gsel = jnp.min(jnp.where(gl == gmax, lane, LANES), axis=1, keepdims=True)
    ggate = 1.0 / jnp.sum(jnp.where(is_g, jnp.exp(gl - gmax), 0.0), axis=1, keepdims=True)
    lo = N_GROUPS + gsel * EXPERTS_PER_GROUP
    el = jnp.where((lane >= lo) & (lane < lo + EXPERTS_PER_GROUP), logits, -jnp.inf)
    v1 = jnp.max(el, axis=1, keepdims=True)
    i1 = jnp.min(jnp.where(el == v1, lane, LANES), axis=1, keepdims=True)
    el2 = jnp.where(lane == i1, -jnp.inf, el)
    v2 = jnp.max(el2, axis=1, keepdims=True)
    i2 = jnp.min(jnp.where(el2 == v2, lane, LANES), axis=1, keepdims=True)
    e21 = jnp.exp(v2 - v1)
    w1 = ggate / (1.0 + e21)
    w2 = ggate * e21 / (1.0 + e21)
    ids_ref[...] = jnp.where(lane == 0, i1 - N_GROUPS, jnp.where(lane == 1, i2 - N_GROUPS, 0))
    wts_ref[...] = jnp.where(lane == 0, w1, jnp.where(lane == 1, w2, 0.0))


def _outproj(mabc, md, x, wo_bf, g1, b1, wr_bf, br, alpha):
    nt = x.shape[0]
    tm = TOK_TILE
    row = lambda w: pl.BlockSpec((tm, w), lambda i: (i, 0))
    full = lambda a: pl.BlockSpec(a.shape, lambda i: (0,) * a.ndim)
    return pl.pallas_call(
        functools.partial(_outproj_kernel, alpha=alpha),
        grid=(nt // tm,),
        in_specs=[row(mabc.shape[1]), row(md.shape[1]), row(D_MODEL), full(wo_bf), full(g1), full(b1), full(wr_bf),
                  full(br)],
        out_specs=[row(D_MODEL), row(D_MODEL), row(LANES), row(LANES)],
        out_shape=[jax.ShapeDtypeStruct((nt, D_MODEL), F32), jax.ShapeDtypeStruct((nt, D_MODEL), BF16),
                   jax.ShapeDtypeStruct((nt, LANES), jnp.int32), jax.ShapeDtypeStruct((nt, LANES), F32)],
        compiler_params=_params(("parallel",)),
        name="outproj_ln1_route",
    )(mabc, md, x, wo_bf, g1, b1, wr_bf, br)


def _moe_kernel(te_ref, tv_ref, x_ref, w_ref, wg_ref, wu_ref, wd_ref, o_ref, wgb, wub, wdb):
    i = pl.program_id(0)
    prev = te_ref[jnp.maximum(i - 1, 0)]

    @pl.when((i == 0) | (te_ref[i] != prev))
    def _():
        wgb[...] = wg_ref[...].astype(BF16)
        wub[...] = wu_ref[...].astype(BF16)
        wdb[...] = wd_ref[...].astype(BF16)

    @pl.when(tv_ref[i] > 0)
    def _():
        x = x_ref[...]
        hg = _dot(x, wgb[...])
        hu = _dot(x, wub[...])
        hidden = (hg * jax.nn.sigmoid(hg)) * hu * w_ref[...]
        o_ref[...] = _dot(hidden.astype(BF16), wdb[...])

    @pl.when(tv_ref[i] == 0)
    def _():
        o_ref[...] = jnp.zeros(o_ref.shape, F32)


def _moe(tile_expert, tile_valid, xg, slot_w, w_gate_e, w_up_e, w_down_e, layer):
    n_slots = xg.shape[0]
    tm = MOE_TM
    wspec = lambda r, c: pl.BlockSpec((None, None, r, c), lambda i, te, tv: (layer, te[i], 0, 0))
    grid_spec = pltpu.PrefetchScalarGridSpec(
        num_scalar_prefetch=2,
        grid=(n_slots // tm,),
        in_specs=[
            pl.BlockSpec((tm, D_MODEL), lambda i, te, tv: (i, 0)),
            pl.BlockSpec((tm, 1), lambda i, te, tv: (i, 0)),
            wspec(D_MODEL, D_EXPERT), wspec(D_MODEL, D_EXPERT), wspec(D_EXPERT, D_MODEL),
        ],
        out_specs=pl.BlockSpec((tm, D_MODEL), lambda i, te, tv: (i, 0)),
        scratch_shapes=[pltpu.VMEM((D_MODEL, D_EXPERT), BF16), pltpu.VMEM((D_MODEL, D_EXPERT), BF16),
                        pltpu.VMEM((D_EXPERT, D_MODEL), BF16)],
    )
    return pl.pallas_call(
        _moe_kernel,
        grid_spec=grid_spec,
        out_shape=jax.ShapeDtypeStruct((n_slots, D_MODEL), F32),
        compiler_params=_params(("arbitrary",)),
        name="moe_experts",
    )(tile_expert, tile_valid, xg, slot_w, w_gate_e, w_up_e, w_down_e)


def _ln2_kernel(x1, y0, y1, g2, b2, o_ref, *, alpha):
    o_ref[...] = _layer_norm(alpha * x1[...] + (y0[...] + y1[...]), g2[...], b2[...])


def _ln2(x1, y0, y1, g2, b2, alpha):
    nt = x1.shape[0]
    tm = TOK_TILE
    row = pl.BlockSpec((tm, D_MODEL), lambda i: (i, 0))
    full = lambda a: pl.BlockSpec(a.shape, lambda i: (0,) * a.ndim)
    return pl.pallas_call(
        functools.partial(_ln2_kernel, alpha=alpha),
        grid=(nt // tm,),
        in_specs=[row, row, row, full(g2), full(b2)],
        out_specs=row,
        out_shape=jax.ShapeDtypeStruct((nt, D_MODEL), F32),
        compiler_params=_params(("parallel",)),
        name="combine_ln2",
    )(x1, y0, y1, g2, b2)


def _route_slots(ids, wts, n_tok, n_slots):
    tm = MOE_TM
    e_flat = ids[:n_tok, :2].reshape(-1)
    w_flat = wts[:n_tok, :2].reshape(-1)
    n_pairs = e_flat.shape[0]
    order = jnp.argsort(e_flat, stable=True).astype(jnp.int32)
    e_sorted = e_flat[order]
    counts = jnp.sum(e_flat[:, None] == jnp.arange(N_EXPERTS, dtype=jnp.int32)[None, :], axis=0).astype(jnp.int32)
    padded = (counts + tm - 1) // tm * tm
    ends = jnp.cumsum(padded)
    offs = ends - padded
    starts = jnp.cumsum(counts) - counts
    slot_sorted = offs[e_sorted] + (jnp.arange(n_pairs, dtype=jnp.int32) - starts[e_sorted])
    slot_of_pair = jnp.zeros((n_pairs,), jnp.int32).at[order].set(slot_sorted)
    slot_token = jnp.zeros((n_slots,), jnp.int32).at[slot_sorted].set(order // 2)
    slot_w = jnp.zeros((n_slots,), F32).at[slot_sorted].set(w_flat[order])
    tile_start = jnp.arange(n_slots // tm, dtype=jnp.int32) * tm
    total = ends[-1]
    tile_valid = (tile_start < total).astype(jnp.int32)
    last_start = jnp.maximum(total - tm, 0)
    tile_expert = jnp.searchsorted(ends, jnp.minimum(tile_start, last_start), side="right").astype(jnp.int32)
    tile_expert = jnp.minimum(tile_expert, N_EXPERTS - 1)
    return slot_token, slot_w, slot_of_pair, tile_expert, tile_valid


def _layer(x, layer, n_prompt_rows, batch, seq, bs, past_len, page_table, caches, states, weights):
    (cache_cmp, cache_sel, state_win, state_conv_a, state_conv_c) = (caches[0], caches[1], states[0], states[1],
                                                                      states[2])
    w = weights
    nt = x.shape[0]
    n_tok = n_prompt_rows + bs
    depth = w["w_in"].shape[0]
    alpha = (2.0 * depth) ** 0.25
    rep_g = lambda a: jnp.repeat(a, HEAD_DIM, axis=1)
    bd = lambda a: jnp.zeros((LANES, LANES), F32).at[:HEAD_DIM, :HEAD_DIM].set(a[0]).at[HEAD_DIM:, HEAD_DIM:].set(a[1])
    row = lambda a: a[layer][None, :]
    p = {
        "conv_a_w": w["conv_a_w"][layer], "conv_a_b": row(w["conv_a_b"]),
        "ln_a_g": row(w["ln_a_g"]), "ln_a_b": row(w["ln_a_b"]), "ln_v_g": row(w["ln_v_g"]), "ln_v_b": row(w["ln_v_b"]),
        "spatial_w": w["spatial_w"][layer],
        "spatial_b_rows": jnp.repeat(w["spatial_b"][layer].T, CHUNK, axis=1),
        "spatial_w00": jnp.repeat(w["spatial_w"][layer][:, 0, 0], CHUNK)[None, :],
        "spatial_b0": jnp.repeat(w["spatial_b"][layer][:, 0], CHUNK)[None, :],
        "conv_c_w": w["conv_c_w"][layer],
        "posk_rows": rep_g(w["cmp_pos_k"][layer]), "posv_rows": rep_g(w["cmp_pos_v"][layer]),
        "projk_bd": bd(w["cmp_proj_k"][layer]), "projv_bd": bd(w["cmp_proj_v"][layer]),
        "slope_rows": jnp.broadcast_to(jnp.asarray(SLOPES, F32)[:, None], (NSA_HEADS, LANES)),
    }

    h = _inproj(x, w["w_in"], w["b_in"], layer)

    mabc_p, sta_p, stc_p = _mix_abc_prompt(h, p, batch, seq)
    md_p = _nsa_prompt(h, p, batch, seq)

    hs = h[n_prompt_rows:n_tok]
    sta_t = jnp.transpose(state_conv_a[layer], (1, 0, 2))
    stc_t = jnp.transpose(state_conv_c[layer], (1, 0, 2))
    mabc_s, a_in_s, ccx_s, v_s = _mix_abc_sample(hs[:, :7 * D_GRP], sta_t, stc_t, p)
    q_s = hs[:, COL_Q:COL_KV].reshape(bs, NSA_HEADS, HEAD_DIM)
    grp = jnp.arange(NSA_HEADS) // HEADS_PER_KV
    qm = jnp.where((jnp.arange(LANES)[None, :] // HEAD_DIM == grp[:, None])[None],
                   jnp.tile(q_s, (1, 1, KV_HEADS)), 0.0)
    kv_s = hs[:, COL_KV:COL_G].reshape(bs, 3, 2, LANES)
    graw = jnp.pad(jnp.transpose(hs[:, COL_G:N_IN].reshape(bs, 3, NSA_HEADS), (0, 2, 1)),
                   ((0, 0), (0, 0), (0, LANES - 3)))
    cache_cmp_r = cache_cmp.reshape(cache_cmp.shape[0], cache_cmp.shape[1], PAGE_SIZE, 2 * LANES)
    cache_sel_r = cache_sel.reshape(cache_sel.shape[0], cache_sel.shape[1], PAGE_SIZE, 2 * LANES)
    win_r = state_win.reshape(state_win.shape[0], bs, state_win.shape[2], 2 * LANES)
    oc, sel = _nsa_dec_cmp(page_table, qm, cache_cmp_r, p, layer, past_len)
    n_steps = past_len // PAGE_SIZE // PAGES_PER_STEP
    n_past_blocks = past_len // SEL_BLOCK
    blocks_per_step = n_past_blocks // n_steps
    selr = jnp.transpose(sel[:, :, :n_past_blocks].reshape(bs, NSA_HEADS, n_steps, blocks_per_step), (0, 2, 1, 3))
    selr = jnp.pad(selr, ((0, 0), (0, 0), (0, 0), (0, LANES - blocks_per_step)))
    selnew = jnp.broadcast_to(sel[:, :, n_past_blocks:n_past_blocks + 1], (bs, NSA_HEADS, LANES))
    d_s = _nsa_dec_sel(page_table, qm, selr, selnew, kv_s[:, 1, 0][:, None], kv_s[:, 1, 1][:, None],
                       kv_s[:, 2, 0][:, None], kv_s[:, 2, 1][:, None], win_r, oc, graw, cache_sel_r, p, layer,
                       past_len)
    d_s = d_s.reshape(bs, NSA_HEADS, KV_HEADS, HEAD_DIM)
    md_s = jnp.concatenate([d_s[:, :HEADS_PER_KV, 0], d_s[:, HEADS_PER_KV:, 1]], axis=1).reshape(bs, D_GRP)

    pad_rows = nt - n_tok
    mabc = jnp.concatenate([mabc_p, mabc_s, jnp.zeros((pad_rows, 3 * D_GRP), BF16)], axis=0)
    md = jnp.concatenate([md_p, md_s.astype(BF16), jnp.zeros((pad_rows, D_GRP), BF16)], axis=0)
    wr = jnp.concatenate([w["w_group"][layer], w["w_router"][layer]], axis=1)
    wr = jnp.pad(wr, ((0, 0), (0, LANES - wr.shape[1]))).astype(BF16)
    br = jnp.pad(jnp.concatenate([w["b_group"][layer], w["b_router"][layer]]), (0, LANES - N_GROUPS - N_EXPERTS))[None]
    x1, x1b, ids, wts = _outproj(mabc, md, x, w["w_out"][layer].astype(BF16), row(w["ln1_g"]), row(w["ln1_b"]),
                                 wr, br, alpha)

    n_slots = (-(-(2 * n_tok) // MOE_TM) + N_EXPERTS) * MOE_TM
    slot_token, slot_w, slot_of_pair, tile_expert, tile_valid = _route_slots(ids, wts, n_tok, n_slots)
    xg = jnp.take(x1b, slot_token, axis=0)
    y_slots = _moe(tile_expert, tile_valid, xg, slot_w[:, None], w["w_gate_e"], w["w_up_e"], w["w_down_e"], layer)
    pair_slots = jnp.pad(slot_of_pair.reshape(n_tok, 2), ((0, pad_rows), (0, 0)))
    y0 = jnp.take(y_slots, pair_slots[:, 0], axis=0)
    y1 = jnp.take(y_slots, pair_slots[:, 1], axis=0)
    x2 = _ln2(x1, y0, y1, row(w["ln2_g"]), row(w["ln2_b"]), alpha)

    kv_p = h[:n_prompt_rows, COL_KV:COL_G].reshape(batch, seq, 3, 2, KV_HEADS, HEAD_DIM)
    w_buf = state_win.shape[2]
    kv_s6 = hs[:, COL_KV:COL_G].reshape(bs, 1, 3, 2, KV_HEADS, HEAD_DIM)
    st_p = (kv_p[:, :, 0], kv_p[:, :, 1], kv_p[:, seq - w_buf:, 2],
            sta_p[:, 32 - (CONV_A_WIDTH - 1):], stc_p[:, 8 - (CONV_C_WIDTH - 1):])
    st_s = (kv_s6[:, :, 0], kv_s6[:, :, 1],
            jnp.concatenate([state_win[layer][:, 1:], kv_s6[:, :, 2]], axis=1),
            jnp.concatenate([state_conv_a[layer][:, 1:], a_in_s[:, None]], axis=1),
            jnp.concatenate([state_conv_c[layer][:, 1:], ccx_s[:, None]], axis=1),
            v_s[:, None])
    return x2, st_p, st_s


def kernel(x_prompt, x_sample, cache_cmp_kv, cache_sel_kv, state_win_kv, state_conv_a, state_conv_c, page_table, w_in, b_in, conv_a_w, conv_a_b, ln_a_g, ln_a_b, ln_v_g, ln_v_b, spatial_w, spatial_b, conv_c_w, cmp_pos_k, cmp_pos_v, cmp_proj_k, cmp_proj_v, w_out, ln1_g, ln1_b, ln2_g, ln2_b, w_group, b_group, w_router, b_router, w_gate_e, w_up_e, w_down_e):
    batch, seq, d_model = x_prompt.shape
    bs, dec_seq, _ = x_sample.shape
    depth = w_in.shape[0]
    past_len = page_table.shape[1] * PAGE_SIZE
    assert d_model == D_MODEL and dec_seq == 1 and w_in.shape[2] == N_IN
    assert seq % KEY_CHUNK == 0 and seq % MIX_TT == 0
    assert past_len % (PAGE_SIZE * PAGES_PER_STEP) == 0 and state_win_kv.shape[2] == WINDOW and past_len >= WINDOW
    n_prompt_rows = batch * seq
    n_tok = n_prompt_rows + bs
    tile = math.lcm(TOK_TILE, INPROJ_TM, MOE_TM)
    nt = -(-n_tok // tile) * tile
    x = jnp.concatenate([x_prompt.reshape(n_prompt_rows, d_model), x_sample.reshape(bs, d_model),
                         jnp.zeros((nt - n_tok, d_model), F32)], axis=0)
    weights = dict(w_in=w_in, b_in=b_in, conv_a_w=conv_a_w, conv_a_b=conv_a_b, ln_a_g=ln_a_g, ln_a_b=ln_a_b,
                   ln_v_g=ln_v_g, ln_v_b=ln_v_b, spatial_w=spatial_w, spatial_b=spatial_b, conv_c_w=conv_c_w,
                   cmp_pos_k=cmp_pos_k, cmp_pos_v=cmp_pos_v, cmp_proj_k=cmp_proj_k, cmp_proj_v=cmp_proj_v,
                   w_out=w_out, ln1_g=ln1_g, ln1_b=ln1_b, ln2_g=ln2_g, ln2_b=ln2_b, w_group=w_group,
                   b_group=b_group, w_router=w_router, b_router=b_router, w_gate_e=w_gate_e, w_up_e=w_up_e,
                   w_down_e=w_down_e)
    st_p, st_s = [], []
    for layer in range(depth):
        x, sp, ss = _layer(x, layer, n_prompt_rows, batch, seq, bs, past_len, page_table,
                           (cache_cmp_kv, cache_sel_kv), (state_win_kv, state_conv_a, state_conv_c), weights)
        st_p.append(sp)
        st_s.append(ss)
    y_prompt = x[:n_prompt_rows].reshape(batch, seq, d_model)
    y_sample = x[n_prompt_rows:n_tok].reshape(bs, 1, d_model)
    return (y_prompt, y_sample,
            jnp.stack([s[0] for s in st_p]), jnp.stack([s[1] for s in st_p]), jnp.stack([s[2] for s in st_p]),
            jnp.stack([s[3] for s in st_p]), jnp.stack([s[4] for s in st_p]),
            jnp.stack([s[0] for s in st_s]), jnp.stack([s[1] for s in st_s]), jnp.stack([s[2] for s in st_s]),
            jnp.stack([s[3] for s in st_s]), jnp.stack([s[4] for s in st_s]), jnp.stack([s[5] for s in st_s]))
```

```python
import functools
import math

import jax
import jax.numpy as jnp
from jax import lax
from jax.experimental import pallas as pl
from jax.experimental.pallas import tpu as pltpu

F32 = jnp.float32
BF16 = jnp.bfloat16

D_MODEL = 2048
D_GRP = 512
NSA_HEADS = 8
HEAD_DIM = 64
KV_HEADS = 2
HEADS_PER_KV = 4
CONV_A_WIDTH = 31
CONV_C_WIDTH = 3
CHUNK = 128
GMLP_HEADS = 4
CMP_BLOCK = 32
SEL_BLOCK = 64
N_SELECT = 16
WINDOW = 512
PAGE_SIZE = 128
N_GROUPS = 4
EXPERTS_PER_GROUP = 8
N_EXPERTS = 32
D_EXPERT = 512
LN_EPS = 1e-5
NEG_INF = -1e30
FORCE_SCORE = 1e9
NOT_A_BLOCK = -3e38
N_IN = 4888
COL_Q = 3584
COL_KV = 4096
COL_G = 4864
SCALE = HEAD_DIM ** -0.5
SLOPES = tuple(2.0 ** (-(h + 1)) for h in range(NSA_HEADS))

LANES = 128
SUBLANES = 8
VMEM_LIMIT = 56 * 1024 * 1024

TOK_TILE = 256
INPROJ_TM = 768
INPROJ_TN = 512
MIX_TT = 256
Q_TILE = 128
KEY_CHUNK = 512
PAGES_PER_STEP = 16
MOE_TM = 256


def _dot(a, b):
    return jnp.dot(a, b, preferred_element_type=F32)


def _dot_nt(a, b):
    return lax.dot_general(a, b, (((1,), (1,)), ((), ())), preferred_element_type=F32)


def _layer_norm(x, g, b):
    mu = jnp.mean(x, axis=-1, keepdims=True)
    xc = x - mu
    var = jnp.mean(xc * xc, axis=-1, keepdims=True)
    return xc * lax.rsqrt(var + LN_EPS) * g + b


def _split3(x):
    hi = x.astype(BF16)
    r = x - hi.astype(F32)
    mid = r.astype(BF16)
    lo = (r - mid.astype(F32)).astype(BF16)
    return hi, mid, lo


def _params(sem):
    return pltpu.CompilerParams(dimension_semantics=sem, vmem_limit_bytes=VMEM_LIMIT)


def _inproj_kernel(x_ref, w_ref, b_ref, o_ref, xb_ref):
    @pl.when(pl.program_id(1) == 0)
    def _():
        xb_ref[...] = x_ref[...].astype(BF16)

    o_ref[...] = _dot(xb_ref[...], w_ref[...].astype(BF16)) + b_ref[...]


def _inproj(x, w_in, b_in, layer):
    nt, d = x.shape
    n_in = w_in.shape[2]
    tm, tn = INPROJ_TM, INPROJ_TN
    return pl.pallas_call(
        _inproj_kernel,
        grid=(nt // tm, pl.cdiv(n_in, tn)),
        in_specs=[
            pl.BlockSpec((tm, d), lambda i, j: (i, 0)),
            pl.BlockSpec((None, d, tn), lambda i, j: (layer, 0, j)),
            pl.BlockSpec((1, tn), lambda i, j: (0, j)),
        ],
        out_specs=pl.BlockSpec((tm, tn), lambda i, j: (i, j)),
        out_shape=jax.ShapeDtypeStruct((nt, n_in), F32),
        scratch_shapes=[pltpu.VMEM((tm, d), BF16)],
        compiler_params=_params(("parallel", "arbitrary")),
        name="inproj",
    )(x, w_in, b_in[layer][None, :])


def _mix_abc_kernel(aval, agate, bu, bv, cb, cc, cx, caw, cab, lag, lab, lvg, lvb, sw, sb, ccw,
                    out_ref, sta_ref, stc_ref, abuf, cbuf, *, tt):
    t = pl.program_id(1)
    halo_a = 32
    halo_c = 8

    @pl.when(t == 0)
    def _():
        abuf[0:halo_a, :] = jnp.zeros((halo_a, D_GRP), F32)
        cbuf[0:halo_c, :] = jnp.zeros((halo_c, D_GRP), F32)

    abuf[halo_a:halo_a + tt, :] = aval[...] * jax.nn.sigmoid(agate[...])
    rc = 64
    off_a = halo_a - (CONV_A_WIDTH - 1)
    for r in range(tt // rc):
        acc = jnp.zeros((rc, D_GRP), F32)
        for k in range(CONV_A_WIDTH):
            acc = acc + caw[k:k + 1, :] * abuf[r * rc + off_a + k:r * rc + off_a + k + rc, :]
        y = _layer_norm(acc + cab[...], lag[...], lab[...])
        out_ref[r * rc:(r + 1) * rc, 0:D_GRP] = (y * jax.nn.sigmoid(y)).astype(BF16)
    sta_ref[...] = abuf[tt:tt + halo_a, :]
    abuf[0:halo_a, :] = abuf[tt:tt + halo_a, :]

    row = lax.broadcasted_iota(jnp.int32, (CHUNK, CHUNK), 0)
    col = lax.broadcasted_iota(jnp.int32, (CHUNK, CHUNK), 1)
    for c in range(tt // CHUNK):
        rows = slice(c * CHUNK, (c + 1) * CHUNK)
        v = _layer_norm(jax.nn.gelu(bv[rows, :]), lvg[...], lvb[...]).astype(BF16)
        gu = jax.nn.gelu(bu[rows, :])
        for h in range(GMLP_HEADS):
            lanes = slice(h * CHUNK, (h + 1) * CHUNK)
            wh = jnp.where(col <= row, sw[h], 0.0).astype(BF16)
            s = _dot(wh, v[:, lanes]) + sb[:, lanes]
            out_ref[rows, D_GRP + h * CHUNK:D_GRP + (h + 1) * CHUNK] = (gu[:, lanes] * s).astype(BF16)

    cbuf[halo_c:halo_c + tt, :] = cc[...] * cx[...]
    off_c = halo_c - (CONV_C_WIDTH - 1)
    conv = jnp.zeros((tt, D_GRP), F32)
    for k in range(CONV_C_WIDTH):
        conv = conv + ccw[k:k + 1, :] * cbuf[off_c + k:off_c + k + tt, :]
    out_ref[:, 2 * D_GRP:3 * D_GRP] = (cb[...] * conv).astype(BF16)
    stc_ref[...] = cbuf[tt:tt + halo_c, :]
    cbuf[0:halo_c, :] = cbuf[tt:tt + halo_c, :]


def _mix_abc_prompt(h, p, batch, seq):
    tt = MIX_TT
    nt = seq // tt
    col_spec = lambda c: pl.BlockSpec((tt, D_GRP), lambda b, t, c=c: (b * nt + t, c))
    full = lambda a: pl.BlockSpec(a.shape, lambda b, t: (0,) * a.ndim)
    small = [p["conv_a_w"], p["conv_a_b"], p["ln_a_g"], p["ln_a_b"], p["ln_v_g"], p["ln_v_b"],
             p["spatial_w"], p["spatial_b_rows"], p["conv_c_w"]]
    return pl.pallas_call(
        functools.partial(_mix_abc_kernel, tt=tt),
        grid=(batch, nt),
        in_specs=[col_spec(c) for c in range(7)] + [full(a) for a in small],
        out_specs=[
            pl.BlockSpec((tt, 3 * D_GRP), lambda b, t: (b * nt + t, 0)),
            pl.BlockSpec((None, 32, D_GRP), lambda b, t: (b, 0, 0)),
            pl.BlockSpec((None, 8, D_GRP), lambda b, t: (b, 0, 0)),
        ],
        out_shape=[
            jax.ShapeDtypeStruct((batch * seq, 3 * D_GRP), BF16),
            jax.ShapeDtypeStruct((batch, 32, D_GRP), F32),
            jax.ShapeDtypeStruct((batch, 8, D_GRP), F32),
        ],
        scratch_shapes=[pltpu.VMEM((32 + tt, D_GRP), F32), pltpu.VMEM((8 + tt, D_GRP), F32)],
        compiler_params=_params(("parallel", "arbitrary")),
        name="mix_abc_prompt",
    )(*([h] * 7), *small)


def _place_head(q_ref, hh, g, lane_half):
    tile = q_ref[:, (hh // 2) * LANES:(hh // 2 + 1) * LANES]
    if hh % 2 != g:
        tile = pltpu.roll(tile, HEAD_DIM, axis=1)
    return jnp.where(lane_half == g, tile * SCALE, 0.0).astype(BF16)


def _nsa_prompt_kernel(q_ref, g_ref, kvc_ref, kvs_ref, kvw_ref, posk_ref, posv_ref, pk_ref, pv_ref, pair_ref,
                       out_ref,
                       kcmp, vcmp, ks, vs, kw, vw, m_s, l_s, acc_s, sc_s, *, seq):
    i = pl.program_id(1)
    tq_n = Q_TILE
    n_cmp = seq // CMP_BLOCK
    n_sel = seq // SEL_BLOCK
    rows_all = HEADS_PER_KV * tq_n

    @pl.when(i == 0)
    def _():
        step = 256
        for r in range(seq // step):
            rows = slice(r * step, (r + 1) * step)
            blk = step // CMP_BLOCK
            kb = (kvc_ref[rows, 0:LANES].reshape(blk, CMP_BLOCK, LANES) * posk_ref[...][None]).sum(axis=1)
            vb = (kvc_ref[rows, LANES:2 * LANES].reshape(blk, CMP_BLOCK, LANES) * posv_ref[...][None]).sum(axis=1)
            kcmp[r * blk:(r + 1) * blk, :] = _dot(kb.astype(BF16), pk_ref[...].astype(BF16)).astype(BF16)
            vcmp[r * blk:(r + 1) * blk, :] = _dot(vb.astype(BF16), pv_ref[...].astype(BF16)).astype(BF16)
            ks[rows, :] = kvs_ref[rows, 0:LANES].astype(BF16)
            vs[rows, :] = kvs_ref[rows, LANES:2 * LANES].astype(BF16)
            kw[WINDOW + r * step:WINDOW + (r + 1) * step, :] = kvw_ref[rows, 0:LANES].astype(BF16)
            vw[WINDOW + r * step:WINDOW + (r + 1) * step, :] = kvw_ref[rows, LANES:2 * LANES].astype(BF16)
        kw[0:WINDOW, :] = jnp.zeros((WINDOW, LANES), BF16)
        vw[0:WINDOW, :] = jnp.zeros((WINDOW, LANES), BF16)

    t0 = i * tq_n
    tq = t0 + lax.broadcasted_iota(jnp.int32, (tq_n, 1), 0)
    lane_half = lax.broadcasted_iota(jnp.int32, (tq_n, LANES), 1) // HEAD_DIM
    gate = jax.nn.sigmoid(g_ref[...])

    n_idx = lax.broadcasted_iota(jnp.int32, (1, n_cmp), 1)
    ok_c = (n_idx * CMP_BLOCK + (CMP_BLOCK - 1)) <= tq
    dist_c = tq.astype(F32) - (n_idx.astype(F32) * CMP_BLOCK + 0.5 * (CMP_BLOCK - 1))
    tq_l = t0 + lax.broadcasted_iota(jnp.int32, (1, tq_n), 1)
    cur = tq_l // SEL_BLOCK
    j_idx = lax.broadcasted_iota(jnp.int32, (n_sel, tq_n), 0)
    ok_s = j_idx <= cur
    forced = (j_idx == 0) | (j_idx >= cur - 1)
    n_win = WINDOW + tq_n
    off_w = lax.broadcasted_iota(jnp.int32, (1, n_win), 1)
    dist_w = lax.broadcasted_iota(jnp.int32, (tq_n, 1), 0) + WINDOW - off_w
    ok_w = (dist_w >= 0) & (dist_w <= WINDOW) & ((t0 - WINDOW + off_w) >= 0)
    dist_wf = dist_w.astype(F32)

    for g in range(KV_HEADS):
        qg = jnp.concatenate([_place_head(q_ref, g * HEADS_PER_KV + h, g, lane_half)
                              for h in range(HEADS_PER_KV)], axis=0)

        s_all = _dot_nt(qg, kcmp[...])
        psum = jnp.zeros((tq_n, n_cmp), F32)
        p_list = []
        for h in range(HEADS_PER_KV):
            s = s_all[h * tq_n:(h + 1) * tq_n] - SLOPES[g * HEADS_PER_KV + h] * dist_c
            s = jnp.where(ok_c, s, NEG_INF)
            e = jnp.exp(s - jnp.max(s, axis=1, keepdims=True))
            p = jnp.where(ok_c, e / jnp.sum(e, axis=1, keepdims=True), 0.0)
            psum = psum + p
            p_list.append(p.astype(BF16))
        o_c = _dot(jnp.concatenate(p_list, axis=0), vcmp[...])

        hi, mid, lo = _split3(psum)
        pair = pair_ref[...]
        imp_t = _dot_nt(pair, hi) + _dot_nt(pair, mid) + _dot_nt(pair, lo)
        score = jnp.where(ok_s & forced, FORCE_SCORE, jnp.where(ok_s, imp_t, NEG_INF))
        sc_s[...] = score
        rank = jnp.zeros((n_sel, tq_n), F32)
        for k in range(n_sel):
            rk = sc_s[k:k + 1, :]
            beats = (rk > score) | ((rk == score) & (j_idx > k))
            rank = rank + jnp.where(beats, 1.0, 0.0)
        sel_t = jnp.where(rank < float(min(N_SELECT, n_sel)), 1.0, 0.0)
        sel = jnp.concatenate([sel_t, jnp.zeros((LANES - n_sel, tq_n), F32)], axis=0).T.astype(BF16)

        m_s[...] = jnp.full((rows_all, LANES), NEG_INF, F32)
        l_s[...] = jnp.zeros((rows_all, LANES), F32)
        acc_s[...] = jnp.zeros((rows_all, LANES), F32)
        n_chunks = (t0 + tq_n + KEY_CHUNK - 1) // KEY_CHUNK
        blocks_per_chunk = KEY_CHUNK // SEL_BLOCK

        def chunk_body(c, carry):
            k0 = pl.multiple_of(c * KEY_CHUNK, KEY_CHUNK)
            kc = ks[pl.ds(k0, KEY_CHUNK), :]
            vc = vs[pl.ds(k0, KEY_CHUNK), :]
            s_all = _dot_nt(qg, kc)
            jj = lax.broadcasted_iota(jnp.int32, (LANES, KEY_CHUNK), 0)
            ss = lax.broadcasted_iota(jnp.int32, (LANES, KEY_CHUNK), 1)
            expand = jnp.where(jj == c * blocks_per_chunk + ss // SEL_BLOCK, 1.0, 0.0).astype(BF16)
            in_sel = _dot(sel, expand)
            spos = k0 + lax.broadcasted_iota(jnp.int32, (1, KEY_CHUNK), 1)
            ndist = spos - tq
            valid = (in_sel > 0.5) & (ndist <= 0)
            ndist_f = ndist.astype(F32)
            for h in range(HEADS_PER_KV):
                rows = slice(h * tq_n, (h + 1) * tq_n)
                s = s_all[rows] + SLOPES[g * HEADS_PER_KV + h] * ndist_f
                s = jnp.where(valid, s, NEG_INF)
                m_old = m_s[rows, 0:1]
                m_new = jnp.maximum(m_old, jnp.max(s, axis=1, keepdims=True))
                alpha = jnp.exp(m_old - m_new)
                p = jnp.where(valid, jnp.exp(s - m_new), 0.0)
                l_s[rows, :] = jnp.broadcast_to(alpha * l_s[rows, 0:1] + jnp.sum(p, axis=1, keepdims=True),
                                                (tq_n, LANES))
                acc_s[rows, :] = alpha * acc_s[rows, :] + _dot(p.astype(BF16), vc)
                m_s[rows, :] = jnp.broadcast_to(m_new, (tq_n, LANES))
            return carry

        lax.fori_loop(0, n_chunks, chunk_body, 0)
        o_s = acc_s[...] / l_s[...]

        w0 = pl.multiple_of(t0, Q_TILE)
        s_all = _dot_nt(qg, kw[pl.ds(w0, n_win), :])
        p_list = []
        for h in range(HEADS_PER_KV):
            s = s_all[h * tq_n:(h + 1) * tq_n] - SLOPES[g * HEADS_PER_KV + h] * dist_wf
            s = jnp.where(ok_w, s, NEG_INF)
            e = jnp.exp(s - jnp.max(s, axis=1, keepdims=True))
            p_list.append((e / jnp.sum(e, axis=1, keepdims=True)).astype(BF16))
        o_w = _dot(jnp.concatenate(p_list, axis=0), vw[pl.ds(w0, n_win), :])

        for h in range(HEADS_PER_KV):
            hh = g * HEADS_PER_KV + h
            rows = slice(h * tq_n, (h + 1) * tq_n)
            o = (gate[:, hh:hh + 1] * o_c[rows]
                 + gate[:, NSA_HEADS + hh:NSA_HEADS + hh + 1] * o_s[rows]
                 + gate[:, 2 * NSA_HEADS + hh:2 * NSA_HEADS + hh + 1] * o_w[rows])
            o = o[:, g * HEAD_DIM:(g + 1) * HEAD_DIM]
            out_ref[:, hh * HEAD_DIM:(hh + 1) * HEAD_DIM] = o.astype(BF16)


def _nsa_prompt(h, p, batch, seq):
    nq = seq // Q_TILE
    n_cmp = seq // CMP_BLOCK
    n_sel = seq // SEL_BLOCK
    full = lambda a: pl.BlockSpec(a.shape, lambda b, i: (0,) * a.ndim)
    pair = (jnp.arange(n_cmp)[None, :] // (SEL_BLOCK // CMP_BLOCK) == jnp.arange(n_sel)[:, None]).astype(BF16)
    small = [p["posk_rows"], p["posv_rows"], p["projk_bd"], p["projv_bd"], pair]
    kv_spec = lambda br: pl.BlockSpec((seq, 2 * LANES), lambda b, i, br=br: (b, COL_KV // (2 * LANES) + br))
    return pl.pallas_call(
        functools.partial(_nsa_prompt_kernel, seq=seq),
        grid=(batch, nq),
        in_specs=[
            pl.BlockSpec((Q_TILE, D_GRP), lambda b, i: (b * nq + i, COL_Q // D_GRP)),
            pl.BlockSpec((Q_TILE, LANES), lambda b, i: (b * nq + i, COL_G // LANES)),
            kv_spec(0), kv_spec(1), kv_spec(2),
        ] + [full(a) for a in small],
        out_specs=pl.BlockSpec((Q_TILE, D_GRP), lambda b, i: (b * nq + i, 0)),
        out_shape=jax.ShapeDtypeStruct((batch * seq, D_GRP), BF16),
        scratch_shapes=[
            pltpu.VMEM((n_cmp, LANES), BF16), pltpu.VMEM((n_cmp, LANES), BF16),
            pltpu.VMEM((seq, LANES), BF16), pltpu.VMEM((seq, LANES), BF16),
            pltpu.VMEM((seq + WINDOW, LANES), BF16), pltpu.VMEM((seq + WINDOW, LANES), BF16),
            pltpu.VMEM((HEADS_PER_KV * Q_TILE, LANES), F32),
            pltpu.VMEM((HEADS_PER_KV * Q_TILE, LANES), F32),
            pltpu.VMEM((HEADS_PER_KV * Q_TILE, LANES), F32),
            pltpu.VMEM((n_sel, Q_TILE), F32),
        ],
        compiler_params=_params(("parallel", "arbitrary")),
        name="nsa_prompt",
    )(h, h, h, h, h, *small)


def _mix_abc_sample_kernel(hs, sta, stc, caw, cab, lag, lab, lvg, lvb, sw0, sb0, ccw,
                           out_ref, ain_ref, ccx_ref, v_ref):
    g = D_GRP
    a_in = hs[:, 0:g] * jax.nn.sigmoid(hs[:, g:2 * g])
    acc = caw[CONV_A_WIDTH - 1:CONV_A_WIDTH, :] * a_in
    for k in range(CONV_A_WIDTH - 1):
        acc = acc + caw[k:k + 1, :] * sta[k]
    y = _layer_norm(acc + cab[...], lag[...], lab[...])
    out_ref[:, 0:g] = (y * jax.nn.sigmoid(y)).astype(BF16)
    ain_ref[...] = a_in

    v = _layer_norm(jax.nn.gelu(hs[:, 3 * g:4 * g]), lvg[...], lvb[...])
    v_ref[...] = v
    s = sw0[...].astype(BF16).astype(F32) * v.astype(BF16).astype(F32) + sb0[...]
    out_ref[:, g:2 * g] = (jax.nn.gelu(hs[:, 2 * g:3 * g]) * s).astype(BF16)

    ccx = hs[:, 5 * g:6 * g] * hs[:, 6 * g:7 * g]
    ccx_ref[...] = ccx
    conv = ccw[CONV_C_WIDTH - 1:CONV_C_WIDTH, :] * ccx
    for k in range(CONV_C_WIDTH - 1):
        conv = conv + ccw[k:k + 1, :] * stc[k]
    out_ref[:, 2 * g:3 * g] = (hs[:, 4 * g:5 * g] * conv).astype(BF16)


def _mix_abc_sample(hs_abc, sta_t, stc_t, p):
    bs = hs_abc.shape[0]
    args = [hs_abc, sta_t, stc_t, p["conv_a_w"], p["conv_a_b"], p["ln_a_g"], p["ln_a_b"], p["ln_v_g"], p["ln_v_b"],
            p["spatial_w00"], p["spatial_b0"], p["conv_c_w"]]
    full = lambda a: pl.BlockSpec(a.shape, lambda i: (0,) * a.ndim)
    return pl.pallas_call(
        _mix_abc_sample_kernel,
        grid=(1,),
        in_specs=[full(a) for a in args],
        out_specs=[pl.BlockSpec((bs, 3 * D_GRP), lambda i: (0, 0))] + [pl.BlockSpec((bs, D_GRP), lambda i: (0, 0))] * 3,
        out_shape=[jax.ShapeDtypeStruct((bs, 3 * D_GRP), BF16)] + [jax.ShapeDtypeStruct((bs, D_GRP), F32)] * 3,
        compiler_params=_params(("arbitrary",)),
        name="mix_abc_sample",
    )(*args)


def _nsa_dec_cmp_kernel(pt_ref, qm_ref, slope_ref, posk_ref, posv_ref, pk_ref, pv_ref, pair_ref, *rest,
                        past_len, pages_per_step):
    pages = rest[:pages_per_step]
    oc_ref, sel_ref, kbar, vbar = rest[pages_per_step:]
    pc = pl.program_id(1)
    n_cmp = past_len // CMP_BLOCK
    n_sel = past_len // SEL_BLOCK + 1
    blk_pp = PAGE_SIZE // CMP_BLOCK
    rows_of = lambda page, kv: page[kv].reshape(LANES, PAGE_SIZE).T
    for kk in range(pages_per_step // 2):
        two_k = jnp.concatenate([rows_of(pages[2 * kk], 0), rows_of(pages[2 * kk + 1], 0)], axis=0)
        two_v = jnp.concatenate([rows_of(pages[2 * kk], 1), rows_of(pages[2 * kk + 1], 1)], axis=0)
        kb = (two_k.reshape(2 * blk_pp, CMP_BLOCK, LANES) * posk_ref[...][None]).sum(axis=1)
        vb = (two_v.reshape(2 * blk_pp, CMP_BLOCK, LANES) * posv_ref[...][None]).sum(axis=1)
        r0 = pl.multiple_of(pc * (pages_per_step * blk_pp) + kk * 2 * blk_pp, SUBLANES)
        kbar[pl.ds(r0, 2 * blk_pp), :] = kb
        vbar[pl.ds(r0, 2 * blk_pp), :] = vb

    @pl.when(pc == pl.num_programs(1) - 1)
    def _():
        kc = _dot(kbar[...].astype(BF16), pk_ref[...].astype(BF16)).astype(BF16)
        vc = _dot(vbar[...].astype(BF16), pv_ref[...].astype(BF16)).astype(BF16)
        q = (qm_ref[...] * SCALE).astype(BF16)
        slope = slope_ref[:, 0:1]
        n_idx = lax.broadcasted_iota(jnp.int32, (1, n_cmp), 1)
        center = n_idx.astype(F32) * CMP_BLOCK + 0.5 * (CMP_BLOCK - 1)
        ok_c = (n_idx * CMP_BLOCK + (CMP_BLOCK - 1)) <= past_len
        s = _dot_nt(q, kc) - slope * (float(past_len) - center)
        s = jnp.where(ok_c, s, NEG_INF)
        e = jnp.exp(s - jnp.max(s, axis=1, keepdims=True))
        p = jnp.where(ok_c, e / jnp.sum(e, axis=1, keepdims=True), 0.0)
        oc_ref[...] = _dot(p.astype(BF16), vc)
        psum = jnp.concatenate(
            [jnp.broadcast_to(jnp.sum(p[g * HEADS_PER_KV:(g + 1) * HEADS_PER_KV], axis=0, keepdims=True),
                              (HEADS_PER_KV, n_cmp)) for g in range(KV_HEADS)], axis=0)
        hi, mid, lo = _split3(psum)
        pair = pair_ref[...]
        imp = _dot(hi, pair) + _dot(mid, pair) + _dot(lo, pair)
        n_lanes = imp.shape[1]
        j_idx = lax.broadcasted_iota(jnp.int32, (1, n_lanes), 1)
        cur = past_len // SEL_BLOCK
        real = j_idx < n_sel
        ok_s = j_idx <= cur
        forced = (j_idx == 0) | (j_idx >= cur - 1)
        score = jnp.where(ok_s & forced, FORCE_SCORE, jnp.where(ok_s, imp, NEG_INF))
        score = jnp.where(real, score, NOT_A_BLOCK)
        rank = jnp.zeros(score.shape, F32)
        for k in range(n_sel):
            sk = score[:, k:k + 1]
            beats = (sk > score) | ((sk == score) & (j_idx > k))
            rank = rank + jnp.where(beats, 1.0, 0.0)
        sel_ref[...] = jnp.where((rank < float(min(N_SELECT, n_sel))) & real, 1.0, 0.0)


def _nsa_dec_cmp(page_table, qm, cache, p, layer, past_len):
    bs = qm.shape[0]
    n_pages = past_len // PAGE_SIZE
    pps = PAGES_PER_STEP
    n_cmp = past_len // CMP_BLOCK
    n_sel = past_len // SEL_BLOCK + 1
    n_lanes = -(-n_sel // LANES) * LANES
    pair = (jnp.arange(n_cmp)[:, None] // (SEL_BLOCK // CMP_BLOCK) == jnp.arange(n_lanes)[None, :]).astype(BF16)
    small = [p["slope_rows"], p["posk_rows"], p["posv_rows"], p["projk_bd"], p["projv_bd"], pair]
    full = lambda a: pl.BlockSpec(a.shape, lambda b, pc, pt: (0,) * a.ndim)
    page_spec = lambda k: pl.BlockSpec((None, None, 2, KV_HEADS, HEAD_DIM, PAGE_SIZE),
                                       lambda b, pc, pt, k=k: (layer, pt[b, pc * pps + k], 0, 0, 0, 0))
    grid_spec = pltpu.PrefetchScalarGridSpec(
        num_scalar_prefetch=1,
        grid=(bs, n_pages // pps),
        in_specs=[pl.BlockSpec((None, NSA_HEADS, LANES), lambda b, pc, pt: (b, 0, 0))]
                 + [full(a) for a in small] + [page_spec(k) for k in range(pps)],
        out_specs=[pl.BlockSpec((None, NSA_HEADS, LANES), lambda b, pc, pt: (b, 0, 0)),
                   pl.BlockSpec((None, NSA_HEADS, n_lanes), lambda b, pc, pt: (b, 0, 0))],
        scratch_shapes=[pltpu.VMEM((n_cmp, LANES), F32), pltpu.VMEM((n_cmp, LANES), F32)],
    )
    return pl.pallas_call(
        functools.partial(_nsa_dec_cmp_kernel, past_len=past_len, pages_per_step=pps),
        grid_spec=grid_spec,
        out_shape=[jax.ShapeDtypeStruct((bs, NSA_HEADS, LANES), F32),
                   jax.ShapeDtypeStruct((bs, NSA_HEADS, n_lanes), F32)],
        compiler_params=_params(("parallel", "arbitrary")),
        name="nsa_dec_cmp",
    )(page_table, qm, *small, *([cache] * pps))


def _nsa_dec_sel_kernel(pt_ref, qm_ref, slope_ref, selr_ref, selnew_ref, knew_ref, vnew_ref, kwnew_ref, vwnew_ref,
                        win_ref, oc_ref, graw_ref, *rest, past_len, pages_per_step):
    pages = rest[:pages_per_step]
    out_ref, m_s, l_s, acc_s = rest[pages_per_step:]
    pc = pl.program_id(1)
    keys_per_step = pages_per_step * PAGE_SIZE
    q = (qm_ref[...] * SCALE).astype(BF16)
    qf = q.astype(F32)
    slope = slope_ref[:, 0:1]

    @pl.when(pc == 0)
    def _():
        m_s[...] = jnp.full(m_s.shape, NEG_INF, F32)
        l_s[...] = jnp.zeros(l_s.shape, F32)
        acc_s[...] = jnp.zeros(acc_s.shape, F32)

    lane = lax.broadcasted_iota(jnp.int32, (1, PAGE_SIZE), 1)
    selr = selr_ref[...]
    blocks_pp = PAGE_SIZE // SEL_BLOCK
    s_parts, ok_parts = [], []
    for k in range(pages_per_step):
        s = _dot(q, pages[k][0].reshape(LANES, PAGE_SIZE).astype(BF16))
        spos = pc * keys_per_step + k * PAGE_SIZE + lane
        dist = past_len - spos
        in_sel = selr[:, blocks_pp * k:blocks_pp * k + 1]
        for bb in range(1, blocks_pp):
            in_sel = jnp.where(lane // SEL_BLOCK == bb, selr[:, blocks_pp * k + bb:blocks_pp * k + bb + 1], in_sel)
        s_parts.append(s - slope * dist.astype(F32))
        ok_parts.append((in_sel > 0.5) & (dist >= 0))
    ok = jnp.concatenate(ok_parts, axis=1)
    s = jnp.where(ok, jnp.concatenate(s_parts, axis=1), NEG_INF)
    m_old = m_s[:, 0:1]
    m_new = jnp.maximum(m_old, jnp.max(s, axis=1, keepdims=True))
    alpha = jnp.exp(m_old - m_new)
    p = jnp.where(ok, jnp.exp(s - m_new), 0.0)
    l_new = alpha * l_s[:, 0:1] + jnp.sum(p, axis=1, keepdims=True)
    pb = p.astype(BF16)
    acc = alpha * acc_s[...]
    for k in range(pages_per_step):
        acc = acc + _dot_nt(pb[:, k * PAGE_SIZE:(k + 1) * PAGE_SIZE],
                            pages[k][1].reshape(LANES, PAGE_SIZE).astype(BF16))
    acc_s[...] = acc
    m_s[...] = jnp.broadcast_to(m_new, m_s.shape)
    l_s[...] = jnp.broadcast_to(l_new, l_s.shape)

    @pl.when(pc == pl.num_programs(1) - 1)
    def _():
        kn = knew_ref[...].astype(BF16).astype(F32)
        vn = vnew_ref[...].astype(BF16).astype(F32)
        s_n = jnp.sum(qf * kn, axis=1, keepdims=True)
        ok_n = selnew_ref[:, 0:1] > 0.5
        s_n = jnp.where(ok_n, s_n, NEG_INF)
        m_o = m_s[:, 0:1]
        m_f = jnp.maximum(m_o, s_n)
        a_f = jnp.exp(m_o - m_f)
        p_n = jnp.where(ok_n, jnp.exp(s_n - m_f), 0.0)
        l_f = a_f * l_s[:, 0:1] + p_n
        o_s = (a_f * acc_s[...] + p_n.astype(BF16).astype(F32) * vn) / l_f

        w_buf = win_ref.shape[-1]
        idx = lax.broadcasted_iota(jnp.int32, (1, w_buf), 1)
        dist_w = w_buf - idx
        ok_w = (dist_w <= WINDOW) & ((past_len - dist_w) >= 0)
        s_w = _dot(q, win_ref[0].reshape(LANES, w_buf).astype(BF16)) - slope * dist_w.astype(F32)
        s_w = jnp.where(ok_w, s_w, NEG_INF)
        kwn = kwnew_ref[...].astype(BF16).astype(F32)
        vwn = vwnew_ref[...].astype(BF16).astype(F32)
        s_wn = jnp.sum(qf * kwn, axis=1, keepdims=True)
        m_w = jnp.maximum(jnp.max(s_w, axis=1, keepdims=True), s_wn)
        e_w = jnp.exp(s_w - m_w)
        e_wn = jnp.exp(s_wn - m_w)
        den = jnp.sum(e_w, axis=1, keepdims=True) + e_wn
        p_w = (e_w / den).astype(BF16)
        p_wn = (e_wn / den).astype(BF16).astype(F32)
        o_w = _dot_nt(p_w, win_ref[1].reshape(LANES, w_buf).astype(BF16)) + p_wn * vwn

        gate = jax.nn.sigmoid(graw_ref[...])
        out_ref[...] = gate[:, 0:1] * oc_ref[...] + gate[:, 1:2] * o_s + gate[:, 2:3] * o_w


def _nsa_dec_sel(page_table, qm, selr, selnew, knew_s, vnew_s, knew_w, vnew_w, win, oc, graw, cache, p, layer,
                 past_len):
    bs = qm.shape[0]
    n_pages = past_len // PAGE_SIZE
    pps = PAGES_PER_STEP
    w_buf = win.shape[-1]
    per_b =lambda a: pl.BlockSpec((None,) + a.shape[1:], lambda b, pc, pt: (b,) + (0,) * (a.ndim - 1))
    page_spec = lambda k: pl.BlockSpec((None, None, 2, KV_HEADS, HEAD_DIM, PAGE_SIZE),
                                       lambda b, pc, pt, k=k: (layer, pt[b, pc * pps + k], 0, 0, 0, 0))
    slope = p["slope_rows"]
    grid_spec = pltpu.PrefetchScalarGridSpec(
        num_scalar_prefetch=1,
        grid=(bs, n_pages // pps),
        in_specs=[
            per_b(qm),
            pl.BlockSpec(slope.shape, lambda b, pc, pt: (0, 0)),
            pl.BlockSpec((None, None, NSA_HEADS, LANES), lambda b, pc, pt: (b, pc, 0, 0)),
            per_b(selnew), per_b(knew_s), per_b(vnew_s), per_b(knew_w), per_b(vnew_w),
            pl.BlockSpec((None, None, 2, KV_HEADS, HEAD_DIM, w_buf), lambda b, pc, pt: (layer, b, 0, 0, 0, 0)),
            per_b(oc), per_b(graw),
        ] + [page_spec(k) for k in range(pps)],
        out_specs=pl.BlockSpec((None, NSA_HEADS, LANES), lambda b, pc, pt: (b, 0, 0)),
        scratch_shapes=[pltpu.VMEM((NSA_HEADS, LANES), F32)] * 3,
    )
    return pl.pallas_call(
        functools.partial(_nsa_dec_sel_kernel, past_len=past_len, pages_per_step=pps),
        grid_spec=grid_spec,
        out_shape=jax.ShapeDtypeStruct((bs, NSA_HEADS, LANES), F32),
        compiler_params=_params(("parallel", "arbitrary")),
        name="nsa_dec_sel",
    )(page_table, qm, slope, selr, selnew, knew_s, vnew_s, knew_w, vnew_w, win, oc, graw, *([cache] * pps))


def _outproj_kernel(mabc_p, md_p, mabc_s, md_s, x, wo, g1, b1, wr, br, x1_ref, ids_ref, wts_ref, *,
                    alpha, n_prompt_tiles):
    i = pl.program_id(0)

    @pl.when(i < n_prompt_tiles)
    def _():
        _outproj_tile(mabc_p, md_p, x, wo, g1, b1, wr, br, x1_ref, ids_ref, wts_ref, alpha)

    @pl.when(i >= n_prompt_tiles)
    def _():
        _outproj_tile(mabc_s, md_s, x, wo, g1, b1, wr, br, x1_ref, ids_ref, wts_ref, alpha)


def _outproj_tile(mabc, md, x, wo, g1, b1, wr, br, x1_ref, ids_ref, wts_ref, alpha):
    k_abc = mabc.shape[1]
    mix = _dot(mabc[...], wo[0:k_abc, :]) + _dot(md[...], wo[k_abc:, :])
    x1 = _layer_norm(alpha * x[...] + mix, g1[...], b1[...])
    x1_ref[...] = x1
    logits = _dot(x1.astype(BF16), wr[...]) + br[...]
    lane = lax.broadcasted_iota(jnp.int32, logits.shape, 1)
    is_g = lane < N_GROUPS
    gl = jnp.where(is_g, logits, -jnp.inf)
    gmax = jnp.max(gl, axis=1, keepdims=True)
    gsel = jnp.min(jnp.where(gl == gmax, lane, LANES), axis=1, keepdims=True)
    ggate = 1.0 / jnp.sum(jnp.where(is_g, jnp.exp(gl - gmax), 0.0), axis=1, keepdims=True)
    lo = N_GROUPS + gsel * EXPERTS_PER_GROUP
    el = jnp.where((lane >= lo) & (lane < lo + EXPERTS_PER_GROUP), logits, -jnp.inf)
    v1 = jnp.max(el, axis=1, keepdims=True)
    i1 = jnp.min(jnp.where(el == v1, lane, LANES), axis=1, keepdims=True)
    el2 = jnp.where(lane == i1, -jnp.inf, el)
    v2 = jnp.max(el2, axis=1, keepdims=True)
    i2 = jnp.min(jnp.where(el2 == v2, lane, LANES), axis=1, keepdims=True)
    e21 = jnp.exp(v2 - v1)
    w1 = ggate / (1.0 + e21)
    w2 = ggate * e21 / (1.0 + e21)
    ids_ref[...] = jnp.where(lane == 0, i1 - N_GROUPS, jnp.where(lane == 1, i2 - N_GROUPS, 0))
    wts_ref[...] = jnp.where(lane == 0, w1, jnp.where(lane == 1, w2, 0.0))


def _outproj(mabc_p, md_p, mabc_s, md_s, x, wo_bf, g1, b1, wr_bf, br, alpha):
    nt = x.shape[0]
    tm = TOK_TILE
    n_p = mabc_p.shape[0] // tm
    assert mabc_p.shape[0] % tm == 0 and mabc_s.shape[0] == nt - mabc_p.shape[0]
    row = lambda w: pl.BlockSpec((tm, w), lambda i: (i, 0))
    prow = lambda w: pl.BlockSpec((tm, w), lambda i: (jnp.minimum(i, n_p - 1), 0))
    srow = lambda w: pl.BlockSpec((tm, w), lambda i: (jnp.maximum(i - n_p, 0), 0))
    full = lambda a: pl.BlockSpec(a.shape, lambda i: (0,) * a.ndim)
    return pl.pallas_call(
        functools.partial(_outproj_kernel, alpha=alpha, n_prompt_tiles=n_p),
        grid=(nt // tm,),
        in_specs=[prow(mabc_p.shape[1]), prow(md_p.shape[1]), srow(mabc_s.shape[1]), srow(md_s.shape[1]),
                  row(D_MODEL), full(wo_bf), full(g1), full(b1), full(wr_bf), full(br)],
        out_specs=[row(D_MODEL), row(LANES), row(LANES)],
        out_shape=[jax.ShapeDtypeStruct((nt, D_MODEL), F32),
                   jax.ShapeDtypeStruct((nt, LANES), jnp.int32), jax.ShapeDtypeStruct((nt, LANES), F32)],
        compiler_params=_params(("parallel",)),
        name="outproj_ln1_route",
    )(mabc_p, md_p, mabc_s, md_s, x, wo_bf, g1, b1, wr_bf, br)


def _row_gather_start(idx_of_row, src_hbm, dst, sem, n_rows):
    def body(r, carry):
        pltpu.make_async_copy(src_hbm.at[pl.ds(idx_of_row(r), 1), :], dst.at[pl.ds(r, 1), :], sem).start()
        return carry

    lax.fori_loop(0, n_rows, body, 0, unroll=8)


def _row_gather_wait(src_hbm, dst, sem):
    pltpu.make_async_copy(src_hbm.at[pl.ds(0, dst.shape[0]), :], dst, sem).wait()


def _moe_kernel(te_ref, tv_ref, st_ref, x_hbm, w_ref, wg_ref, wu_ref, wd_ref, o_ref, xbuf, sem, wgb, wub, wdb):
    i = pl.program_id(0)
    n = pl.num_programs(0)
    tm = xbuf.shape[1]
    prev = te_ref[jnp.maximum(i - 1, 0)]

    def gather(tile, slot):
        _row_gather_start(lambda r: st_ref[tile * tm + r], x_hbm, xbuf.at[slot], sem.at[slot], tm)

    @pl.when(i == 0)
    def _():
        gather(0, 0)

    nxt = jnp.minimum(i + 1, n - 1)

    @pl.when((i + 1 < n) & (tv_ref[nxt] > 0))
    def _():
        gather(i + 1, (i + 1) % 2)

    @pl.when((i == 0) | (te_ref[i] != prev))
    def _():
        wgb[...] = wg_ref[...].astype(BF16)
        wub[...] = wu_ref[...].astype(BF16)
        wdb[...] = wd_ref[...].astype(BF16)

    @pl.when(tv_ref[i] > 0)
    def _():
        slot = i % 2
        _row_gather_wait(x_hbm, xbuf.at[slot], sem.at[slot])
        x = xbuf[slot].astype(BF16)
        hg = _dot(x, wgb[...])
        hu = _dot(x, wub[...])
        hidden = (hg * jax.nn.sigmoid(hg)) * hu * w_ref[...]
        o_ref[...] = _dot(hidden.astype(BF16), wdb[...])

    @pl.when(tv_ref[i] == 0)
    def _():
        o_ref[...] = jnp.zeros(o_ref.shape, F32)


def _moe(tile_expert, tile_valid, slot_token, x1, slot_w, w_gate_e, w_up_e, w_down_e, layer):
    n_slots = slot_token.shape[0]
    tm = MOE_TM
    wspec = lambda r, c: pl.BlockSpec((None, None, r, c), lambda i, te, tv, st: (layer, te[i], 0, 0))
    grid_spec = pltpu.PrefetchScalarGridSpec(
        num_scalar_prefetch=3,
        grid=(n_slots // tm,),
        in_specs=[
            pl.BlockSpec(memory_space=pl.ANY),
            pl.BlockSpec((tm, 1), lambda i, te, tv, st: (i, 0)),
            wspec(D_MODEL, D_EXPERT), wspec(D_MODEL, D_EXPERT), wspec(D_EXPERT, D_MODEL),
        ],
        out_specs=pl.BlockSpec((tm, D_MODEL), lambda i, te, tv, st: (i, 0)),
        scratch_shapes=[pltpu.VMEM((2, tm, D_MODEL), F32), pltpu.SemaphoreType.DMA((2,)),
                        pltpu.VMEM((D_MODEL, D_EXPERT), BF16), pltpu.VMEM((D_MODEL, D_EXPERT), BF16),
                        pltpu.VMEM((D_EXPERT, D_MODEL), BF16)],
    )
    return pl.pallas_call(
        _moe_kernel,
        grid_spec=grid_spec,
        out_shape=jax.ShapeDtypeStruct((n_slots, D_MODEL), F32),
        compiler_params=_params(("arbitrary",)),
        name="moe_experts",
    )(tile_expert, tile_valid, slot_token, x1, slot_w, w_gate_e, w_up_e, w_down_e)


def _ln2_kernel(ps_ref, x1, y_hbm, g2, b2, o_ref, ybuf, sem, *, alpha):
    i = pl.program_id(0)
    n = pl.num_programs(0)
    tm = ybuf.shape[2]

    def gather(tile, slot):
        for k in range(2):
            _row_gather_start(lambda r, k=k: ps_ref[(tile * tm + r) * 2 + k], y_hbm, ybuf.at[slot, k],
                              sem.at[slot], tm)

    @pl.when(i == 0)
    def _():
        gather(0, 0)

    @pl.when(i + 1 < n)
    def _():
        gather(i + 1, (i + 1) % 2)

    slot = i % 2
    for k in range(2):
        _row_gather_wait(y_hbm, ybuf.at[slot, k], sem.at[slot])
    o_ref[...] = _layer_norm(alpha * x1[...] + (ybuf[slot, 0] + ybuf[slot, 1]), g2[...], b2[...])


def _ln2(pair_slots, x1, y_slots, g2, b2, alpha):
    nt = x1.shape[0]
    tm = TOK_TILE
    row = pl.BlockSpec((tm, D_MODEL), lambda i, ps: (i, 0))
    full = lambda a: pl.BlockSpec(a.shape, lambda i, ps: (0,) * a.ndim)
    grid_spec = pltpu.PrefetchScalarGridSpec(
        num_scalar_prefetch=1,
        grid=(nt // tm,),
        in_specs=[row, pl.BlockSpec(memory_space=pl.ANY), full(g2), full(b2)],
        out_specs=row,
        scratch_shapes=[pltpu.VMEM((2, 2, tm, D_MODEL), F32), pltpu.SemaphoreType.DMA((2,))],
    )
    return pl.pallas_call(
        functools.partial(_ln2_kernel, alpha=alpha),
        grid_spec=grid_spec,
        out_shape=jax.ShapeDtypeStruct((nt, D_MODEL), F32),
        compiler_params=_params(("arbitrary",)),
        name="combine_ln2",
    )(pair_slots, x1, y_slots, g2, b2)


def _route_slots(ids, wts, n_tok, n_slots):
    tm = MOE_TM
    e_flat = ids[:n_tok, :2].reshape(-1)
    w_flat = wts[:n_tok, :2].reshape(-1)
    n_pairs = e_flat.shape[0]
    take = lambda table, idx: table.at[idx].get(mode="promise_in_bounds")
    order = jnp.argsort(e_flat, stable=True).astype(jnp.int32)
    rank = jnp.argsort(order).astype(jnp.int32)
    counts = jnp.sum(e_flat[:, None] == jnp.arange(N_EXPERTS, dtype=jnp.int32)[None, :], axis=0).astype(jnp.int32)
    padded = (counts + tm - 1) // tm * tm
    ends = jnp.cumsum(padded)
    offs = ends - padded
    starts = jnp.cumsum(counts) - counts
    slot_of_pair = take(offs - starts, e_flat) + rank
    tile_start = jnp.arange(n_slots // tm, dtype=jnp.int32) * tm
    total = ends[-1]
    tile_valid = (tile_start < total).astype(jnp.int32)
    last_start = jnp.maximum(total - tm, 0)
    tile_expert = jnp.sum(jnp.minimum(tile_start, last_start)[:, None] >= ends[None, :], axis=1).astype(jnp.int32)
    tile_expert = jnp.minimum(tile_expert, N_EXPERTS - 1)
    in_run = tile_start[:, None] + jnp.arange(tm, dtype=jnp.int32)[None, :] - take(offs, tile_expert)[:, None]
    filled = (in_run < take(counts, tile_expert)[:, None]) & (tile_valid[:, None] > 0)
    src = jnp.clip(take(starts, tile_expert)[:, None] + in_run, 0, n_pairs - 1).reshape(-1)
    pair = take(order, src)
    filled = filled.reshape(-1)
    slot_token = jnp.where(filled, pair // 2, 0)
    slot_w = jnp.where(filled, take(w_flat, pair), 0.0)
    return slot_token, slot_w, slot_of_pair, tile_expert, tile_valid


def _transpose_kernel(x_ref, o_ref):
    o_ref[...] = x_ref[...].T


def _kv_rows_minor(h, batch, seq):
    tt = KEY_CHUNK
    nt = seq // tt
    return pl.pallas_call(
        _transpose_kernel,
        grid=(batch, 3, nt),
        in_specs=[pl.BlockSpec((tt, 2 * LANES), lambda b, br, t: (b * nt + t, COL_KV // (2 * LANES) + br))],
        out_specs=pl.BlockSpec((None, None, 2 * LANES, tt), lambda b, br, t: (b, br, 0, t)),
        out_shape=jax.ShapeDtypeStruct((batch, 3, 2 * LANES, seq), F32),
        compiler_params=_params(("parallel", "parallel", "parallel")),
        name="kv_rows_minor",
    )(h)


def _layer(x, layer, n_prompt_rows, batch, seq, bs, past_len, page_table, caches, states, weights):
    (cache_cmp, cache_sel, state_win, state_conv_a, state_conv_c) = (caches[0], caches[1], states[0], states[1],
                                                                      states[2])
    w = weights
    nt = x.shape[0]
    n_tok = n_prompt_rows + bs
    depth = w["w_in"].shape[0]
    alpha = (2.0 * depth) ** 0.25
    rep_g = lambda a: jnp.repeat(a, HEAD_DIM, axis=1)
    bd = lambda a: jnp.zeros((LANES, LANES), F32).at[:HEAD_DIM, :HEAD_DIM].set(a[0]).at[HEAD_DIM:, HEAD_DIM:].set(a[1])
    row = lambda a: a[layer][None, :]
    p = {
        "conv_a_w": w["conv_a_w"][layer], "conv_a_b": row(w["conv_a_b"]),
        "ln_a_g": row(w["ln_a_g"]), "ln_a_b": row(w["ln_a_b"]), "ln_v_g": row(w["ln_v_g"]), "ln_v_b": row(w["ln_v_b"]),
        "spatial_w": w["spatial_w"][layer],
        "spatial_b_rows": jnp.repeat(w["spatial_b"][layer].T, CHUNK, axis=1),
        "spatial_w00": jnp.repeat(w["spatial_w"][layer][:, 0, 0], CHUNK)[None, :],
        "spatial_b0": jnp.repeat(w["spatial_b"][layer][:, 0], CHUNK)[None, :],
        "conv_c_w": w["conv_c_w"][layer],
        "posk_rows": rep_g(w["cmp_pos_k"][layer]), "posv_rows": rep_g(w["cmp_pos_v"][layer]),
        "projk_bd": bd(w["cmp_proj_k"][layer]), "projv_bd": bd(w["cmp_proj_v"][layer]),
        "slope_rows": jnp.broadcast_to(jnp.asarray(SLOPES, F32)[:, None], (NSA_HEADS, LANES)),
    }

    h = _inproj(x, w["w_in"], w["b_in"], layer)

    mabc_p, sta_p, stc_p = _mix_abc_prompt(h, p, batch, seq)
    md_p = _nsa_prompt(h, p, batch, seq)

    hs = h[n_prompt_rows:n_tok]
    sta_t = jnp.transpose(state_conv_a[layer], (1, 0, 2))
    stc_t = jnp.transpose(state_conv_c[layer], (1, 0, 2))
    mabc_s, a_in_s, ccx_s, v_s = _mix_abc_sample(hs[:, :7 * D_GRP], sta_t, stc_t, p)
    q_s = hs[:, COL_Q:COL_KV].reshape(bs, NSA_HEADS, HEAD_DIM)
    grp = jnp.arange(NSA_HEADS) // HEADS_PER_KV
    qm = jnp.where((jnp.arange(LANES)[None, :] // HEAD_DIM == grp[:, None])[None],
                   jnp.tile(q_s, (1, 1, KV_HEADS)), 0.0)
    kv_s = hs[:, COL_KV:COL_G].reshape(bs, 3, 2, LANES)
    graw = jnp.pad(jnp.transpose(hs[:, COL_G:N_IN].reshape(bs, 3, NSA_HEADS), (0, 2, 1)),
                   ((0, 0), (0, 0), (0, LANES - 3)))
    rows_minor = lambda a: jnp.transpose(a, (0, 1, 3, 4, 5, 2))
    cache_cmp_r, cache_sel_r, win_r = rows_minor(cache_cmp), rows_minor(cache_sel), rows_minor(state_win)
    oc, sel = _nsa_dec_cmp(page_table, qm, cache_cmp_r, p, layer, past_len)
    n_steps = past_len // PAGE_SIZE // PAGES_PER_STEP
    n_past_blocks = past_len // SEL_BLOCK
    blocks_per_step = n_past_blocks // n_steps
    selr = jnp.transpose(sel[:, :, :n_past_blocks].reshape(bs, NSA_HEADS, n_steps, blocks_per_step), (0, 2, 1, 3))
    selr = jnp.pad(selr, ((0, 0), (0, 0), (0, 0), (0, LANES - blocks_per_step)))
    selnew = jnp.broadcast_to(sel[:, :, n_past_blocks:n_past_blocks + 1], (bs, NSA_HEADS, LANES))
    d_s = _nsa_dec_sel(page_table, qm, selr, selnew, kv_s[:, 1, 0][:, None], kv_s[:, 1, 1][:, None],
                       kv_s[:, 2, 0][:, None], kv_s[:, 2, 1][:, None], win_r, oc, graw, cache_sel_r, p, layer,
                       past_len)
    d_s = d_s.reshape(bs, NSA_HEADS, KV_HEADS, HEAD_DIM)
    md_s = jnp.concatenate([d_s[:, :HEADS_PER_KV, 0], d_s[:, HEADS_PER_KV:, 1]], axis=1).reshape(bs, D_GRP)

    pad_rows = nt - n_tok
    s_rows = nt - n_prompt_rows
    mabc_s = jnp.pad(mabc_s, ((0, s_rows - bs), (0, 0)))
    md_s = jnp.pad(md_s.astype(BF16), ((0, s_rows - bs), (0, 0)))
    wr = jnp.concatenate([w["w_group"][layer], w["w_router"][layer]], axis=1)
    wr = jnp.pad(wr, ((0, 0), (0, LANES - wr.shape[1]))).astype(BF16)
    br = jnp.pad(jnp.concatenate([w["b_group"][layer], w["b_router"][layer]]), (0, LANES - N_GROUPS - N_EXPERTS))[None]
    x1, ids, wts = _outproj(mabc_p, md_p, mabc_s, md_s, x, w["w_out"][layer].astype(BF16), row(w["ln1_g"]),
                            row(w["ln1_b"]), wr, br, alpha)

    n_slots = (-(-(2 * n_tok) // MOE_TM) + N_EXPERTS) * MOE_TM
    slot_token, slot_w, slot_of_pair, tile_expert, tile_valid = _route_slots(ids, wts, n_tok, n_slots)
    y_slots = _moe(tile_expert, tile_valid, slot_token, x1, slot_w[:, None], w["w_gate_e"], w["w_up_e"],
                   w["w_down_e"], layer)
    pair_slots = jnp.pad(slot_of_pair, (0, 2 * pad_rows))
    x2 = _ln2(pair_slots, x1, y_slots, row(w["ln2_g"]), row(w["ln2_b"]), alpha)

    kv_t = _kv_rows_minor(h, batch, seq).reshape(batch, 3, 2, KV_HEADS, HEAD_DIM, seq)
    kv_p = jnp.transpose(kv_t, (0, 5, 1, 2, 3, 4))
    w_buf = state_win.shape[2]
    kv_s6 = hs[:, COL_KV:COL_G].reshape(bs, 1, 3, 2, KV_HEADS, HEAD_DIM)
    st_p = (kv_p[:, :, 0], kv_p[:, :, 1], kv_p[:, seq - w_buf:, 2],
            sta_p[:, 32 - (CONV_A_WIDTH - 1):], stc_p[:, 8 - (CONV_C_WIDTH - 1):])
    st_s = (kv_s6[:, :, 0], kv_s6[:, :, 1],
            jnp.concatenate([state_win[layer][:, 1:], kv_s6[:, :, 2]], axis=1),
            jnp.concatenate([state_conv_a[layer][:, 1:], a_in_s[:, None]], axis=1),
            jnp.concatenate([state_conv_c[layer][:, 1:], ccx_s[:, None]], axis=1),
            v_s[:, None])
    return x2, st_p, st_s


def kernel(x_prompt, x_sample, cache_cmp_kv, cache_sel_kv, state_win_kv, state_conv_a, state_conv_c, page_table, w_in, b_in, conv_a_w, conv_a_b, ln_a_g, ln_a_b, ln_v_g, ln_v_b, spatial_w, spatial_b, conv_c_w, cmp_pos_k, cmp_pos_v, cmp_proj_k, cmp_proj_v, w_out, ln1_g, ln1_b, ln2_g, ln2_b, w_group, b_group, w_router, b_router, w_gate_e, w_up_e, w_down_e):
    batch, seq, d_model = x_prompt.shape
    bs, dec_seq, _ = x_sample.shape
    depth = w_in.shape[0]
    past_len = page_table.shape[1] * PAGE_SIZE
    assert d_model == D_MODEL and dec_seq == 1 and w_in.shape[2] == N_IN
    assert seq % KEY_CHUNK == 0 and seq % MIX_TT == 0
    assert past_len % (PAGE_SIZE * PAGES_PER_STEP) == 0 and state_win_kv.shape[2] == WINDOW and past_len >= WINDOW
    n_prompt_rows = batch * seq
    n_tok = n_prompt_rows + bs
    tile = math.lcm(TOK_TILE, INPROJ_TM, MOE_TM)
    nt = -(-n_tok // tile) * tile
    x = jnp.concatenate([x_prompt.reshape(n_prompt_rows, d_model), x_sample.reshape(bs, d_model),
                         jnp.zeros((nt - n_tok, d_model), F32)], axis=0)
    weights = dict(w_in=w_in, b_in=b_in, conv_a_w=conv_a_w, conv_a_b=conv_a_b, ln_a_g=ln_a_g, ln_a_b=ln_a_b,
                   ln_v_g=ln_v_g, ln_v_b=ln_v_b, spatial_w=spatial_w, spatial_b=spatial_b, conv_c_w=conv_c_w,
                   cmp_pos_k=cmp_pos_k, cmp_pos_v=cmp_pos_v, cmp_proj_k=cmp_proj_k, cmp_proj_v=cmp_proj_v,
                   w_out=w_out, ln1_g=ln1_g, ln1_b=ln1_b, ln2_g=ln2_g, ln2_b=ln2_b, w_group=w_group,
                   b_group=b_group, w_router=w_router, b_router=b_router, w_gate_e=w_gate_e, w_up_e=w_up_e,
                   w_down_e=w_down_e)
    st_p, st_s = [], []
    for layer in range(depth):
        x, sp, ss = _layer(x, layer, n_prompt_rows, batch, seq, bs, past_len, page_table,
                           (cache_cmp_kv, cache_sel_kv), (state_win_kv, state_conv_a, state_conv_c), weights)
        st_p.append(sp)
        st_s.append(ss)
    y_prompt = x[:n_prompt_rows].reshape(batch, seq, d_model)
    y_sample = x[n_prompt_rows:n_tok].reshape(bs, 1, d_model)
    return (y_prompt, y_sample,
            jnp.stack([s[0] for s in st_p]), jnp.stack([s[1] for s in st_p]), jnp.stack([s[2] for s in st_p]),
            jnp.stack([s[3] for s in st_p]), jnp.stack([s[4] for s in st_p]),
            jnp.stack([s[0] for s in st_s]), jnp.stack([s[1] for s in st_s]), jnp.stack([s[2] for s in st_s]),
            jnp.stack([s[3] for s in st_s]), jnp.stack([s[4] for s in st_s]), jnp.stack([s[5] for s in st_s]))
```

```python
import functools

import jax
import jax.numpy as jnp
from jax import lax
from jax.experimental import pallas as pl
from jax.experimental.pallas import tpu as pltpu

F32 = jnp.float32
BF16 = jnp.bfloat16

D_MODEL = 2048
D_GRP = 512
NSA_HEADS = 8
HEAD_DIM = 64
KV_HEADS = 2
HEADS_PER_KV = 4
CONV_A_WIDTH = 31
CONV_C_WIDTH = 3
CHUNK = 128
GMLP_HEADS = 4
CMP_BLOCK = 32
SEL_BLOCK = 64
N_SELECT = 16
WINDOW = 512
PAGE_SIZE = 128
N_GROUPS = 4
EXPERTS_PER_GROUP = 8
N_EXPERTS = 32
D_EXPERT = 512
LN_EPS = 1e-5
NEG_INF = -1e30
FORCE_SCORE = 1e9
NOT_A_BLOCK = -3e38
MASKED_SCORE = -1e35
N_IN = 4888
COL_Q = 3584
COL_KV = 4096
COL_G = 4864
SCALE = HEAD_DIM ** -0.5
SLOPES = tuple(2.0 ** (-(h + 1)) for h in range(NSA_HEADS))

LANES = 128
SUBLANES = 8
VMEM_LIMIT = 56 * 1024 * 1024

TOK_TILE = 256
UNIFIED_ROW_MULTIPLE = 512
INPROJ_ROW_TILES = 8
INPROJ_TN = 512
MIX_TT = 256
Q_TILE = 128
KEY_CHUNK = 512
PAGES_PER_STEP = 16
MOE_TM = 256


def _dot(a, b):
    return jnp.dot(a, b, preferred_element_type=F32)


def _dot_nt(a, b):
    return lax.dot_general(a, b, (((1,), (1,)), ((), ())), preferred_element_type=F32)


def _layer_norm(x, g, b):
    mu = jnp.mean(x, axis=-1, keepdims=True)
    xc = x - mu
    var = jnp.mean(xc * xc, axis=-1, keepdims=True)
    return xc * lax.rsqrt(var + LN_EPS) * g + b


def _split3(x):
    hi = x.astype(BF16)
    r = x - hi.astype(F32)
    mid = r.astype(BF16)
    lo = (r - mid.astype(F32)).astype(BF16)
    return hi, mid, lo


def _params(sem):
    return pltpu.CompilerParams(dimension_semantics=sem, vmem_limit_bytes=VMEM_LIMIT)


def _inproj_kernel(x_ref, w_ref, b_ref, o_ref, xb_ref):
    @pl.when(pl.program_id(1) == 0)
    def _():
        xb_ref[...] = x_ref[...].astype(BF16)

    o_ref[...] = _dot(xb_ref[...], w_ref[...]) + b_ref[...]


def _inproj(x, w_in, b_in, layer):
    nt, d = x.shape
    n_in = w_in.shape[2]
    tn = INPROJ_TN
    tm = nt // INPROJ_ROW_TILES
    assert nt % (INPROJ_ROW_TILES * SUBLANES) == 0
    return pl.pallas_call(
        _inproj_kernel,
        grid=(nt // tm, pl.cdiv(n_in, tn)),
        in_specs=[
            pl.BlockSpec((tm, d), lambda i, j: (i, 0)),
            pl.BlockSpec((None, d, tn), lambda i, j: (layer, 0, j)),
            pl.BlockSpec((1, tn), lambda i, j: (0, j)),
        ],
        out_specs=pl.BlockSpec((tm, tn), lambda i, j: (i, j)),
        out_shape=jax.ShapeDtypeStruct((nt, n_in), F32),
        scratch_shapes=[pltpu.VMEM((tm, d), BF16)],
        compiler_params=_params(("parallel", "arbitrary")),
        name="inproj",
    )(x, w_in, b_in[layer][None, :])


def _mix_abc_kernel(aval, agate, bu, bv, cb, cc, cx, caw, cab, lag, lab, lvg, lvb, sw, sb, ccw,
                    out_ref, sta_ref, stc_ref, abuf, cbuf, *, tt):
    t = pl.program_id(1)
    halo_a = 32
    halo_c = 8

    @pl.when(t == 0)
    def _():
        abuf[0:halo_a, :] = jnp.zeros((halo_a, D_GRP), F32)
        cbuf[0:halo_c, :] = jnp.zeros((halo_c, D_GRP), F32)

    abuf[halo_a:halo_a + tt, :] = aval[...] * jax.nn.sigmoid(agate[...])
    rc = 64
    off_a = halo_a - (CONV_A_WIDTH - 1)
    for r in range(tt // rc):
        acc = jnp.zeros((rc, D_GRP), F32)
        for k in range(CONV_A_WIDTH):
            acc = acc + caw[k:k + 1, :] * abuf[r * rc + off_a + k:r * rc + off_a + k + rc, :]
        y = _layer_norm(acc + cab[...], lag[...], lab[...])
        out_ref[r * rc:(r + 1) * rc, 0:D_GRP] = (y * jax.nn.sigmoid(y)).astype(BF16)
    sta_ref[...] = abuf[tt:tt + halo_a, :]
    abuf[0:halo_a, :] = abuf[tt:tt + halo_a, :]

    row = lax.broadcasted_iota(jnp.int32, (CHUNK, CHUNK), 0)
    col = lax.broadcasted_iota(jnp.int32, (CHUNK, CHUNK), 1)
    for c in range(tt // CHUNK):
        rows = slice(c * CHUNK, (c + 1) * CHUNK)
        v = _layer_norm(jax.nn.gelu(bv[rows, :]), lvg[...], lvb[...]).astype(BF16)
        gu = jax.nn.gelu(bu[rows, :])
        for h in range(GMLP_HEADS):
            lanes = slice(h * CHUNK, (h + 1) * CHUNK)
            wh = jnp.where(col <= row, sw[h], 0.0).astype(BF16)
            s = _dot(wh, v[:, lanes]) + sb[:, lanes]
            out_ref[rows, D_GRP + h * CHUNK:D_GRP + (h + 1) * CHUNK] = (gu[:, lanes] * s).astype(BF16)

    cbuf[halo_c:halo_c + tt, :] = cc[...] * cx[...]
    off_c = halo_c - (CONV_C_WIDTH - 1)
    conv = jnp.zeros((tt, D_GRP), F32)
    for k in range(CONV_C_WIDTH):
        conv = conv + ccw[k:k + 1, :] * cbuf[off_c + k:off_c + k + tt, :]
    out_ref[:, 2 * D_GRP:3 * D_GRP] = (cb[...] * conv).astype(BF16)
    stc_ref[...] = cbuf[tt:tt + halo_c, :]
    cbuf[0:halo_c, :] = cbuf[tt:tt + halo_c, :]


def _mix_abc_prompt(h, p, batch, seq):
    tt = MIX_TT
    nt = seq // tt
    col_spec = lambda c: pl.BlockSpec((tt, D_GRP), lambda b, t, c=c: (b * nt + t, c))
    full = lambda a: pl.BlockSpec(a.shape, lambda b, t: (0,) * a.ndim)
    small = [p["conv_a_w"], p["conv_a_b"], p["ln_a_g"], p["ln_a_b"], p["ln_v_g"], p["ln_v_b"],
             p["spatial_w"], p["spatial_b_rows"], p["conv_c_w"]]
    return pl.pallas_call(
        functools.partial(_mix_abc_kernel, tt=tt),
        grid=(batch, nt),
        in_specs=[col_spec(c) for c in range(7)] + [full(a) for a in small],
        out_specs=[
            pl.BlockSpec((tt, 3 * D_GRP), lambda b, t: (b * nt + t, 0)),
            pl.BlockSpec((None, 32, D_GRP), lambda b, t: (b, 0, 0)),
            pl.BlockSpec((None, 8, D_GRP), lambda b, t: (b, 0, 0)),
        ],
        out_shape=[
            jax.ShapeDtypeStruct((batch * seq, 3 * D_GRP), BF16),
            jax.ShapeDtypeStruct((batch, 32, D_GRP), F32),
            jax.ShapeDtypeStruct((batch, 8, D_GRP), F32),
        ],
        scratch_shapes=[pltpu.VMEM((32 + tt, D_GRP), F32), pltpu.VMEM((8 + tt, D_GRP), F32)],
        compiler_params=_params(("parallel", "arbitrary")),
        name="mix_abc_prompt",
    )(*([h] * 7), *small)


def _place_head(q_ref, hh, g, lane_half):
    tile = q_ref[:, (hh // 2) * LANES:(hh // 2 + 1) * LANES]
    if hh % 2 != g:
        tile = pltpu.roll(tile, HEAD_DIM, axis=1)
    return jnp.where(lane_half == g, tile * SCALE, 0.0).astype(BF16)


def _nsa_prompt_kernel(q_ref, g_ref, kvc_ref, kvs_ref, kvw_ref, posk_ref, posv_ref, pk_ref, pv_ref, pair_ref,
                       out_ref,
                       kcmp, vcmp, ks, vs, kw, vw, m_s, acc_s, sc_s, sa_s, sb_s, *, seq):
    i = pl.program_id(1)
    tq_n = Q_TILE
    n_cmp = seq // CMP_BLOCK
    n_sel = seq // SEL_BLOCK

    @pl.when(i == 0)
    def _():
        step = 256
        lane = lax.broadcasted_iota(jnp.int32, (step, LANES), 1)
        for r in range(seq // step):
            rows = slice(r * step, (r + 1) * step)
            blk = step // CMP_BLOCK
            kb = (kvc_ref[rows, 0:LANES].reshape(blk, CMP_BLOCK, LANES) * posk_ref[...][None]).sum(axis=1)
            vb = (kvc_ref[rows, LANES:2 * LANES].reshape(blk, CMP_BLOCK, LANES) * posv_ref[...][None]).sum(axis=1)
            kcmp[r * blk:(r + 1) * blk, :] = _dot(kb.astype(BF16), pk_ref[...].astype(BF16)).astype(BF16)
            vcmp[r * blk:(r + 1) * blk, :] = _dot(vb.astype(BF16), pv_ref[...].astype(BF16)).astype(BF16)
            pos = r * step + lax.broadcasted_iota(jnp.int32, (step, 1), 0)
            blk_idx = pos // SEL_BLOCK
            feat = jnp.where(lane == HEAD_DIM, blk_idx.astype(F32),
                             jnp.where(lane == HEAD_DIM + 1, (pos % SEL_BLOCK).astype(F32), 0.0))
            onehot = jnp.where(lane == blk_idx, 1.0, 0.0)
            k_sel = kvs_ref[rows, 0:LANES]
            v_sel = kvs_ref[rows, LANES:2 * LANES]
            for g in range(KV_HEADS):
                k_g = k_sel if g == 0 else pltpu.roll(k_sel, HEAD_DIM, axis=1)
                ks[g, rows, :] = jnp.concatenate([jnp.where(lane < HEAD_DIM, k_g, feat), onehot],
                                                 axis=1).astype(BF16)
                vs[g, rows, :] = jnp.where(lane // HEAD_DIM == g, v_sel, 1.0).astype(BF16)
            kw[WINDOW + r * step:WINDOW + (r + 1) * step, :] = kvw_ref[rows, 0:LANES].astype(BF16)
            vw[WINDOW + r * step:WINDOW + (r + 1) * step, :] = kvw_ref[rows, LANES:2 * LANES].astype(BF16)
        kw[0:WINDOW, :] = jnp.zeros((WINDOW, LANES), BF16)
        vw[0:WINDOW, :] = jnp.zeros((WINDOW, LANES), BF16)

    t0 = i * tq_n
    tq = t0 + lax.broadcasted_iota(jnp.int32, (tq_n, 1), 0)
    lane_half = lax.broadcasted_iota(jnp.int32, (tq_n, LANES), 1) // HEAD_DIM
    gate = jax.nn.sigmoid(g_ref[...])

    n_idx = lax.broadcasted_iota(jnp.int32, (1, n_cmp), 1)
    ok_c = (n_idx * CMP_BLOCK + (CMP_BLOCK - 1)) <= tq
    dist_c = tq.astype(F32) - (n_idx.astype(F32) * CMP_BLOCK + 0.5 * (CMP_BLOCK - 1))
    tq_l = t0 + lax.broadcasted_iota(jnp.int32, (1, tq_n), 1)
    cur = tq_l // SEL_BLOCK
    j_idx = lax.broadcasted_iota(jnp.int32, (n_sel, tq_n), 0)
    ok_s = j_idx <= cur
    forced = (j_idx == 0) | (j_idx >= cur - 1)
    n_win = WINDOW + tq_n
    off_w = lax.broadcasted_iota(jnp.int32, (1, n_win), 1)
    dist_w = lax.broadcasted_iota(jnp.int32, (tq_n, 1), 0) + WINDOW - off_w
    ok_w = (dist_w >= 0) & (dist_w <= WINDOW) & ((t0 - WINDOW + off_w) >= 0)
    nbase_w = jnp.where(ok_w, -dist_w.astype(F32), MASKED_SCORE)

    group_of = lambda hh: hh // HEADS_PER_KV
    rows_of = lambda hh: slice(hh * tq_n, (hh + 1) * tq_n)
    q_all = jnp.concatenate([_place_head(q_ref, hh, group_of(hh), lane_half) for hh in range(NSA_HEADS)],
                            axis=0)

    s_all = _dot_nt(q_all, kcmp[...])
    psum = [jnp.zeros((tq_n, n_cmp), F32) for _ in range(KV_HEADS)]
    p_list = []
    for hh in range(NSA_HEADS):
        s = s_all[rows_of(hh)] - SLOPES[hh] * dist_c
        s = jnp.where(ok_c, s, NEG_INF)
        e = jnp.exp(s - jnp.max(s, axis=1, keepdims=True))
        p = jnp.where(ok_c, e * (1.0 / jnp.sum(e, axis=1, keepdims=True)), 0.0)
        psum[group_of(hh)] = psum[group_of(hh)] + p
        p_list.append(p.astype(BF16))
    o_c = _dot(jnp.concatenate(p_list, axis=0), vcmp[...])

    n_top = float(min(N_SELECT, n_sel))
    pair = pair_ref[...]
    unsel = []
    for g in range(KV_HEADS):
        hi, mid, lo = _split3(psum[g])
        imp_t = _dot_nt(pair, hi) + _dot_nt(pair, mid) + _dot_nt(pair, lo)
        score = jnp.where(ok_s & forced, FORCE_SCORE, jnp.where(ok_s, imp_t, NEG_INF))
        sc_s[...] = score
        blocks = [score[b * SUBLANES:(b + 1) * SUBLANES] for b in range(n_sel // SUBLANES)]
        ranks = [jnp.zeros((SUBLANES, tq_n), F32) for _ in blocks]
        for k in range(n_sel):
            rk = sc_s[k:k + 1, :]
            for b, sb in enumerate(blocks):
                if b < k // SUBLANES:
                    beats = rk > sb
                elif b > k // SUBLANES:
                    beats = rk >= sb
                else:
                    beats = (rk > sb) | ((rk == sb) & (j_idx[b * SUBLANES:(b + 1) * SUBLANES] > k))
                ranks[b] = ranks[b] + jnp.where(beats, 1.0, 0.0)
        rank = jnp.concatenate(ranks, axis=0)
        unsel_t = jnp.where(rank < n_top, 0.0, MASKED_SCORE)
        unsel.append(jnp.concatenate([unsel_t, jnp.zeros((LANES - n_sel, tq_n), F32)], axis=0).T.astype(BF16))

    lane_q = lax.broadcasted_iota(jnp.int32, (tq_n, LANES), 1)

    def q_aug(hh):
        tile = q_ref[:, (hh // 2) * LANES:(hh // 2 + 1) * LANES]
        if hh % 2 == 1:
            tile = pltpu.roll(tile, HEAD_DIM, axis=1)
        feat = jnp.where(lane_q == HEAD_DIM, SEL_BLOCK * SLOPES[hh], jnp.where(lane_q == HEAD_DIM + 1, SLOPES[hh], 0.0))
        low = jnp.where(lane_q < HEAD_DIM, tile * SCALE, feat).astype(BF16)
        return jnp.concatenate([low, unsel[group_of(hh)]], axis=1)

    q_sel = [jnp.concatenate([q_aug(hh) for hh in range(g * HEADS_PER_KV, (g + 1) * HEADS_PER_KV)], axis=0)
             for g in range(KV_HEADS)]
    m_s[...] = jnp.full(m_s.shape, NEG_INF, F32)
    acc_s[...] = jnp.zeros(acc_s.shape, F32)

    def scores(c, buf):
        k0 = pl.multiple_of(c * KEY_CHUNK, KEY_CHUNK)
        for g in range(KV_HEADS):
            buf[g] = _dot_nt(q_sel[g], ks[g, pl.ds(k0, KEY_CHUNK), :])

    def softmax_pv(c, buf, on_diagonal):
        k0 = pl.multiple_of(c * KEY_CHUNK, KEY_CHUNK)
        if on_diagonal:
            causal = (k0 + lax.broadcasted_iota(jnp.int32, (1, KEY_CHUNK), 1)) <= tq
        masked = (lambda s: jnp.where(causal, s, MASKED_SCORE)) if on_diagonal else (lambda s: s)
        for g in range(KV_HEADS):
            alphas, p_list = [], []
            for h in range(HEADS_PER_KV):
                rows = rows_of(g * HEADS_PER_KV + h)
                local = slice(h * tq_n, (h + 1) * tq_n)
                m_old = m_s[rows, 0:1]
                m_new = jnp.maximum(m_old, jnp.max(masked(buf[g, local, :]), axis=1, keepdims=True))
                alphas.append(jnp.exp(m_old - m_new))
                p_list.append(jnp.exp(masked(buf[g, local, :]) - m_new).astype(BF16))
                m_s[rows, :] = jnp.broadcast_to(m_new, (tq_n, LANES))
            rows_g = slice(g * HEADS_PER_KV * tq_n, (g + 1) * HEADS_PER_KV * tq_n)
            acc_s[rows_g, :] = (jnp.concatenate(alphas, axis=0) * acc_s[rows_g, :]
                                + _dot(jnp.concatenate(p_list, axis=0), vs[g, pl.ds(k0, KEY_CHUNK), :]))

    n_full = t0 // KEY_CHUNK
    scores(0, sa_s)

    def chunk_pair(j, carry):
        scores(2 * j + 1, sb_s)
        softmax_pv(2 * j, sa_s, on_diagonal=False)
        scores(2 * j + 2, sa_s)
        softmax_pv(2 * j + 1, sb_s, on_diagonal=False)
        return carry

    lax.fori_loop(0, n_full // 2, chunk_pair, 0)

    @pl.when(n_full % 2 == 0)
    def _():
        softmax_pv(n_full, sa_s, on_diagonal=True)

    @pl.when(n_full % 2 == 1)
    def _():
        scores(n_full, sb_s)
        softmax_pv(n_full - 1, sa_s, on_diagonal=False)
        softmax_pv(n_full, sb_s, on_diagonal=True)
    o_s = jnp.concatenate(
        [acc_s[rows_of(hh), :] * (1.0 / acc_s[rows_of(hh), (1 - group_of(hh)) * HEAD_DIM:(1 - group_of(hh)) * HEAD_DIM + 1])
         for hh in range(NSA_HEADS)], axis=0)

    w0 = pl.multiple_of(t0, Q_TILE)
    s_all = _dot_nt(q_all, kw[pl.ds(w0, n_win), :])
    p_list = []
    for hh in range(NSA_HEADS):
        s = s_all[rows_of(hh)] + SLOPES[hh] * nbase_w
        e = jnp.exp(s - jnp.max(s, axis=1, keepdims=True))
        p_list.append((e * (1.0 / jnp.sum(e, axis=1, keepdims=True))).astype(BF16))
    o_w = _dot(jnp.concatenate(p_list, axis=0), vw[pl.ds(w0, n_win), :])

    for hh in range(NSA_HEADS):
        g = group_of(hh)
        rows = rows_of(hh)
        o = (gate[:, hh:hh + 1] * o_c[rows]
             + gate[:, NSA_HEADS + hh:NSA_HEADS + hh + 1] * o_s[rows]
             + gate[:, 2 * NSA_HEADS + hh:2 * NSA_HEADS + hh + 1] * o_w[rows])
        o = o[:, g * HEAD_DIM:(g + 1) * HEAD_DIM]
        out_ref[:, hh * HEAD_DIM:(hh + 1) * HEAD_DIM] = o.astype(BF16)


def _nsa_prompt(h, p, batch, seq):
    nq = seq // Q_TILE
    n_cmp = seq // CMP_BLOCK
    n_sel = seq // SEL_BLOCK
    full = lambda a: pl.BlockSpec(a.shape, lambda b, i: (0,) * a.ndim)
    pair = (jnp.arange(n_cmp)[None, :] // (SEL_BLOCK // CMP_BLOCK) == jnp.arange(n_sel)[:, None]).astype(BF16)
    small = [p["posk_rows"], p["posv_rows"], p["projk_bd"], p["projv_bd"], pair]
    kv_spec = lambda br: pl.BlockSpec((seq, 2 * LANES), lambda b, i, br=br: (b, COL_KV // (2 * LANES) + br))
    return pl.pallas_call(
        functools.partial(_nsa_prompt_kernel, seq=seq),
        grid=(batch, nq),
        in_specs=[
            pl.BlockSpec((Q_TILE, D_GRP), lambda b, i: (b * nq + i, COL_Q // D_GRP)),
            pl.BlockSpec((Q_TILE, LANES), lambda b, i: (b * nq + i, COL_G // LANES)),
            kv_spec(0), kv_spec(1), kv_spec(2),
        ] + [full(a) for a in small],
        out_specs=pl.BlockSpec((Q_TILE, D_GRP), lambda b, i: (b * nq + i, 0)),
        out_shape=jax.ShapeDtypeStruct((batch * seq, D_GRP), BF16),
        scratch_shapes=[
            pltpu.VMEM((n_cmp, LANES), BF16), pltpu.VMEM((n_cmp, LANES), BF16),
            pltpu.VMEM((KV_HEADS, seq, 2 * LANES), BF16), pltpu.VMEM((KV_HEADS, seq, LANES), BF16),
            pltpu.VMEM((seq + WINDOW, LANES), BF16), pltpu.VMEM((seq + WINDOW, LANES), BF16),
            pltpu.VMEM((NSA_HEADS * Q_TILE, LANES), F32),
            pltpu.VMEM((NSA_HEADS * Q_TILE, LANES), F32),
            pltpu.VMEM((n_sel, Q_TILE), F32),
            pltpu.VMEM((KV_HEADS, HEADS_PER_KV * Q_TILE, KEY_CHUNK), F32),
            pltpu.VMEM((KV_HEADS, HEADS_PER_KV * Q_TILE, KEY_CHUNK), F32),
        ],
        compiler_params=_params(("parallel", "arbitrary")),
        name="nsa_prompt",
    )(h, h, h, h, h, *small)


def _mix_abc_sample_kernel(hs, sta, stc, caw, cab, lag, lab, lvg, lvb, sw0, sb0, ccw,
                           out_ref, ain_ref, ccx_ref, v_ref):
    g = D_GRP
    a_in = hs[:, 0:g] * jax.nn.sigmoid(hs[:, g:2 * g])
    acc = caw[CONV_A_WIDTH - 1:CONV_A_WIDTH, :] * a_in
    for k in range(CONV_A_WIDTH - 1):
        acc = acc + caw[k:k + 1, :] * sta[k]
    y = _layer_norm(acc + cab[...], lag[...], lab[...])
    out_ref[:, 0:g] = (y * jax.nn.sigmoid(y)).astype(BF16)
    ain_ref[...] = a_in

    v = _layer_norm(jax.nn.gelu(hs[:, 3 * g:4 * g]), lvg[...], lvb[...])
    v_ref[...] = v
    s = sw0[...].astype(BF16).astype(F32) * v.astype(BF16).astype(F32) + sb0[...]
    out_ref[:, g:2 * g] = (jax.nn.gelu(hs[:, 2 * g:3 * g]) * s).astype(BF16)

    ccx = hs[:, 5 * g:6 * g] * hs[:, 6 * g:7 * g]
    ccx_ref[...] = ccx
    conv = ccw[CONV_C_WIDTH - 1:CONV_C_WIDTH, :] * ccx
    for k in range(CONV_C_WIDTH - 1):
        conv = conv + ccw[k:k + 1, :] * stc[k]
    out_ref[:, 2 * g:3 * g] = (hs[:, 4 * g:5 * g] * conv).astype(BF16)


def _mix_abc_sample(hs_abc, sta_t, stc_t, p):
    bs = hs_abc.shape[0]
    args = [hs_abc, sta_t, stc_t, p["conv_a_w"], p["conv_a_b"], p["ln_a_g"], p["ln_a_b"], p["ln_v_g"], p["ln_v_b"],
            p["spatial_w00"], p["spatial_b0"], p["conv_c_w"]]
    full = lambda a: pl.BlockSpec(a.shape, lambda i: (0,) * a.ndim)
    return pl.pallas_call(
        _mix_abc_sample_kernel,
        grid=(1,),
        in_specs=[full(a) for a in args],
        out_specs=[pl.BlockSpec((bs, 3 * D_GRP), lambda i: (0, 0))] + [pl.BlockSpec((bs, D_GRP), lambda i: (0, 0))] * 3,
        out_shape=[jax.ShapeDtypeStruct((bs, 3 * D_GRP), BF16)] + [jax.ShapeDtypeStruct((bs, D_GRP), F32)] * 3,
        compiler_params=_params(("arbitrary",)),
        name="mix_abc_sample",
    )(*args)


def _nsa_dec_cmp_kernel(pt_ref, qm_ref, slope_ref, posk_ref, posv_ref, pk_ref, pv_ref, pair_ref, *rest,
                        past_len, pages_per_step):
    pages = rest[:pages_per_step]
    oc_ref, sel_ref, kbar, vbar = rest[pages_per_step:]
    pc = pl.program_id(1)
    n_cmp = past_len // CMP_BLOCK
    n_sel = past_len // SEL_BLOCK + 1
    blk_pp = PAGE_SIZE // CMP_BLOCK
    rows_of = lambda page, kv: page[kv].reshape(LANES, PAGE_SIZE).T
    for kk in range(pages_per_step // 2):
        two_k = jnp.concatenate([rows_of(pages[2 * kk], 0), rows_of(pages[2 * kk + 1], 0)], axis=0)
        two_v = jnp.concatenate([rows_of(pages[2 * kk], 1), rows_of(pages[2 * kk + 1], 1)], axis=0)
        kb = (two_k.reshape(2 * blk_pp, CMP_BLOCK, LANES) * posk_ref[...][None]).sum(axis=1)
        vb = (two_v.reshape(2 * blk_pp, CMP_BLOCK, LANES) * posv_ref[...][None]).sum(axis=1)
        r0 = pl.multiple_of(pc * (pages_per_step * blk_pp) + kk * 2 * blk_pp, SUBLANES)
        kbar[pl.ds(r0, 2 * blk_pp), :] = kb
        vbar[pl.ds(r0, 2 * blk_pp), :] = vb

    @pl.when(pc == pl.num_programs(1) - 1)
    def _():
        kc = _dot(kbar[...].astype(BF16), pk_ref[...].astype(BF16)).astype(BF16)
        vc = _dot(vbar[...].astype(BF16), pv_ref[...].astype(BF16)).astype(BF16)
        q = (qm_ref[...] * SCALE).astype(BF16)
        slope = slope_ref[:, 0:1]
        n_idx = lax.broadcasted_iota(jnp.int32, (1, n_cmp), 1)
        center = n_idx.astype(F32) * CMP_BLOCK + 0.5 * (CMP_BLOCK - 1)
        ok_c = (n_idx * CMP_BLOCK + (CMP_BLOCK - 1)) <= past_len
        s = _dot_nt(q, kc) - slope * (float(past_len) - center)
        s = jnp.where(ok_c, s, NEG_INF)
        e = jnp.exp(s - jnp.max(s, axis=1, keepdims=True))
        p = jnp.where(ok_c, e / jnp.sum(e, axis=1, keepdims=True), 0.0)
        oc_ref[...] = _dot(p.astype(BF16), vc)
        psum = jnp.concatenate(
            [jnp.broadcast_to(jnp.sum(p[g * HEADS_PER_KV:(g + 1) * HEADS_PER_KV], axis=0, keepdims=True),
                              (HEADS_PER_KV, n_cmp)) for g in range(KV_HEADS)], axis=0)
        hi, mid, lo = _split3(psum)
        pair = pair_ref[...]
        imp = _dot(hi, pair) + _dot(mid, pair) + _dot(lo, pair)
        n_lanes = imp.shape[1]
        j_idx = lax.broadcasted_iota(jnp.int32, (1, n_lanes), 1)
        cur = past_len // SEL_BLOCK
        real = j_idx < n_sel
        ok_s = j_idx <= cur
        forced = (j_idx == 0) | (j_idx >= cur - 1)
        score = jnp.where(ok_s & forced, FORCE_SCORE, jnp.where(ok_s, imp, NEG_INF))
        score = jnp.where(real, score, NOT_A_BLOCK)
        rank = jnp.zeros(score.shape, F32)
        for k in range(n_sel):
            sk = score[:, k:k + 1]
            beats = (sk > score) | ((sk == score) & (j_idx > k))
            rank = rank + jnp.where(beats, 1.0, 0.0)
        sel_ref[...] = jnp.where((rank < float(min(N_SELECT, n_sel))) & real, 1.0, 0.0)


def _nsa_dec_cmp(page_table, qm, cache, p, layer, past_len):
    bs = qm.shape[0]
    n_pages = past_len // PAGE_SIZE
    pps = PAGES_PER_STEP
    n_cmp = past_len // CMP_BLOCK
    n_sel = past_len // SEL_BLOCK + 1
    n_lanes = -(-n_sel // LANES) * LANES
    pair = (jnp.arange(n_cmp)[:, None] // (SEL_BLOCK // CMP_BLOCK) == jnp.arange(n_lanes)[None, :]).astype(BF16)
    small = [p["slope_rows"], p["posk_rows"], p["posv_rows"], p["projk_bd"], p["projv_bd"], pair]
    full = lambda a: pl.BlockSpec(a.shape, lambda b, pc, pt: (0,) * a.ndim)
    page_spec = lambda k: pl.BlockSpec((None, None, 2, KV_HEADS, HEAD_DIM, PAGE_SIZE),
                                       lambda b, pc, pt, k=k: (layer, pt[b, pc * pps + k], 0, 0, 0, 0))
    grid_spec = pltpu.PrefetchScalarGridSpec(
        num_scalar_prefetch=1,
        grid=(bs, n_pages // pps),
        in_specs=[pl.BlockSpec((None, NSA_HEADS, LANES), lambda b, pc, pt: (b, 0, 0))]
                 + [full(a) for a in small] + [page_spec(k) for k in range(pps)],
        out_specs=[pl.BlockSpec((None, NSA_HEADS, LANES), lambda b, pc, pt: (b, 0, 0)),
                   pl.BlockSpec((None, NSA_HEADS, n_lanes), lambda b, pc, pt: (b, 0, 0))],
        scratch_shapes=[pltpu.VMEM((n_cmp, LANES), F32), pltpu.VMEM((n_cmp, LANES), F32)],
    )
    return pl.pallas_call(
        functools.partial(_nsa_dec_cmp_kernel, past_len=past_len, pages_per_step=pps),
        grid_spec=grid_spec,
        out_shape=[jax.ShapeDtypeStruct((bs, NSA_HEADS, LANES), F32),
                   jax.ShapeDtypeStruct((bs, NSA_HEADS, n_lanes), F32)],
        compiler_params=_params(("parallel", "arbitrary")),
        name="nsa_dec_cmp",
    )(page_table, qm, *small, *([cache] * pps))


def _nsa_dec_sel_kernel(pt_ref, qm_ref, slope_ref, selr_ref, selnew_ref, knew_ref, vnew_ref, kwnew_ref, vwnew_ref,
                        win_ref, oc_ref, graw_ref, *rest, past_len, pages_per_step):
    pages = rest[:pages_per_step]
    out_ref, m_s, l_s, acc_s = rest[pages_per_step:]
    pc = pl.program_id(1)
    keys_per_step = pages_per_step * PAGE_SIZE
    q = (qm_ref[...] * SCALE).astype(BF16)
    qf = q.astype(F32)
    slope = slope_ref[:, 0:1]

    @pl.when(pc == 0)
    def _():
        m_s[...] = jnp.full(m_s.shape, NEG_INF, F32)
        l_s[...] = jnp.zeros(l_s.shape, F32)
        acc_s[...] = jnp.zeros(acc_s.shape, F32)

    lane = lax.broadcasted_iota(jnp.int32, (1, PAGE_SIZE), 1)
    selr = selr_ref[...]
    blocks_pp = PAGE_SIZE // SEL_BLOCK
    s_parts, ok_parts = [], []
    for k in range(pages_per_step):
        s = _dot(q, pages[k][0].reshape(LANES, PAGE_SIZE).astype(BF16))
        spos = pc * keys_per_step + k * PAGE_SIZE + lane
        dist = past_len - spos
        in_sel = selr[:, blocks_pp * k:blocks_pp * k + 1]
        for bb in range(1, blocks_pp):
            in_sel = jnp.where(lane // SEL_BLOCK == bb, selr[:, blocks_pp * k + bb:blocks_pp * k + bb + 1], in_sel)
        s_parts.append(s - slope * dist.astype(F32))
        ok_parts.append((in_sel > 0.5) & (dist >= 0))
    ok = jnp.concatenate(ok_parts, axis=1)
    s = jnp.where(ok, jnp.concatenate(s_parts, axis=1), NEG_INF)
    m_old = m_s[:, 0:1]
    m_new = jnp.maximum(m_old, jnp.max(s, axis=1, keepdims=True))
    alpha = jnp.exp(m_old - m_new)
    p = jnp.where(ok, jnp.exp(s - m_new), 0.0)
    l_new = alpha * l_s[:, 0:1] + jnp.sum(p, axis=1, keepdims=True)
    pb = p.astype(BF16)
    acc = alpha * acc_s[...]
    for k in range(pages_per_step):
        acc = acc + _dot_nt(pb[:, k * PAGE_SIZE:(k + 1) * PAGE_SIZE],
                            pages[k][1].reshape(LANES, PAGE_SIZE).astype(BF16))
    acc_s[...] = acc
    m_s[...] = jnp.broadcast_to(m_new, m_s.shape)
    l_s[...] = jnp.broadcast_to(l_new, l_s.shape)

    @pl.when(pc == pl.num_programs(1) - 1)
    def _():
        kn = knew_ref[...].astype(BF16).astype(F32)
        vn = vnew_ref[...].astype(BF16).astype(F32)
        s_n = jnp.sum(qf * kn, axis=1, keepdims=True)
        ok_n = selnew_ref[:, 0:1] > 0.5
        s_n = jnp.where(ok_n, s_n, NEG_INF)
        m_o = m_s[:, 0:1]
        m_f = jnp.maximum(m_o, s_n)
        a_f = jnp.exp(m_o - m_f)
        p_n = jnp.where(ok_n, jnp.exp(s_n - m_f), 0.0)
        l_f = a_f * l_s[:, 0:1] + p_n
        o_s = (a_f * acc_s[...] + p_n.astype(BF16).astype(F32) * vn) / l_f

        w_buf = win_ref.shape[-1]
        idx = lax.broadcasted_iota(jnp.int32, (1, w_buf), 1)
        dist_w = w_buf - idx
        ok_w = (dist_w <= WINDOW) & ((past_len - dist_w) >= 0)
        s_w = _dot(q, win_ref[0].reshape(LANES, w_buf).astype(BF16)) - slope * dist_w.astype(F32)
        s_w = jnp.where(ok_w, s_w, NEG_INF)
        kwn = kwnew_ref[...].astype(BF16).astype(F32)
        vwn = vwnew_ref[...].astype(BF16).astype(F32)
        s_wn = jnp.sum(qf * kwn, axis=1, keepdims=True)
        m_w = jnp.maximum(jnp.max(s_w, axis=1, keepdims=True), s_wn)
        e_w = jnp.exp(s_w - m_w)
        e_wn = jnp.exp(s_wn - m_w)
        den = jnp.sum(e_w, axis=1, keepdims=True) + e_wn
        p_w = (e_w / den).astype(BF16)
        p_wn = (e_wn / den).astype(BF16).astype(F32)
        o_w = _dot_nt(p_w, win_ref[1].reshape(LANES, w_buf).astype(BF16)) + p_wn * vwn

        gate = jax.nn.sigmoid(graw_ref[...])
        out_ref[...] = gate[:, 0:1] * oc_ref[...] + gate[:, 1:2] * o_s + gate[:, 2:3] * o_w


def _nsa_dec_sel(page_table, qm, selr, selnew, knew_s, vnew_s, knew_w, vnew_w, win, oc, graw, cache, p, layer,
                 past_len):
    bs = qm.shape[0]
    n_pages = past_len // PAGE_SIZE
    pps = PAGES_PER_STEP
    w_buf = win.shape[-1]
    per_b =lambda a: pl.BlockSpec((None,) + a.shape[1:], lambda b, pc, pt: (b,) + (0,) * (a.ndim - 1))
    page_spec = lambda k: pl.BlockSpec((None, None, 2, KV_HEADS, HEAD_DIM, PAGE_SIZE),
                                       lambda b, pc, pt, k=k: (layer, pt[b, pc * pps + k], 0, 0, 0, 0))
    slope = p["slope_rows"]
    grid_spec = pltpu.PrefetchScalarGridSpec(
        num_scalar_prefetch=1,
        grid=(bs, n_pages // pps),
        in_specs=[
            per_b(qm),
            pl.BlockSpec(slope.shape, lambda b, pc, pt: (0, 0)),
            pl.BlockSpec((None, None, NSA_HEADS, LANES), lambda b, pc, pt: (b, pc, 0, 0)),
            per_b(selnew), per_b(knew_s), per_b(vnew_s), per_b(knew_w), per_b(vnew_w),
            pl.BlockSpec((None, None, 2, KV_HEADS, HEAD_DIM, w_buf), lambda b, pc, pt: (layer, b, 0, 0, 0, 0)),
            per_b(oc), per_b(graw),
        ] + [page_spec(k) for k in range(pps)],
        out_specs=pl.BlockSpec((None, NSA_HEADS, LANES), lambda b, pc, pt: (b, 0, 0)),
        scratch_shapes=[pltpu.VMEM((NSA_HEADS, LANES), F32)] * 3,
    )
    return pl.pallas_call(
        functools.partial(_nsa_dec_sel_kernel, past_len=past_len, pages_per_step=pps),
        grid_spec=grid_spec,
        out_shape=jax.ShapeDtypeStruct((bs, NSA_HEADS, LANES), F32),
        compiler_params=_params(("parallel", "arbitrary")),
        name="nsa_dec_sel",
    )(page_table, qm, slope, selr, selnew, knew_s, vnew_s, knew_w, vnew_w, win, oc, graw, *([cache] * pps))


def _outproj_kernel(mabc_p, md_p, mabc_s, md_s, x, wo, g1, b1, wr, br, x1_ref, ids_ref, wts_ref, *,
                    alpha, n_prompt_tiles):
    i = pl.program_id(0)

    @pl.when(i < n_prompt_tiles)
    def _():
        _outproj_tile(mabc_p, md_p, x, wo, g1, b1, wr, br, x1_ref, ids_ref, wts_ref, alpha)

    @pl.when(i >= n_prompt_tiles)
    def _():
        _outproj_tile(mabc_s, md_s, x, wo, g1, b1, wr, br, x1_ref, ids_ref, wts_ref, alpha)


def _outproj_tile(mabc, md, x, wo, g1, b1, wr, br, x1_ref, ids_ref, wts_ref, alpha):
    k_abc = mabc.shape[1]
    mix = _dot(mabc[...], wo[0:k_abc, :]) + _dot(md[...], wo[k_abc:, :])
    x1 = _layer_norm(alpha * x[...] + mix, g1[...], b1[...])
    x1_ref[...] = x1
    logits = _dot(x1.astype(BF16), wr[...]) + br[...]
    lane = lax.broadcasted_iota(jnp.int32, logits.shape, 1)
    is_g = lane < N_GROUPS
    gl = jnp.where(is_g, logits, -jnp.inf)
    gmax = jnp.max(gl, axis=1, keepdims=True)
    gsel = jnp.min(jnp.where(gl == gmax, lane, LANES), axis=1, keepdims=True)
    ggate = 1.0 / jnp.sum(jnp.where(is_g, jnp.exp(gl - gmax), 0.0), axis=1, keepdims=True)
    lo = N_GROUPS + gsel * EXPERTS_PER_GROUP
    el = jnp.where((lane >= lo) & (lane < lo + EXPERTS_PER_GROUP), logits, -jnp.inf)
    v1 = jnp.max(el, axis=1, keepdims=True)
    i1 = jnp.min(jnp.where(el == v1, lane, LANES), axis=1, keepdims=True)
    el2 = jnp.where(lane == i1, -jnp.inf, el)
    v2 = jnp.max(el2, axis=1, keepdims=True)
    i2 = jnp.min(jnp.where(el2 == v2, lane, LANES), axis=1, keepdims=True)
    e21 = jnp.exp(v2 - v1)
    w1 = ggate / (1.0 + e21)
    w2 = ggate * e21 / (1.0 + e21)
    ids_ref[...] = jnp.where(lane == 0, i1 - N_GROUPS, jnp.where(lane == 1, i2 - N_GROUPS, 0))
    wts_ref[...] = jnp.where(lane == 0, w1, jnp.where(lane == 1, w2, 0.0))


def _outproj(mabc_p, md_p, mabc_s, md_s, x, wo_bf, g1, b1, wr_bf, br, alpha):
    nt = x.shape[0]
    tm = TOK_TILE
    n_p = mabc_p.shape[0] // tm
    assert mabc_p.shape[0] % tm == 0 and mabc_s.shape[0] == nt - mabc_p.shape[0]
    row = lambda w: pl.BlockSpec((tm, w), lambda i: (i, 0))
    prow = lambda w: pl.BlockSpec((tm, w), lambda i: (jnp.minimum(i, n_p - 1), 0))
    srow = lambda w: pl.BlockSpec((tm, w), lambda i: (jnp.maximum(i - n_p, 0), 0))
    full = lambda a: pl.BlockSpec(a.shape, lambda i: (0,) * a.ndim)
    return pl.pallas_call(
        functools.partial(_outproj_kernel, alpha=alpha, n_prompt_tiles=n_p),
        grid=(nt // tm,),
        in_specs=[prow(mabc_p.shape[1]), prow(md_p.shape[1]), srow(mabc_s.shape[1]), srow(md_s.shape[1]),
                  row(D_MODEL), full(wo_bf), full(g1), full(b1), full(wr_bf), full(br)],
        out_specs=[row(D_MODEL), row(LANES), row(LANES)],
        out_shape=[jax.ShapeDtypeStruct((nt, D_MODEL), F32),
                   jax.ShapeDtypeStruct((nt, LANES), jnp.int32), jax.ShapeDtypeStruct((nt, LANES), F32)],
        compiler_params=_params(("parallel",)),
        name="outproj_ln1_route",
    )(mabc_p, md_p, mabc_s, md_s, x, wo_bf, g1, b1, wr_bf, br)


def _rows_wait(hbm, buf, sem):
    pltpu.make_async_copy(hbm.at[pl.ds(0, buf.shape[0]), :], buf, sem).wait()


def _moe_kernel(te_ref, tv_ref, first_ref, nxt_ref, st_ref, dst_ref,
                x_hbm, w_ref, wg_hbm, wu_hbm, wd_hbm, y_hbm,
                xb0, xb1, ob0, ob1, wgf, wuf, wdf, wgb, wub, wdb, gsem, ssem, wsem, *, layer):
    i = pl.program_id(0)
    tm = xb0.shape[0]
    xbufs, obufs = (xb0, xb1), (ob0, ob1)
    valid = tv_ref[i] > 0
    prev_valid = tv_ref[jnp.maximum(i - 1, 0)] > 0

    def weights_copy(e, start):
        for src, dst, k in ((wg_hbm, wgf, 0), (wu_hbm, wuf, 1), (wd_hbm, wdf, 2)):
            cp = pltpu.make_async_copy(src.at[layer, e], dst, wsem.at[k])
            cp.start() if start else cp.wait()

    def start_gather(tile, k):
        for r in range(tm):
            pltpu.make_async_copy(x_hbm.at[pl.ds(st_ref[tile * tm + r], 1), :], xbufs[k].at[pl.ds(r, 1), :],
                                  gsem.at[k]).start()

    def start_scatter(tile, k):
        for r in range(tm):
            pltpu.make_async_copy(obufs[k].at[pl.ds(r, 1), :], y_hbm.at[pl.ds(dst_ref[(tile + 2) * tm + r], 1), :],
                                  ssem.at[k]).start()

    @pl.when(i == 0)
    def _():
        weights_copy(te_ref[0], start=True)
        start_gather(0, 0)
        ob0[...] = jnp.zeros(ob0.shape, F32)
        ob1[...] = jnp.zeros(ob1.shape, F32)
        start_scatter(-2, 0)

    @pl.when(valid & (first_ref[i] > 0))
    def _():
        weights_copy(te_ref[i], start=False)
        wgb[...] = wgf[...].astype(BF16)
        wub[...] = wuf[...].astype(BF16)
        wdb[...] = wdf[...].astype(BF16)

        @pl.when(nxt_ref[i] >= 0)
        def _():
            weights_copy(nxt_ref[i], start=True)

    def tile_step(k):
        _rows_wait(x_hbm, xbufs[k], gsem.at[k])
        _rows_wait(y_hbm, obufs[k], ssem.at[k])
        start_gather(i + 1, 1 - k)
        start_scatter(i - 1, 1 - k)
        x = xbufs[k][...].astype(BF16)
        hg = _dot(x, wgb[...])
        hu = _dot(x, wub[...])
        hidden = (hg * jax.nn.sigmoid(hg)) * hu * w_ref[...]
        obufs[k][...] = _dot(hidden.astype(BF16), wdb[...])

    def drain(k):
        _rows_wait(x_hbm, xbufs[k], gsem.at[k])
        _rows_wait(y_hbm, obufs[k], ssem.at[k])
        start_scatter(i - 1, 1 - k)
        _rows_wait(y_hbm, obufs[1 - k], ssem.at[1 - k])

    for k in range(2):
        @pl.when(valid & (i % 2 == k))
        def _(k=k):
            tile_step(k)

        @pl.when(jnp.logical_not(valid) & prev_valid & (i % 2 == k))
        def _(k=k):
            drain(k)


def _moe(route, x1, w_gate_e, w_up_e, w_down_e, layer):
    nt = x1.shape[0]
    n_slots = route["slot_token"].shape[0]
    tm = MOE_TM
    any_spec = pl.BlockSpec(memory_space=pl.ANY)
    grid_spec = pltpu.PrefetchScalarGridSpec(
        num_scalar_prefetch=6,
        grid=(n_slots // tm,),
        in_specs=[any_spec, pl.BlockSpec((tm, 1), lambda i, *_: (i, 0)), any_spec, any_spec, any_spec],
        out_specs=any_spec,
        scratch_shapes=[
            pltpu.VMEM((tm, D_MODEL), F32), pltpu.VMEM((tm, D_MODEL), F32),
            pltpu.VMEM((tm, D_MODEL), F32), pltpu.VMEM((tm, D_MODEL), F32),
            pltpu.VMEM((D_MODEL, D_EXPERT), F32), pltpu.VMEM((D_MODEL, D_EXPERT), F32),
            pltpu.VMEM((D_EXPERT, D_MODEL), F32),
            pltpu.VMEM((D_MODEL, D_EXPERT), BF16), pltpu.VMEM((D_MODEL, D_EXPERT), BF16),
            pltpu.VMEM((D_EXPERT, D_MODEL), BF16),
            pltpu.SemaphoreType.DMA((2,)), pltpu.SemaphoreType.DMA((2,)), pltpu.SemaphoreType.DMA((3,)),
        ],
    )
    lead_dest = 2 * nt + jnp.arange(2 * tm, dtype=jnp.int32)
    return pl.pallas_call(
        functools.partial(_moe_kernel, layer=layer),
        grid_spec=grid_spec,
        out_shape=jax.ShapeDtypeStruct((2 * nt + 2 * tm, D_MODEL), F32),
        compiler_params=_params(("arbitrary",)),
        name="moe_experts",
    )(route["tile_expert"], route["tile_valid"], route["tile_first"], route["tile_next_expert"],
      route["slot_token"], jnp.concatenate([lead_dest, route["slot_dest"]]), x1, route["slot_w"][:, None],
      w_gate_e, w_up_e, w_down_e)


def _ln2_kernel(x1, y0, y1, g2, b2, o_ref, *, alpha):
    o_ref[...] = _layer_norm(alpha * x1[...] + (y0[...] + y1[...]), g2[...], b2[...])


def _ln2(x1, y_pairs, g2, b2, alpha):
    nt = x1.shape[0]
    tm = TOK_TILE
    n_tiles = nt // tm
    row = pl.BlockSpec((tm, D_MODEL), lambda i: (i, 0))
    second = pl.BlockSpec((tm, D_MODEL), lambda i: (n_tiles + i, 0))
    full = lambda a: pl.BlockSpec(a.shape, lambda i: (0,) * a.ndim)
    return pl.pallas_call(
        functools.partial(_ln2_kernel, alpha=alpha),
        grid=(n_tiles,),
        in_specs=[row, row, second, full(g2), full(b2)],
        out_specs=row,
        out_shape=jax.ShapeDtypeStruct((nt, D_MODEL), F32),
        compiler_params=_params(("parallel",)),
        name="combine_ln2",
    )(x1, y_pairs, y_pairs, g2, b2)


def _route_slots(ids, wts, n_slots):
    tm = MOE_TM
    nt = ids.shape[0]
    e_flat = ids[:, :2].reshape(-1)
    w_flat = wts[:, :2].reshape(-1)
    n_pairs = e_flat.shape[0]
    take = lambda table, idx: table.at[idx].get(mode="promise_in_bounds")
    experts = jnp.arange(N_EXPERTS, dtype=jnp.int32)
    order = jnp.argsort(e_flat, stable=True).astype(jnp.int32)
    counts = jnp.sum(e_flat[:, None] == experts[None, :], axis=0).astype(jnp.int32)
    padded = (counts + tm - 1) // tm * tm
    ends = jnp.cumsum(padded)
    offs = ends - padded
    starts = jnp.cumsum(counts) - counts
    tile_start = jnp.arange(n_slots // tm, dtype=jnp.int32) * tm
    total = ends[-1]
    tile_valid = (tile_start < total).astype(jnp.int32)
    last_start = jnp.maximum(total - tm, 0)
    tile_expert = jnp.sum(jnp.minimum(tile_start, last_start)[:, None] >= ends[None, :], axis=1).astype(jnp.int32)
    tile_expert = jnp.minimum(tile_expert, N_EXPERTS - 1)
    in_run = tile_start[:, None] + jnp.arange(tm, dtype=jnp.int32)[None, :] - take(offs, tile_expert)[:, None]
    filled = (in_run < take(counts, tile_expert)[:, None]) & (tile_valid[:, None] > 0)
    src = jnp.clip(take(starts, tile_expert)[:, None] + in_run, 0, n_pairs - 1).reshape(-1)
    pair = take(order, src)
    filled = filled.reshape(-1)
    slot_token = jnp.where(filled, pair // 2, 0)
    slot_w = jnp.where(filled, take(w_flat, pair), 0.0)
    slot_idx = jnp.arange(n_slots, dtype=jnp.int32)
    slot_dest = jnp.where(filled, (pair % 2) * nt + pair // 2, 2 * nt + slot_idx % (2 * tm))
    prev_expert = jnp.concatenate([jnp.full((1,), -1, jnp.int32), tile_expert[:-1]])
    tile_first = (tile_expert != prev_expert).astype(jnp.int32)
    later = (experts[None, :] > experts[:, None]) & (counts[None, :] > 0)
    next_expert = jnp.min(jnp.where(later, experts[None, :], N_EXPERTS), axis=1)
    next_expert = jnp.where(next_expert == N_EXPERTS, -1, next_expert).astype(jnp.int32)
    return dict(slot_token=slot_token, slot_w=slot_w, slot_dest=slot_dest, tile_expert=tile_expert,
                tile_valid=tile_valid, tile_first=tile_first, tile_next_expert=take(next_expert, tile_expert))


def _transpose_kernel(x_ref, o_ref):
    o_ref[...] = x_ref[...].T


def _kv_rows_minor(h, batch, seq):
    tt = KEY_CHUNK
    nt = seq // tt
    return pl.pallas_call(
        _transpose_kernel,
        grid=(batch, 3, nt),
        in_specs=[pl.BlockSpec((tt, 2 * LANES), lambda b, br, t: (b * nt + t, COL_KV // (2 * LANES) + br))],
        out_specs=pl.BlockSpec((None, None, 2 * LANES, tt), lambda b, br, t: (b, br, 0, t)),
        out_shape=jax.ShapeDtypeStruct((batch, 3, 2 * LANES, seq), F32),
        compiler_params=_params(("parallel", "parallel", "parallel")),
        name="kv_rows_minor",
    )(h)


def _layer(x, layer, n_prompt_rows, batch, seq, bs, past_len, page_table, caches, states, weights):
    (cache_cmp, cache_sel, state_win, state_conv_a, state_conv_c) = (caches[0], caches[1], states[0], states[1],
                                                                      states[2])
    w = weights
    nt = x.shape[0]
    n_tok = n_prompt_rows + bs
    depth = w["w_in"].shape[0]
    alpha = (2.0 * depth) ** 0.25
    rep_g = lambda a: jnp.repeat(a, HEAD_DIM, axis=1)
    bd = lambda a: jnp.zeros((LANES, LANES), F32).at[:HEAD_DIM, :HEAD_DIM].set(a[0]).at[HEAD_DIM:, HEAD_DIM:].set(a[1])
    row = lambda a: a[layer][None, :]
    p = {
        "conv_a_w": w["conv_a_w"][layer], "conv_a_b": row(w["conv_a_b"]),
        "ln_a_g": row(w["ln_a_g"]), "ln_a_b": row(w["ln_a_b"]), "ln_v_g": row(w["ln_v_g"]), "ln_v_b": row(w["ln_v_b"]),
        "spatial_w": w["spatial_w"][layer],
        "spatial_b_rows": jnp.repeat(w["spatial_b"][layer].T, CHUNK, axis=1),
        "spatial_w00": jnp.repeat(w["spatial_w"][layer][:, 0, 0], CHUNK)[None, :],
        "spatial_b0": jnp.repeat(w["spatial_b"][layer][:, 0], CHUNK)[None, :],
        "conv_c_w": w["conv_c_w"][layer],
        "posk_rows": rep_g(w["cmp_pos_k"][layer]), "posv_rows": rep_g(w["cmp_pos_v"][layer]),
        "projk_bd": bd(w["cmp_proj_k"][layer]), "projv_bd": bd(w["cmp_proj_v"][layer]),
        "slope_rows": jnp.broadcast_to(jnp.asarray(SLOPES, F32)[:, None], (NSA_HEADS, LANES)),
    }

    h = _inproj(x, w["w_in"], w["b_in"], layer)

    mabc_p, sta_p, stc_p = _mix_abc_prompt(h, p, batch, seq)
    md_p = _nsa_prompt(h, p, batch, seq)

    hs = h[n_prompt_rows:n_tok]
    sta_t = jnp.transpose(state_conv_a[layer], (1, 0, 2))
    stc_t = jnp.transpose(state_conv_c[layer], (1, 0, 2))
    mabc_s, a_in_s, ccx_s, v_s = _mix_abc_sample(hs[:, :7 * D_GRP], sta_t, stc_t, p)
    q_s = hs[:, COL_Q:COL_KV].reshape(bs, NSA_HEADS, HEAD_DIM)
    grp = jnp.arange(NSA_HEADS) // HEADS_PER_KV
    qm = jnp.where((jnp.arange(LANES)[None, :] // HEAD_DIM == grp[:, None])[None],
                   jnp.tile(q_s, (1, 1, KV_HEADS)), 0.0)
    kv_s = hs[:, COL_KV:COL_G].reshape(bs, 3, 2, LANES)
    graw = jnp.pad(jnp.transpose(hs[:, COL_G:N_IN].reshape(bs, 3, NSA_HEADS), (0, 2, 1)),
                   ((0, 0), (0, 0), (0, LANES - 3)))
    rows_minor = lambda a: jnp.transpose(a, (0, 1, 3, 4, 5, 2))
    cache_cmp_r, cache_sel_r, win_r = rows_minor(cache_cmp), rows_minor(cache_sel), rows_minor(state_win)
    oc, sel = _nsa_dec_cmp(page_table, qm, cache_cmp_r, p, layer, past_len)
    n_steps = past_len // PAGE_SIZE // PAGES_PER_STEP
    n_past_blocks = past_len // SEL_BLOCK
    blocks_per_step = n_past_blocks // n_steps
    selr = jnp.transpose(sel[:, :, :n_past_blocks].reshape(bs, NSA_HEADS, n_steps, blocks_per_step), (0, 2, 1, 3))
    selr = jnp.pad(selr, ((0, 0), (0, 0), (0, 0), (0, LANES - blocks_per_step)))
    selnew = jnp.broadcast_to(sel[:, :, n_past_blocks:n_past_blocks + 1], (bs, NSA_HEADS, LANES))
    d_s = _nsa_dec_sel(page_table, qm, selr, selnew, kv_s[:, 1, 0][:, None], kv_s[:, 1, 1][:, None],
                       kv_s[:, 2, 0][:, None], kv_s[:, 2, 1][:, None], win_r, oc, graw, cache_sel_r, p, layer,
                       past_len)
    d_s = d_s.reshape(bs, NSA_HEADS, KV_HEADS, HEAD_DIM)
    md_s = jnp.concatenate([d_s[:, :HEADS_PER_KV, 0], d_s[:, HEADS_PER_KV:, 1]], axis=1).reshape(bs, D_GRP)

    pad_rows = nt - n_tok
    s_rows = nt - n_prompt_rows
    mabc_s = jnp.pad(mabc_s, ((0, s_rows - bs), (0, 0)))
    md_s = jnp.pad(md_s.astype(BF16), ((0, s_rows - bs), (0, 0)))
    wr = jnp.concatenate([w["w_group"][layer], w["w_router"][layer]], axis=1)
    wr = jnp.pad(wr, ((0, 0), (0, LANES - wr.shape[1]))).astype(BF16)
    br = jnp.pad(jnp.concatenate([w["b_group"][layer], w["b_router"][layer]]), (0, LANES - N_GROUPS - N_EXPERTS))[None]
    x1, ids, wts = _outproj(mabc_p, md_p, mabc_s, md_s, x, w["w_out"][layer].astype(BF16), row(w["ln1_g"]),
                            row(w["ln1_b"]), wr, br, alpha)

    n_slots = (-(-(2 * nt) // MOE_TM) + N_EXPERTS) * MOE_TM
    route = _route_slots(ids, wts, n_slots)
    y_pairs = _moe(route, x1, w["w_gate_e"], w["w_up_e"], w["w_down_e"], layer)
    x2 = _ln2(x1, y_pairs, row(w["ln2_g"]), row(w["ln2_b"]), alpha)

    kv_t = _kv_rows_minor(h, batch, seq).reshape(batch, 3, 2, KV_HEADS, HEAD_DIM, seq)
    kv_p = jnp.transpose(kv_t, (0, 5, 1, 2, 3, 4))
    w_buf = state_win.shape[2]
    kv_s6 = hs[:, COL_KV:COL_G].reshape(bs, 1, 3, 2, KV_HEADS, HEAD_DIM)
    st_p = (kv_p[:, :, 0], kv_p[:, :, 1], kv_p[:, seq - w_buf:, 2],
            sta_p[:, 32 - (CONV_A_WIDTH - 1):], stc_p[:, 8 - (CONV_C_WIDTH - 1):])
    st_s = (kv_s6[:, :, 0], kv_s6[:, :, 1],
            jnp.concatenate([state_win[layer][:, 1:], kv_s6[:, :, 2]], axis=1),
            jnp.concatenate([state_conv_a[layer][:, 1:], a_in_s[:, None]], axis=1),
            jnp.concatenate([state_conv_c[layer][:, 1:], ccx_s[:, None]], axis=1),
            v_s[:, None])
    return x2, st_p, st_s


def kernel(x_prompt, x_sample, cache_cmp_kv, cache_sel_kv, state_win_kv, state_conv_a, state_conv_c, page_table, w_in, b_in, conv_a_w, conv_a_b, ln_a_g, ln_a_b, ln_v_g, ln_v_b, spatial_w, spatial_b, conv_c_w, cmp_pos_k, cmp_pos_v, cmp_proj_k, cmp_proj_v, w_out, ln1_g, ln1_b, ln2_g, ln2_b, w_group, b_group, w_router, b_router, w_gate_e, w_up_e, w_down_e):
    batch, seq, d_model = x_prompt.shape
    bs, dec_seq, _ = x_sample.shape
    depth = w_in.shape[0]
    past_len = page_table.shape[1] * PAGE_SIZE
    assert d_model == D_MODEL and dec_seq == 1 and w_in.shape[2] == N_IN
    assert seq % KEY_CHUNK == 0 and seq % MIX_TT == 0
    assert past_len % (PAGE_SIZE * PAGES_PER_STEP) == 0 and state_win_kv.shape[2] == WINDOW and past_len >= WINDOW
    n_prompt_rows = batch * seq
    n_tok = n_prompt_rows + bs
    nt = -(-n_tok // UNIFIED_ROW_MULTIPLE) * UNIFIED_ROW_MULTIPLE
    x = jnp.concatenate([x_prompt.reshape(n_prompt_rows, d_model), x_sample.reshape(bs, d_model),
                         jnp.zeros((nt - n_tok, d_model), F32)], axis=0)
    weights = dict(w_in=w_in.astype(BF16), b_in=b_in, conv_a_w=conv_a_w, conv_a_b=conv_a_b, ln_a_g=ln_a_g, ln_a_b=ln_a_b,
                   ln_v_g=ln_v_g, ln_v_b=ln_v_b, spatial_w=spatial_w, spatial_b=spatial_b, conv_c_w=conv_c_w,
                   cmp_pos_k=cmp_pos_k, cmp_pos_v=cmp_pos_v, cmp_proj_k=cmp_proj_k, cmp_proj_v=cmp_proj_v,
                   w_out=w_out, ln1_g=ln1_g, ln1_b=ln1_b, ln2_g=ln2_g, ln2_b=ln2_b, w_group=w_group,
                   b_group=b_group, w_router=w_router, b_router=b_router, w_gate_e=w_gate_e, w_up_e=w_up_e,
                   w_down_e=w_down_e)
    st_p, st_s = [], []
    for layer in range(depth):
        x, sp, ss = _layer(x, layer, n_prompt_rows, batch, seq, bs, past_len, page_table,
                           (cache_cmp_kv, cache_sel_kv), (state_win_kv, state_conv_a, state_conv_c), weights)
        st_p.append(sp)
        st_s.append(ss)
    y_prompt = x[:n_prompt_rows].reshape(batch, seq, d_model)
    y_sample = x[n_prompt_rows:n_tok].reshape(bs, 1, d_model)
    return (y_prompt, y_sample,
            jnp.stack([s[0] for s in st_p]), jnp.stack([s[1] for s in st_p]), jnp.stack([s[2] for s in st_p]),
            jnp.stack([s[3] for s in st_p]), jnp.stack([s[4] for s in st_p]),
            jnp.stack([s[0] for s in st_s]), jnp.stack([s[1] for s in st_s]), jnp.stack([s[2] for s in st_s]),
            jnp.stack([s[3] for s in st_s]), jnp.stack([s[4] for s in st_s]), jnp.stack([s[5] for s in st_s]))
```

```python
import functools

import jax
import jax.numpy as jnp
from jax import lax
from jax.experimental import pallas as pl
from jax.experimental.pallas import tpu as pltpu

F32 = jnp.float32
BF16 = jnp.bfloat16

D_MODEL = 2048
D_GRP = 512
NSA_HEADS = 8
HEAD_DIM = 64
KV_HEADS = 2
HEADS_PER_KV = 4
CONV_A_WIDTH = 31
CONV_C_WIDTH = 3
CHUNK = 128
GMLP_HEADS = 4
CMP_BLOCK = 32
SEL_BLOCK = 64
N_SELECT = 16
WINDOW = 512
PAGE_SIZE = 128
N_GROUPS = 4
EXPERTS_PER_GROUP = 8
N_EXPERTS = 32
D_EXPERT = 512
LN_EPS = 1e-5
NEG_INF = -1e30
FORCE_SCORE = 1e9
NOT_A_BLOCK = -3e38
MASKED_SCORE = -1e35
N_IN = 4888
COL_Q = 3584
COL_KV = 4096
COL_G = 4864
SCALE = HEAD_DIM ** -0.5
SLOPES = tuple(2.0 ** (-(h + 1)) for h in range(NSA_HEADS))

LANES = 128
SUBLANES = 8
VMEM_LIMIT = 56 * 1024 * 1024

TOK_TILE = 256
UNIFIED_ROW_MULTIPLE = 512
INPROJ_ROW_TILES = 8
INPROJ_TN = 512
MIX_TT = 256
Q_TILE = 128
KEY_CHUNK = 512
PAGES_PER_STEP = 32
MOE_TM = 256


def _dot(a, b):
    return jnp.dot(a, b, preferred_element_type=F32)


def _dot_nt(a, b):
    return lax.dot_general(a, b, (((1,), (1,)), ((), ())), preferred_element_type=F32)


def _layer_norm(x, g, b):
    mu = jnp.mean(x, axis=-1, keepdims=True)
    xc = x - mu
    var = jnp.mean(xc * xc, axis=-1, keepdims=True)
    return xc * lax.rsqrt(var + LN_EPS) * g + b


def _split3(x):
    hi = x.astype(BF16)
    r = x - hi.astype(F32)
    mid = r.astype(BF16)
    lo = (r - mid.astype(F32)).astype(BF16)
    return hi, mid, lo


TM_CHUNKS = D_MODEL // LANES


def _store_token_major(ref, x):
    rows = x.shape[0]
    for c in range(TM_CHUNKS):
        ref[pl.ds(c, rows, stride=TM_CHUNKS), :] = x[:, c * LANES:(c + 1) * LANES]


def _load_token_major(ref):
    rows = ref.shape[0] // TM_CHUNKS
    return jnp.concatenate([ref[pl.ds(c, rows, stride=TM_CHUNKS), :] for c in range(TM_CHUNKS)], axis=1)


def _params(sem):
    return pltpu.CompilerParams(dimension_semantics=sem, vmem_limit_bytes=VMEM_LIMIT)


def _inproj_kernel(x_ref, w_ref, b_ref, o_ref, xb_ref):
    @pl.when(pl.program_id(1) == 0)
    def _():
        xb_ref[...] = x_ref[...].astype(BF16)

    o_ref[...] = _dot(xb_ref[...], w_ref[...]) + b_ref[...]


def _inproj(x, w_in, b_in, layer):
    nt, d = x.shape
    n_in = w_in.shape[2]
    tn = INPROJ_TN
    tm = nt // INPROJ_ROW_TILES
    assert nt % (INPROJ_ROW_TILES * SUBLANES) == 0
    return pl.pallas_call(
        _inproj_kernel,
        grid=(nt // tm, pl.cdiv(n_in, tn)),
        in_specs=[
            pl.BlockSpec((tm, d), lambda i, j: (i, 0)),
            pl.BlockSpec((None, d, tn), lambda i, j: (layer, 0, j)),
            pl.BlockSpec((1, tn), lambda i, j: (0, j)),
        ],
        out_specs=pl.BlockSpec((tm, tn), lambda i, j: (i, j)),
        out_shape=jax.ShapeDtypeStruct((nt, n_in), F32),
        scratch_shapes=[pltpu.VMEM((tm, d), BF16)],
        compiler_params=_params(("parallel", "arbitrary")),
        name="inproj",
    )(x, w_in, b_in[layer][None, :])


def _mix_abc_kernel(aval, agate, bu, bv, cb, cc, cx, caw, cab, lag, lab, lvg, lvb, sw, sb, ccw,
                    out_ref, sta_ref, stc_ref, abuf, cbuf, *, tt):
    t = pl.program_id(1)
    halo_a = 32
    halo_c = 8

    @pl.when(t == 0)
    def _():
        abuf[0:halo_a, :] = jnp.zeros((halo_a, D_GRP), F32)
        cbuf[0:halo_c, :] = jnp.zeros((halo_c, D_GRP), F32)

    abuf[halo_a:halo_a + tt, :] = aval[...] * jax.nn.sigmoid(agate[...])
    rc = 64
    off_a = halo_a - (CONV_A_WIDTH - 1)
    for r in range(tt // rc):
        acc = jnp.zeros((rc, D_GRP), F32)
        for k in range(CONV_A_WIDTH):
            acc = acc + caw[k:k + 1, :] * abuf[r * rc + off_a + k:r * rc + off_a + k + rc, :]
        y = _layer_norm(acc + cab[...], lag[...], lab[...])
        out_ref[r * rc:(r + 1) * rc, 0:D_GRP] = (y * jax.nn.sigmoid(y)).astype(BF16)
    sta_ref[...] = abuf[tt:tt + halo_a, :]
    abuf[0:halo_a, :] = abuf[tt:tt + halo_a, :]

    row = lax.broadcasted_iota(jnp.int32, (CHUNK, CHUNK), 0)
    col = lax.broadcasted_iota(jnp.int32, (CHUNK, CHUNK), 1)
    for c in range(tt // CHUNK):
        rows = slice(c * CHUNK, (c + 1) * CHUNK)
        v = _layer_norm(jax.nn.gelu(bv[rows, :]), lvg[...], lvb[...]).astype(BF16)
        gu = jax.nn.gelu(bu[rows, :])
        for h in range(GMLP_HEADS):
            lanes = slice(h * CHUNK, (h + 1) * CHUNK)
            wh = jnp.where(col <= row, sw[h], 0.0).astype(BF16)
            s = _dot(wh, v[:, lanes]) + sb[:, lanes]
            out_ref[rows, D_GRP + h * CHUNK:D_GRP + (h + 1) * CHUNK] = (gu[:, lanes] * s).astype(BF16)

    cbuf[halo_c:halo_c + tt, :] = cc[...] * cx[...]
    off_c = halo_c - (CONV_C_WIDTH - 1)
    conv = jnp.zeros((tt, D_GRP), F32)
    for k in range(CONV_C_WIDTH):
        conv = conv + ccw[k:k + 1, :] * cbuf[off_c + k:off_c + k + tt, :]
    out_ref[:, 2 * D_GRP:3 * D_GRP] = (cb[...] * conv).astype(BF16)
    stc_ref[...] = cbuf[tt:tt + halo_c, :]
    cbuf[0:halo_c, :] = cbuf[tt:tt + halo_c, :]


def _mix_abc_prompt(h, p, batch, seq):
    tt = MIX_TT
    nt = seq // tt
    col_spec = lambda c: pl.BlockSpec((tt, D_GRP), lambda b, t, c=c: (b * nt + t, c))
    full = lambda a: pl.BlockSpec(a.shape, lambda b, t: (0,) * a.ndim)
    small = [p["conv_a_w"], p["conv_a_b"], p["ln_a_g"], p["ln_a_b"], p["ln_v_g"], p["ln_v_b"],
             p["spatial_w"], p["spatial_b_rows"], p["conv_c_w"]]
    return pl.pallas_call(
        functools.partial(_mix_abc_kernel, tt=tt),
        grid=(batch, nt),
        in_specs=[col_spec(c) for c in range(7)] + [full(a) for a in small],
        out_specs=[
            pl.BlockSpec((tt, 3 * D_GRP), lambda b, t: (b * nt + t, 0)),
            pl.BlockSpec((None, 32, D_GRP), lambda b, t: (b, 0, 0)),
            pl.BlockSpec((None, 8, D_GRP), lambda b, t: (b, 0, 0)),
        ],
        out_shape=[
            jax.ShapeDtypeStruct((batch * seq, 3 * D_GRP), BF16),
            jax.ShapeDtypeStruct((batch, 32, D_GRP), F32),
            jax.ShapeDtypeStruct((batch, 8, D_GRP), F32),
        ],
        scratch_shapes=[pltpu.VMEM((32 + tt, D_GRP), F32), pltpu.VMEM((8 + tt, D_GRP), F32)],
        compiler_params=_params(("parallel", "arbitrary")),
        name="mix_abc_prompt",
    )(*([h] * 7), *small)


def _place_head(q_ref, hh, g, lane_half):
    tile = q_ref[:, (hh // 2) * LANES:(hh // 2 + 1) * LANES]
    if hh % 2 != g:
        tile = pltpu.roll(tile, HEAD_DIM, axis=1)
    return jnp.where(lane_half == g, tile * SCALE, 0.0).astype(BF16)


def _nsa_prompt_kernel(q_ref, g_ref, kvc_ref, kvs_ref, kvw_ref, posk_ref, posv_ref, pk_ref, pv_ref, pair_ref,
                       out_ref,
                       kcmp, vcmp, ks, vs, kw, vw, m_s, acc_s, sc_s, sa_s, sb_s, *, seq):
    i = pl.program_id(1)
    tq_n = Q_TILE
    n_cmp = seq // CMP_BLOCK
    n_sel = seq // SEL_BLOCK

    @pl.when(i == 0)
    def _():
        step = 256
        lane = lax.broadcasted_iota(jnp.int32, (step, LANES), 1)
        for r in range(seq // step):
            rows = slice(r * step, (r + 1) * step)
            blk = step // CMP_BLOCK
            kb = (kvc_ref[rows, 0:LANES].reshape(blk, CMP_BLOCK, LANES) * posk_ref[...][None]).sum(axis=1)
            vb = (kvc_ref[rows, LANES:2 * LANES].reshape(blk, CMP_BLOCK, LANES) * posv_ref[...][None]).sum(axis=1)
            kcmp[r * blk:(r + 1) * blk, :] = _dot(kb.astype(BF16), pk_ref[...].astype(BF16)).astype(BF16)
            vcmp[r * blk:(r + 1) * blk, :] = _dot(vb.astype(BF16), pv_ref[...].astype(BF16)).astype(BF16)
            pos = r * step + lax.broadcasted_iota(jnp.int32, (step, 1), 0)
            blk_idx = pos // SEL_BLOCK
            feat = jnp.where(lane == HEAD_DIM, blk_idx.astype(F32),
                             jnp.where(lane == HEAD_DIM + 1, (pos % SEL_BLOCK).astype(F32), 0.0))
            onehot = jnp.where(lane == blk_idx, 1.0, 0.0)
            k_sel = kvs_ref[rows, 0:LANES]
            v_sel = kvs_ref[rows, LANES:2 * LANES]
            for g in range(KV_HEADS):
                k_g = k_sel if g == 0 else pltpu.roll(k_sel, HEAD_DIM, axis=1)
                ks[g, rows, :] = jnp.concatenate([jnp.where(lane < HEAD_DIM, k_g, feat), onehot],
                                                 axis=1).astype(BF16)
                vs[g, rows, :] = jnp.where(lane // HEAD_DIM == g, v_sel, 1.0).astype(BF16)
            kw[WINDOW + r * step:WINDOW + (r + 1) * step, :] = kvw_ref[rows, 0:LANES].astype(BF16)
            vw[WINDOW + r * step:WINDOW + (r + 1) * step, :] = kvw_ref[rows, LANES:2 * LANES].astype(BF16)
        kw[0:WINDOW, :] = jnp.zeros((WINDOW, LANES), BF16)
        vw[0:WINDOW, :] = jnp.zeros((WINDOW, LANES), BF16)

    t0 = i * tq_n
    tq = t0 + lax.broadcasted_iota(jnp.int32, (tq_n, 1), 0)
    lane_half = lax.broadcasted_iota(jnp.int32, (tq_n, LANES), 1) // HEAD_DIM
    gate = jax.nn.sigmoid(g_ref[...])

    n_idx = lax.broadcasted_iota(jnp.int32, (1, n_cmp), 1)
    ok_c = (n_idx * CMP_BLOCK + (CMP_BLOCK - 1)) <= tq
    dist_c = tq.astype(F32) - (n_idx.astype(F32) * CMP_BLOCK + 0.5 * (CMP_BLOCK - 1))
    tq_l = t0 + lax.broadcasted_iota(jnp.int32, (1, tq_n), 1)
    cur = tq_l // SEL_BLOCK
    j_idx = lax.broadcasted_iota(jnp.int32, (n_sel, tq_n), 0)
    ok_s = j_idx <= cur
    forced = (j_idx == 0) | (j_idx >= cur - 1)
    n_win = WINDOW + tq_n
    off_w = lax.broadcasted_iota(jnp.int32, (1, n_win), 1)
    dist_w = lax.broadcasted_iota(jnp.int32, (tq_n, 1), 0) + WINDOW - off_w
    ok_w = (dist_w >= 0) & (dist_w <= WINDOW) & ((t0 - WINDOW + off_w) >= 0)
    nbase_w = jnp.where(ok_w, -dist_w.astype(F32), MASKED_SCORE)

    group_of = lambda hh: hh // HEADS_PER_KV
    rows_of = lambda hh: slice(hh * tq_n, (hh + 1) * tq_n)
    q_all = jnp.concatenate([_place_head(q_ref, hh, group_of(hh), lane_half) for hh in range(NSA_HEADS)],
                            axis=0)

    s_all = _dot_nt(q_all, kcmp[...])
    psum = [jnp.zeros((tq_n, n_cmp), F32) for _ in range(KV_HEADS)]
    p_list = []
    for hh in range(NSA_HEADS):
        s = s_all[rows_of(hh)] - SLOPES[hh] * dist_c
        s = jnp.where(ok_c, s, NEG_INF)
        e = jnp.exp(s - jnp.max(s, axis=1, keepdims=True))
        p = jnp.where(ok_c, e * (1.0 / jnp.sum(e, axis=1, keepdims=True)), 0.0)
        psum[group_of(hh)] = psum[group_of(hh)] + p
        p_list.append(p.astype(BF16))
    o_c = _dot(jnp.concatenate(p_list, axis=0), vcmp[...])

    n_top = float(min(N_SELECT, n_sel))
    pair = pair_ref[...]
    unsel = []
    for g in range(KV_HEADS):
        hi, mid, lo = _split3(psum[g])
        imp_t = _dot_nt(pair, hi) + _dot_nt(pair, mid) + _dot_nt(pair, lo)
        score = jnp.where(ok_s & forced, FORCE_SCORE, jnp.where(ok_s, imp_t, NEG_INF))
        sc_s[...] = score
        blocks = [score[b * SUBLANES:(b + 1) * SUBLANES] for b in range(n_sel // SUBLANES)]
        ranks = [jnp.zeros((SUBLANES, tq_n), F32) for _ in blocks]
        for k in range(n_sel):
            rk = sc_s[k:k + 1, :]
            for b, sb in enumerate(blocks):
                if b < k // SUBLANES:
                    beats = rk > sb
                elif b > k // SUBLANES:
                    beats = rk >= sb
                else:
                    beats = (rk > sb) | ((rk == sb) & (j_idx[b * SUBLANES:(b + 1) * SUBLANES] > k))
                ranks[b] = ranks[b] + jnp.where(beats, 1.0, 0.0)
        rank = jnp.concatenate(ranks, axis=0)
        unsel_t = jnp.where(rank < n_top, 0.0, MASKED_SCORE)
        unsel.append(jnp.concatenate([unsel_t, jnp.zeros((LANES - n_sel, tq_n), F32)], axis=0).T.astype(BF16))

    lane_q = lax.broadcasted_iota(jnp.int32, (tq_n, LANES), 1)

    def q_aug(hh):
        tile = q_ref[:, (hh // 2) * LANES:(hh // 2 + 1) * LANES]
        if hh % 2 == 1:
            tile = pltpu.roll(tile, HEAD_DIM, axis=1)
        feat = jnp.where(lane_q == HEAD_DIM, SEL_BLOCK * SLOPES[hh], jnp.where(lane_q == HEAD_DIM + 1, SLOPES[hh], 0.0))
        low = jnp.where(lane_q < HEAD_DIM, tile * SCALE, feat).astype(BF16)
        return jnp.concatenate([low, unsel[group_of(hh)]], axis=1)

    q_sel = [jnp.concatenate([q_aug(hh) for hh in range(g * HEADS_PER_KV, (g + 1) * HEADS_PER_KV)], axis=0)
             for g in range(KV_HEADS)]
    m_s[...] = jnp.full(m_s.shape, NEG_INF, F32)
    acc_s[...] = jnp.zeros(acc_s.shape, F32)

    def scores(c, buf):
        k0 = pl.multiple_of(c * KEY_CHUNK, KEY_CHUNK)
        for g in range(KV_HEADS):
            buf[g] = _dot_nt(q_sel[g], ks[g, pl.ds(k0, KEY_CHUNK), :])

    def softmax_pv(c, buf, on_diagonal):
        k0 = pl.multiple_of(c * KEY_CHUNK, KEY_CHUNK)
        if on_diagonal:
            causal = (k0 + lax.broadcasted_iota(jnp.int32, (1, KEY_CHUNK), 1)) <= tq
        masked = (lambda s: jnp.where(causal, s, MASKED_SCORE)) if on_diagonal else (lambda s: s)
        for g in range(KV_HEADS):
            alphas, p_list = [], []
            for h in range(HEADS_PER_KV):
                rows = rows_of(g * HEADS_PER_KV + h)
                local = slice(h * tq_n, (h + 1) * tq_n)
                m_old = m_s[rows, 0:1]
                m_new = jnp.maximum(m_old, jnp.max(masked(buf[g, local, :]), axis=1, keepdims=True))
                alphas.append(jnp.exp(m_old - m_new))
                p_list.append(jnp.exp(masked(buf[g, local, :]) - m_new).astype(BF16))
                m_s[rows, :] = jnp.broadcast_to(m_new, (tq_n, LANES))
            rows_g = slice(g * HEADS_PER_KV * tq_n, (g + 1) * HEADS_PER_KV * tq_n)
            acc_s[rows_g, :] = (jnp.concatenate(alphas, axis=0) * acc_s[rows_g, :]
                                + _dot(jnp.concatenate(p_list, axis=0), vs[g, pl.ds(k0, KEY_CHUNK), :]))

    n_full = t0 // KEY_CHUNK
    scores(0, sa_s)

    def chunk_pair(j, carry):
        scores(2 * j + 1, sb_s)
        softmax_pv(2 * j, sa_s, on_diagonal=False)
        scores(2 * j + 2, sa_s)
        softmax_pv(2 * j + 1, sb_s, on_diagonal=False)
        return carry

    lax.fori_loop(0, n_full // 2, chunk_pair, 0)

    @pl.when(n_full % 2 == 0)
    def _():
        softmax_pv(n_full, sa_s, on_diagonal=True)

    @pl.when(n_full % 2 == 1)
    def _():
        scores(n_full, sb_s)
        softmax_pv(n_full - 1, sa_s, on_diagonal=False)
        softmax_pv(n_full, sb_s, on_diagonal=True)
    o_s = jnp.concatenate(
        [acc_s[rows_of(hh), :] * (1.0 / acc_s[rows_of(hh), (1 - group_of(hh)) * HEAD_DIM:(1 - group_of(hh)) * HEAD_DIM + 1])
         for hh in range(NSA_HEADS)], axis=0)

    w0 = pl.multiple_of(t0, Q_TILE)
    s_all = _dot_nt(q_all, kw[pl.ds(w0, n_win), :])
    p_list = []
    for hh in range(NSA_HEADS):
        s = s_all[rows_of(hh)] + SLOPES[hh] * nbase_w
        e = jnp.exp(s - jnp.max(s, axis=1, keepdims=True))
        p_list.append((e * (1.0 / jnp.sum(e, axis=1, keepdims=True))).astype(BF16))
    o_w = _dot(jnp.concatenate(p_list, axis=0), vw[pl.ds(w0, n_win), :])

    for hh in range(NSA_HEADS):
        g = group_of(hh)
        rows = rows_of(hh)
        o = (gate[:, hh:hh + 1] * o_c[rows]
             + gate[:, NSA_HEADS + hh:NSA_HEADS + hh + 1] * o_s[rows]
             + gate[:, 2 * NSA_HEADS + hh:2 * NSA_HEADS + hh + 1] * o_w[rows])
        o = o[:, g * HEAD_DIM:(g + 1) * HEAD_DIM]
        out_ref[:, hh * HEAD_DIM:(hh + 1) * HEAD_DIM] = o.astype(BF16)


def _nsa_prompt(h, p, batch, seq):
    nq = seq // Q_TILE
    n_cmp = seq // CMP_BLOCK
    n_sel = seq // SEL_BLOCK
    full = lambda a: pl.BlockSpec(a.shape, lambda b, i: (0,) * a.ndim)
    pair = (jnp.arange(n_cmp)[None, :] // (SEL_BLOCK // CMP_BLOCK) == jnp.arange(n_sel)[:, None]).astype(BF16)
    small = [p["posk_rows"], p["posv_rows"], p["projk_bd"], p["projv_bd"], pair]
    kv_spec = lambda br: pl.BlockSpec((seq, 2 * LANES), lambda b, i, br=br: (b, COL_KV // (2 * LANES) + br))
    return pl.pallas_call(
        functools.partial(_nsa_prompt_kernel, seq=seq),
        grid=(batch, nq),
        in_specs=[
            pl.BlockSpec((Q_TILE, D_GRP), lambda b, i: (b * nq + i, COL_Q // D_GRP)),
            pl.BlockSpec((Q_TILE, LANES), lambda b, i: (b * nq + i, COL_G // LANES)),
            kv_spec(0), kv_spec(1), kv_spec(2),
        ] + [full(a) for a in small],
        out_specs=pl.BlockSpec((Q_TILE, D_GRP), lambda b, i: (b * nq + i, 0)),
        out_shape=jax.ShapeDtypeStruct((batch * seq, D_GRP), BF16),
        scratch_shapes=[
            pltpu.VMEM((n_cmp, LANES), BF16), pltpu.VMEM((n_cmp, LANES), BF16),
            pltpu.VMEM((KV_HEADS, seq, 2 * LANES), BF16), pltpu.VMEM((KV_HEADS, seq, LANES), BF16),
            pltpu.VMEM((seq + WINDOW, LANES), BF16), pltpu.VMEM((seq + WINDOW, LANES), BF16),
            pltpu.VMEM((NSA_HEADS * Q_TILE, LANES), F32),
            pltpu.VMEM((NSA_HEADS * Q_TILE, LANES), F32),
            pltpu.VMEM((n_sel, Q_TILE), F32),
            pltpu.VMEM((KV_HEADS, HEADS_PER_KV * Q_TILE, KEY_CHUNK), F32),
            pltpu.VMEM((KV_HEADS, HEADS_PER_KV * Q_TILE, KEY_CHUNK), F32),
        ],
        compiler_params=_params(("parallel", "arbitrary")),
        name="nsa_prompt",
    )(h, h, h, h, h, *small)


def _mix_abc_sample_kernel(hs, sta, stc, caw, cab, lag, lab, lvg, lvb, sw0, sb0, ccw,
                           out_ref, ain_ref, ccx_ref, v_ref):
    g = D_GRP
    a_in = hs[:, 0:g] * jax.nn.sigmoid(hs[:, g:2 * g])
    acc = caw[CONV_A_WIDTH - 1:CONV_A_WIDTH, :] * a_in
    for k in range(CONV_A_WIDTH - 1):
        acc = acc + caw[k:k + 1, :] * sta[k]
    y = _layer_norm(acc + cab[...], lag[...], lab[...])
    out_ref[:, 0:g] = (y * jax.nn.sigmoid(y)).astype(BF16)
    ain_ref[...] = a_in

    v = _layer_norm(jax.nn.gelu(hs[:, 3 * g:4 * g]), lvg[...], lvb[...])
    v_ref[...] = v
    s = sw0[...].astype(BF16).astype(F32) * v.astype(BF16).astype(F32) + sb0[...]
    out_ref[:, g:2 * g] = (jax.nn.gelu(hs[:, 2 * g:3 * g]) * s).astype(BF16)

    ccx = hs[:, 5 * g:6 * g] * hs[:, 6 * g:7 * g]
    ccx_ref[...] = ccx
    conv = ccw[CONV_C_WIDTH - 1:CONV_C_WIDTH, :] * ccx
    for k in range(CONV_C_WIDTH - 1):
        conv = conv + ccw[k:k + 1, :] * stc[k]
    out_ref[:, 2 * g:3 * g] = (hs[:, 4 * g:5 * g] * conv).astype(BF16)


def _mix_abc_sample(hs_abc, sta_t, stc_t, p):
    bs = hs_abc.shape[0]
    args = [hs_abc, sta_t, stc_t, p["conv_a_w"], p["conv_a_b"], p["ln_a_g"], p["ln_a_b"], p["ln_v_g"], p["ln_v_b"],
            p["spatial_w00"], p["spatial_b0"], p["conv_c_w"]]
    full = lambda a: pl.BlockSpec(a.shape, lambda i: (0,) * a.ndim)
    return pl.pallas_call(
        _mix_abc_sample_kernel,
        grid=(1,),
        in_specs=[full(a) for a in args],
        out_specs=[pl.BlockSpec((bs, 3 * D_GRP), lambda i: (0, 0))] + [pl.BlockSpec((bs, D_GRP), lambda i: (0, 0))] * 3,
        out_shape=[jax.ShapeDtypeStruct((bs, 3 * D_GRP), BF16)] + [jax.ShapeDtypeStruct((bs, D_GRP), F32)] * 3,
        compiler_params=_params(("arbitrary",)),
        name="mix_abc_sample",
    )(*args)


def _nsa_dec_cmp_kernel(pt_ref, qm_ref, slope_ref, posk_ref, posv_ref, pk_ref, pv_ref, pair_ref, *rest,
                        past_len, pages_per_step):
    pages = rest[:pages_per_step]
    oc_ref, sel_ref, kbar, vbar = rest[pages_per_step:]
    pc = pl.program_id(1)
    n_cmp = past_len // CMP_BLOCK
    n_sel = past_len // SEL_BLOCK + 1
    blk_pp = PAGE_SIZE // CMP_BLOCK
    rows_of = lambda page, kv: page[kv].reshape(LANES, PAGE_SIZE).T
    for kk in range(pages_per_step // 2):
        two_k = jnp.concatenate([rows_of(pages[2 * kk], 0), rows_of(pages[2 * kk + 1], 0)], axis=0)
        two_v = jnp.concatenate([rows_of(pages[2 * kk], 1), rows_of(pages[2 * kk + 1], 1)], axis=0)
        kb = (two_k.reshape(2 * blk_pp, CMP_BLOCK, LANES) * posk_ref[...][None]).sum(axis=1)
        vb = (two_v.reshape(2 * blk_pp, CMP_BLOCK, LANES) * posv_ref[...][None]).sum(axis=1)
        r0 = pl.multiple_of(pc * (pages_per_step * blk_pp) + kk * 2 * blk_pp, SUBLANES)
        kbar[pl.ds(r0, 2 * blk_pp), :] = kb
        vbar[pl.ds(r0, 2 * blk_pp), :] = vb

    @pl.when(pc == pl.num_programs(1) - 1)
    def _():
        kc = _dot(kbar[...].astype(BF16), pk_ref[...].astype(BF16)).astype(BF16)
        vc = _dot(vbar[...].astype(BF16), pv_ref[...].astype(BF16)).astype(BF16)
        q = (qm_ref[...] * SCALE).astype(BF16)
        slope = slope_ref[:, 0:1]
        n_idx = lax.broadcasted_iota(jnp.int32, (1, n_cmp), 1)
        center = n_idx.astype(F32) * CMP_BLOCK + 0.5 * (CMP_BLOCK - 1)
        ok_c = (n_idx * CMP_BLOCK + (CMP_BLOCK - 1)) <= past_len
        s = _dot_nt(q, kc) - slope * (float(past_len) - center)
        s = jnp.where(ok_c, s, NEG_INF)
        e = jnp.exp(s - jnp.max(s, axis=1, keepdims=True))
        p = jnp.where(ok_c, e / jnp.sum(e, axis=1, keepdims=True), 0.0)
        oc_ref[...] = _dot(p.astype(BF16), vc)
        psum = jnp.concatenate(
            [jnp.broadcast_to(jnp.sum(p[g * HEADS_PER_KV:(g + 1) * HEADS_PER_KV], axis=0, keepdims=True),
                              (HEADS_PER_KV, n_cmp)) for g in range(KV_HEADS)], axis=0)
        hi, mid, lo = _split3(psum)
        pair = pair_ref[...]
        imp = _dot(hi, pair) + _dot(mid, pair) + _dot(lo, pair)
        n_lanes = imp.shape[1]
        j_idx = lax.broadcasted_iota(jnp.int32, (1, n_lanes), 1)
        cur = past_len // SEL_BLOCK
        real = j_idx < n_sel
        ok_s = j_idx <= cur
        forced = (j_idx == 0) | (j_idx >= cur - 1)
        score = jnp.where(ok_s & forced, FORCE_SCORE, jnp.where(ok_s, imp, NEG_INF))
        score = jnp.where(real, score, NOT_A_BLOCK)
        rank = jnp.zeros(score.shape, F32)
        for k in range(n_sel):
            sk = score[:, k:k + 1]
            beats = (sk > score) | ((sk == score) & (j_idx > k))
            rank = rank + jnp.where(beats, 1.0, 0.0)
        sel_ref[...] = jnp.where((rank < float(min(N_SELECT, n_sel))) & real, 1.0, 0.0)


def _nsa_dec_cmp(page_table, qm, cache, p, layer, past_len):
    bs = qm.shape[0]
    n_pages = past_len // PAGE_SIZE
    pps = PAGES_PER_STEP
    n_cmp = past_len // CMP_BLOCK
    n_sel = past_len // SEL_BLOCK + 1
    n_lanes = -(-n_sel // LANES) * LANES
    pair = (jnp.arange(n_cmp)[:, None] // (SEL_BLOCK // CMP_BLOCK) == jnp.arange(n_lanes)[None, :]).astype(BF16)
    small = [p["slope_rows"], p["posk_rows"], p["posv_rows"], p["projk_bd"], p["projv_bd"], pair]
    full = lambda a: pl.BlockSpec(a.shape, lambda b, pc, pt: (0,) * a.ndim)
    page_spec = lambda k: pl.BlockSpec((None, None, 2, KV_HEADS, HEAD_DIM, PAGE_SIZE),
                                       lambda b, pc, pt, k=k: (layer, pt[b, pc * pps + k], 0, 0, 0, 0))
    grid_spec = pltpu.PrefetchScalarGridSpec(
        num_scalar_prefetch=1,
        grid=(bs, n_pages // pps),
        in_specs=[pl.BlockSpec((None, NSA_HEADS, LANES), lambda b, pc, pt: (b, 0, 0))]
                 + [full(a) for a in small] + [page_spec(k) for k in range(pps)],
        out_specs=[pl.BlockSpec((None, NSA_HEADS, LANES), lambda b, pc, pt: (b, 0, 0)),
                   pl.BlockSpec((None, NSA_HEADS, n_lanes), lambda b, pc, pt: (b, 0, 0))],
        scratch_shapes=[pltpu.VMEM((n_cmp, LANES), F32), pltpu.VMEM((n_cmp, LANES), F32)],
    )
    return pl.pallas_call(
        functools.partial(_nsa_dec_cmp_kernel, past_len=past_len, pages_per_step=pps),
        grid_spec=grid_spec,
        out_shape=[jax.ShapeDtypeStruct((bs, NSA_HEADS, LANES), F32),
                   jax.ShapeDtypeStruct((bs, NSA_HEADS, n_lanes), F32)],
        compiler_params=_params(("parallel", "arbitrary")),
        name="nsa_dec_cmp",
    )(page_table, qm, *small, *([cache] * pps))


def _nsa_dec_sel_kernel(pt_ref, qm_ref, slope_ref, selr_ref, selnew_ref, knew_ref, vnew_ref, kwnew_ref, vwnew_ref,
                        win_ref, oc_ref, graw_ref, *rest, past_len, pages_per_step):
    pages = rest[:pages_per_step]
    out_ref, m_s, l_s, acc_s = rest[pages_per_step:]
    pc = pl.program_id(1)
    keys_per_step = pages_per_step * PAGE_SIZE
    q = (qm_ref[...] * SCALE).astype(BF16)
    qf = q.astype(F32)
    slope = slope_ref[:, 0:1]

    @pl.when(pc == 0)
    def _():
        m_s[...] = jnp.full(m_s.shape, NEG_INF, F32)
        l_s[...] = jnp.zeros(l_s.shape, F32)
        acc_s[...] = jnp.zeros(acc_s.shape, F32)

    lane = lax.broadcasted_iota(jnp.int32, (1, PAGE_SIZE), 1)
    selr = selr_ref[...]
    blocks_pp = PAGE_SIZE // SEL_BLOCK
    s_parts, ok_parts = [], []
    for k in range(pages_per_step):
        s = _dot(q, pages[k][0].reshape(LANES, PAGE_SIZE).astype(BF16))
        spos = pc * keys_per_step + k * PAGE_SIZE + lane
        dist = past_len - spos
        in_sel = selr[:, blocks_pp * k:blocks_pp * k + 1]
        for bb in range(1, blocks_pp):
            in_sel = jnp.where(lane // SEL_BLOCK == bb, selr[:, blocks_pp * k + bb:blocks_pp * k + bb + 1], in_sel)
        s_parts.append(s - slope * dist.astype(F32))
        ok_parts.append((in_sel > 0.5) & (dist >= 0))
    ok = jnp.concatenate(ok_parts, axis=1)
    s = jnp.where(ok, jnp.concatenate(s_parts, axis=1), NEG_INF)
    m_old = m_s[:, 0:1]
    m_new = jnp.maximum(m_old, jnp.max(s, axis=1, keepdims=True))
    alpha = jnp.exp(m_old - m_new)
    p = jnp.where(ok, jnp.exp(s - m_new), 0.0)
    l_new = alpha * l_s[:, 0:1] + jnp.sum(p, axis=1, keepdims=True)
    pb = p.astype(BF16)
    acc = alpha * acc_s[...]
    for k in range(pages_per_step):
        acc = acc + _dot_nt(pb[:, k * PAGE_SIZE:(k + 1) * PAGE_SIZE],
                            pages[k][1].reshape(LANES, PAGE_SIZE).astype(BF16))
    acc_s[...] = acc
    m_s[...] = jnp.broadcast_to(m_new, m_s.shape)
    l_s[...] = jnp.broadcast_to(l_new, l_s.shape)

    @pl.when(pc == pl.num_programs(1) - 1)
    def _():
        kn = knew_ref[...].astype(BF16).astype(F32)
        vn = vnew_ref[...].astype(BF16).astype(F32)
        s_n = jnp.sum(qf * kn, axis=1, keepdims=True)
        ok_n = selnew_ref[:, 0:1] > 0.5
        s_n = jnp.where(ok_n, s_n, NEG_INF)
        m_o = m_s[:, 0:1]
        m_f = jnp.maximum(m_o, s_n)
        a_f = jnp.exp(m_o - m_f)
        p_n = jnp.where(ok_n, jnp.exp(s_n - m_f), 0.0)
        l_f = a_f * l_s[:, 0:1] + p_n
        o_s = (a_f * acc_s[...] + p_n.astype(BF16).astype(F32) * vn) / l_f

        w_buf = win_ref.shape[-1]
        idx = lax.broadcasted_iota(jnp.int32, (1, w_buf), 1)
        dist_w = w_buf - idx
        ok_w = (dist_w <= WINDOW) & ((past_len - dist_w) >= 0)
        s_w = _dot(q, win_ref[0].reshape(LANES, w_buf).astype(BF16)) - slope * dist_w.astype(F32)
        s_w = jnp.where(ok_w, s_w, NEG_INF)
        kwn = kwnew_ref[...].astype(BF16).astype(F32)
        vwn = vwnew_ref[...].astype(BF16).astype(F32)
        s_wn = jnp.sum(qf * kwn, axis=1, keepdims=True)
        m_w = jnp.maximum(jnp.max(s_w, axis=1, keepdims=True), s_wn)
        e_w = jnp.exp(s_w - m_w)
        e_wn = jnp.exp(s_wn - m_w)
        den = jnp.sum(e_w, axis=1, keepdims=True) + e_wn
        p_w = (e_w / den).astype(BF16)
        p_wn = (e_wn / den).astype(BF16).astype(F32)
        o_w = _dot_nt(p_w, win_ref[1].reshape(LANES, w_buf).astype(BF16)) + p_wn * vwn

        gate = jax.nn.sigmoid(graw_ref[...])
        out_ref[...] = gate[:, 0:1] * oc_ref[...] + gate[:, 1:2] * o_s + gate[:, 2:3] * o_w


def _nsa_dec_sel(page_table, qm, selr, selnew, knew_s, vnew_s, knew_w, vnew_w, win, oc, graw, cache, p, layer,
                 past_len):
    bs = qm.shape[0]
    n_pages = past_len // PAGE_SIZE
    pps = PAGES_PER_STEP
    w_buf = win.shape[-1]
    per_b =lambda a: pl.BlockSpec((None,) + a.shape[1:], lambda b, pc, pt: (b,) + (0,) * (a.ndim - 1))
    page_spec = lambda k: pl.BlockSpec((None, None, 2, KV_HEADS, HEAD_DIM, PAGE_SIZE),
                                       lambda b, pc, pt, k=k: (layer, pt[b, pc * pps + k], 0, 0, 0, 0))
    slope = p["slope_rows"]
    grid_spec = pltpu.PrefetchScalarGridSpec(
        num_scalar_prefetch=1,
        grid=(bs, n_pages // pps),
        in_specs=[
            per_b(qm),
            pl.BlockSpec(slope.shape, lambda b, pc, pt: (0, 0)),
            pl.BlockSpec((None, None, NSA_HEADS, LANES), lambda b, pc, pt: (b, pc, 0, 0)),
            per_b(selnew), per_b(knew_s), per_b(vnew_s), per_b(knew_w), per_b(vnew_w),
            pl.BlockSpec((None, None, 2, KV_HEADS, HEAD_DIM, w_buf), lambda b, pc, pt: (layer, b, 0, 0, 0, 0)),
            per_b(oc), per_b(graw),
        ] + [page_spec(k) for k in range(pps)],
        out_specs=pl.BlockSpec((None, NSA_HEADS, LANES), lambda b, pc, pt: (b, 0, 0)),
        scratch_shapes=[pltpu.VMEM((NSA_HEADS, LANES), F32)] * 3,
    )
    return pl.pallas_call(
        functools.partial(_nsa_dec_sel_kernel, past_len=past_len, pages_per_step=pps),
        grid_spec=grid_spec,
        out_shape=jax.ShapeDtypeStruct((bs, NSA_HEADS, LANES), F32),
        compiler_params=_params(("parallel", "arbitrary")),
        name="nsa_dec_sel",
    )(page_table, qm, slope, selr, selnew, knew_s, vnew_s, knew_w, vnew_w, win, oc, graw, *([cache] * pps))


def _outproj_kernel(mabc_p, md_p, mabc_s, md_s, x, wo, g1, b1, wr, br, x1_ref, ids_ref, wts_ref, *,
                    alpha, n_prompt_tiles):
    i = pl.program_id(0)

    @pl.when(i < n_prompt_tiles)
    def _():
        _outproj_tile(mabc_p, md_p, x, wo, g1, b1, wr, br, x1_ref, ids_ref, wts_ref, alpha)

    @pl.when(i >= n_prompt_tiles)
    def _():
        _outproj_tile(mabc_s, md_s, x, wo, g1, b1, wr, br, x1_ref, ids_ref, wts_ref, alpha)


def _outproj_tile(mabc, md, x, wo, g1, b1, wr, br, x1_ref, ids_ref, wts_ref, alpha):
    k_abc = mabc.shape[1]
    mix = _dot(mabc[...], wo[0:k_abc, :]) + _dot(md[...], wo[k_abc:, :])
    x1 = _layer_norm(alpha * x[...] + mix, g1[...], b1[...])
    _store_token_major(x1_ref, x1)
    x_hi, x_lo, _ = _split3(x1)
    w_hi, w_lo, _ = _split3(wr[...])
    logits = _dot(x_hi, w_hi) + (_dot(x_lo, w_hi) + _dot(x_hi, w_lo)) + br[...]
    lane = lax.broadcasted_iota(jnp.int32, logits.shape, 1)
    is_g = lane < N_GROUPS
    gl = jnp.where(is_g, logits, -jnp.inf)
    gmax = jnp.max(gl, axis=1, keepdims=True)
    gsel = jnp.min(jnp.where(gl == gmax, lane, LANES), axis=1, keepdims=True)
    ggate = 1.0 / jnp.sum(jnp.where(is_g, jnp.exp(gl - gmax), 0.0), axis=1, keepdims=True)
    lo = N_GROUPS + gsel * EXPERTS_PER_GROUP
    el = jnp.where((lane >= lo) & (lane < lo + EXPERTS_PER_GROUP), logits, -jnp.inf)
    v1 = jnp.max(el, axis=1, keepdims=True)
    i1 = jnp.min(jnp.where(el == v1, lane, LANES), axis=1, keepdims=True)
    el2 = jnp.where(lane == i1, -jnp.inf, el)
    v2 = jnp.max(el2, axis=1, keepdims=True)
    i2 = jnp.min(jnp.where(el2 == v2, lane, LANES), axis=1, keepdims=True)
    e21 = jnp.exp(v2 - v1)
    w1 = ggate / (1.0 + e21)
    w2 = ggate * e21 / (1.0 + e21)
    ids_ref[...] = jnp.where(lane == 0, i1 - N_GROUPS, jnp.where(lane == 1, i2 - N_GROUPS, 0))
    wts_ref[...] = jnp.where(lane == 0, w1, jnp.where(lane == 1, w2, 0.0))


def _outproj(mabc_p, md_p, mabc_s, md_s, x, wo_bf, g1, b1, wr_bf, br, alpha):
    nt = x.shape[0]
    tm = TOK_TILE
    n_p = mabc_p.shape[0] // tm
    assert mabc_p.shape[0] % tm == 0 and mabc_s.shape[0] == nt - mabc_p.shape[0]
    row = lambda w: pl.BlockSpec((tm, w), lambda i: (i, 0))
    prow = lambda w: pl.BlockSpec((tm, w), lambda i: (jnp.minimum(i, n_p - 1), 0))
    srow = lambda w: pl.BlockSpec((tm, w), lambda i: (jnp.maximum(i - n_p, 0), 0))
    full = lambda a: pl.BlockSpec(a.shape, lambda i: (0,) * a.ndim)
    return pl.pallas_call(
        functools.partial(_outproj_kernel, alpha=alpha, n_prompt_tiles=n_p),
        grid=(nt // tm,),
        in_specs=[prow(mabc_p.shape[1]), prow(md_p.shape[1]), srow(mabc_s.shape[1]), srow(md_s.shape[1]),
                  row(D_MODEL), full(wo_bf), full(g1), full(b1), full(wr_bf), full(br)],
        out_specs=[pl.BlockSpec((tm * TM_CHUNKS, LANES), lambda i: (i, 0)), row(LANES), row(LANES)],
        out_shape=[jax.ShapeDtypeStruct((nt * TM_CHUNKS, LANES), F32),
                   jax.ShapeDtypeStruct((nt, LANES), jnp.int32), jax.ShapeDtypeStruct((nt, LANES), F32)],
        compiler_params=_params(("parallel",)),
        name="outproj_ln1_route",
    )(mabc_p, md_p, mabc_s, md_s, x, wo_bf, g1, b1, wr_bf, br)


def _rows_wait(hbm, buf, sem):
    pltpu.make_async_copy(hbm.at[pl.ds(0, buf.shape[0])], buf, sem).wait()


def _moe_kernel(te_ref, tv_ref, first_ref, nxt_ref, st_ref, dst_ref,
                x_hbm, w_ref, wg_hbm, wu_hbm, wd_hbm, y_hbm,
                xb0, xb1, ob0, ob1, wgf, wuf, wdf, wgb, wub, wdb, gsem, ssem, wsem, *, layer):
    i = pl.program_id(0)
    tm = xb0.shape[0] // TM_CHUNKS
    xbufs, obufs = (xb0, xb1), (ob0, ob1)

    def token_rows(n):
        start = n * TM_CHUNKS
        return pl.ds(start if isinstance(n, int) else pl.multiple_of(start, TM_CHUNKS), TM_CHUNKS)

    valid = tv_ref[i] > 0
    prev_valid = tv_ref[jnp.maximum(i - 1, 0)] > 0

    def weights_copy(e, start):
        for src, dst, k in ((wg_hbm, wgf, 0), (wu_hbm, wuf, 1), (wd_hbm, wdf, 2)):
            cp = pltpu.make_async_copy(src.at[layer, e], dst, wsem.at[k])
            cp.start() if start else cp.wait()

    def start_gather(tile, k):
        for r in range(tm):
            pltpu.make_async_copy(x_hbm.at[token_rows(st_ref[tile * tm + r]), :], xbufs[k].at[token_rows(r), :],
                                  gsem.at[k]).start()

    def start_scatter(tile, k):
        for r in range(tm):
            pltpu.make_async_copy(obufs[k].at[token_rows(r), :], y_hbm.at[token_rows(dst_ref[(tile + 2) * tm + r]), :],
                                  ssem.at[k]).start()

    @pl.when(i == 0)
    def _():
        weights_copy(te_ref[0], start=True)
        start_gather(0, 0)
        ob0[...] = jnp.zeros(ob0.shape, F32)
        ob1[...] = jnp.zeros(ob1.shape, F32)
        start_scatter(-2, 0)

    @pl.when(valid & (first_ref[i] > 0))
    def _():
        weights_copy(te_ref[i], start=False)
        wgb[...] = wgf[...].astype(BF16)
        wub[...] = wuf[...].astype(BF16)
        wdb[...] = wdf[...].astype(BF16)

        @pl.when(nxt_ref[i] >= 0)
        def _():
            weights_copy(nxt_ref[i], start=True)

    def tile_step(k):
        _rows_wait(x_hbm, xbufs[k], gsem.at[k])
        _rows_wait(y_hbm, obufs[k], ssem.at[k])
        start_gather(i + 1, 1 - k)
        start_scatter(i - 1, 1 - k)
        x = _load_token_major(xbufs[k]).astype(BF16)
        hg = _dot(x, wgb[...])
        hu = _dot(x, wub[...])
        hidden = (hg * jax.nn.sigmoid(hg)) * hu * w_ref[...]
        _store_token_major(obufs[k], _dot(hidden.astype(BF16), wdb[...]))

    def drain(k):
        _rows_wait(x_hbm, xbufs[k], gsem.at[k])
        _rows_wait(y_hbm, obufs[k], ssem.at[k])
        start_scatter(i - 1, 1 - k)
        _rows_wait(y_hbm, obufs[1 - k], ssem.at[1 - k])

    for k in range(2):
        @pl.when(valid & (i % 2 == k))
        def _(k=k):
            tile_step(k)

        @pl.when(jnp.logical_not(valid) & prev_valid & (i % 2 == k))
        def _(k=k):
            drain(k)


def _moe(route, x1, w_gate_e, w_up_e, w_down_e, layer):
    nt = x1.shape[0] // TM_CHUNKS
    n_slots = route["slot_token"].shape[0]
    tm = MOE_TM
    any_spec = pl.BlockSpec(memory_space=pl.ANY)
    grid_spec = pltpu.PrefetchScalarGridSpec(
        num_scalar_prefetch=6,
        grid=(n_slots // tm,),
        in_specs=[any_spec, pl.BlockSpec((tm, 1), lambda i, *_: (i, 0)), any_spec, any_spec, any_spec],
        out_specs=any_spec,
        scratch_shapes=[
            pltpu.VMEM((tm * TM_CHUNKS, LANES), F32), pltpu.VMEM((tm * TM_CHUNKS, LANES), F32),
            pltpu.VMEM((tm * TM_CHUNKS, LANES), F32), pltpu.VMEM((tm * TM_CHUNKS, LANES), F32),
            pltpu.VMEM((D_MODEL, D_EXPERT), F32), pltpu.VMEM((D_MODEL, D_EXPERT), F32),
            pltpu.VMEM((D_EXPERT, D_MODEL), F32),
            pltpu.VMEM((D_MODEL, D_EXPERT), BF16), pltpu.VMEM((D_MODEL, D_EXPERT), BF16),
            pltpu.VMEM((D_EXPERT, D_MODEL), BF16),
            pltpu.SemaphoreType.DMA((2,)), pltpu.SemaphoreType.DMA((2,)), pltpu.SemaphoreType.DMA((3,)),
        ],
    )
    lead_dest = 2 * nt + jnp.arange(2 * tm, dtype=jnp.int32)
    return pl.pallas_call(
        functools.partial(_moe_kernel, layer=layer),
        grid_spec=grid_spec,
        out_shape=jax.ShapeDtypeStruct(((2 * nt + 2 * tm) * TM_CHUNKS, LANES), F32),
        compiler_params=_params(("arbitrary",)),
        name="moe_experts",
    )(route["tile_expert"], route["tile_valid"], route["tile_first"], route["tile_next_expert"],
      route["slot_token"], jnp.concatenate([lead_dest, route["slot_dest"]]), x1, route["slot_w"][:, None],
      w_gate_e, w_up_e, w_down_e)


def _ln2_kernel(x1, y0, y1, g2, b2, o_ref, *, alpha):
    y = _load_token_major(y0) + _load_token_major(y1)
    o_ref[...] = _layer_norm(alpha * _load_token_major(x1) + y, g2[...], b2[...])


def _ln2(x1, y_pairs, g2, b2, alpha):
    nt = x1.shape[0] // TM_CHUNKS
    tm = TOK_TILE
    n_tiles = nt // tm
    first = pl.BlockSpec((tm * TM_CHUNKS, LANES), lambda i: (i, 0))
    second = pl.BlockSpec((tm * TM_CHUNKS, LANES), lambda i: (n_tiles + i, 0))
    row = pl.BlockSpec((tm, D_MODEL), lambda i: (i, 0))
    full = lambda a: pl.BlockSpec(a.shape, lambda i: (0,) * a.ndim)
    return pl.pallas_call(
        functools.partial(_ln2_kernel, alpha=alpha),
        grid=(n_tiles,),
        in_specs=[first, first, second, full(g2), full(b2)],
        out_specs=row,
        out_shape=jax.ShapeDtypeStruct((nt, D_MODEL), F32),
        compiler_params=_params(("parallel",)),
        name="combine_ln2",
    )(x1, y_pairs, y_pairs, g2, b2)


def _route_slots(ids, wts, n_slots):
    tm = MOE_TM
    nt = ids.shape[0]
    e_flat = ids[:, :2].reshape(-1)
    w_flat = wts[:, :2].reshape(-1)
    n_pairs = e_flat.shape[0]
    take = lambda table, idx: table.at[idx].get(mode="promise_in_bounds")
    experts = jnp.arange(N_EXPERTS, dtype=jnp.int32)
    order = jnp.argsort(e_flat, stable=True).astype(jnp.int32)
    counts = jnp.sum(e_flat[:, None] == experts[None, :], axis=0).astype(jnp.int32)
    padded = (counts + tm - 1) // tm * tm
    ends = jnp.cumsum(padded)
    offs = ends - padded
    starts = jnp.cumsum(counts) - counts
    tile_start = jnp.arange(n_slots // tm, dtype=jnp.int32) * tm
    total = ends[-1]
    tile_valid = (tile_start < total).astype(jnp.int32)
    last_start = jnp.maximum(total - tm, 0)
    tile_expert = jnp.sum(jnp.minimum(tile_start, last_start)[:, None] >= ends[None, :], axis=1).astype(jnp.int32)
    tile_expert = jnp.minimum(tile_expert, N_EXPERTS - 1)
    in_run = tile_start[:, None] + jnp.arange(tm, dtype=jnp.int32)[None, :] - take(offs, tile_expert)[:, None]
    filled = (in_run < take(counts, tile_expert)[:, None]) & (tile_valid[:, None] > 0)
    src = jnp.clip(take(starts, tile_expert)[:, None] + in_run, 0, n_pairs - 1).reshape(-1)
    pair = take(order, src)
    filled = filled.reshape(-1)
    slot_token = jnp.where(filled, pair // 2, 0)
    slot_w = jnp.where(filled, take(w_flat, pair), 0.0)
    slot_idx = jnp.arange(n_slots, dtype=jnp.int32)
    slot_dest = jnp.where(filled, (pair % 2) * nt + pair // 2, 2 * nt + slot_idx % (2 * tm))
    prev_expert = jnp.concatenate([jnp.full((1,), -1, jnp.int32), tile_expert[:-1]])
    tile_first = (tile_expert != prev_expert).astype(jnp.int32)
    later = (experts[None, :] > experts[:, None]) & (counts[None, :] > 0)
    next_expert = jnp.min(jnp.where(later, experts[None, :], N_EXPERTS), axis=1)
    next_expert = jnp.where(next_expert == N_EXPERTS, -1, next_expert).astype(jnp.int32)
    return dict(slot_token=slot_token, slot_w=slot_w, slot_dest=slot_dest, tile_expert=tile_expert,
                tile_valid=tile_valid, tile_first=tile_first, tile_next_expert=take(next_expert, tile_expert))


def _transpose_kernel(x_ref, o_ref):
    o_ref[...] = x_ref[...].T


def _kv_rows_minor(h, batch, seq):
    tt = KEY_CHUNK
    nt = seq // tt
    return pl.pallas_call(
        _transpose_kernel,
        grid=(batch, 3, nt),
        in_specs=[pl.BlockSpec((tt, 2 * LANES), lambda b, br, t: (b * nt + t, COL_KV // (2 * LANES) + br))],
        out_specs=pl.BlockSpec((None, None, 2 * LANES, tt), lambda b, br, t: (b, br, 0, t)),
        out_shape=jax.ShapeDtypeStruct((batch, 3, 2 * LANES, seq), F32),
        compiler_params=_params(("parallel", "parallel", "parallel")),
        name="kv_rows_minor",
    )(h)


def _layer(x, layer, n_prompt_rows, batch, seq, bs, past_len, page_table, caches, states, weights):
    (cache_cmp, cache_sel, state_win, state_conv_a, state_conv_c) = (caches[0], caches[1], states[0], states[1],
                                                                      states[2])
    w = weights
    nt = x.shape[0]
    n_tok = n_prompt_rows + bs
    depth = w["w_in"].shape[0]
    alpha = (2.0 * depth) ** 0.25
    rep_g = lambda a: jnp.repeat(a, HEAD_DIM, axis=1)
    bd = lambda a: jnp.zeros((LANES, LANES), F32).at[:HEAD_DIM, :HEAD_DIM].set(a[0]).at[HEAD_DIM:, HEAD_DIM:].set(a[1])
    row = lambda a: a[layer][None, :]
    p = {
        "conv_a_w": w["conv_a_w"][layer], "conv_a_b": row(w["conv_a_b"]),
        "ln_a_g": row(w["ln_a_g"]), "ln_a_b": row(w["ln_a_b"]), "ln_v_g": row(w["ln_v_g"]), "ln_v_b": row(w["ln_v_b"]),
        "spatial_w": w["spatial_w"][layer],
        "spatial_b_rows": jnp.repeat(w["spatial_b"][layer].T, CHUNK, axis=1),
        "spatial_w00": jnp.repeat(w["spatial_w"][layer][:, 0, 0], CHUNK)[None, :],
        "spatial_b0": jnp.repeat(w["spatial_b"][layer][:, 0], CHUNK)[None, :],
        "conv_c_w": w["conv_c_w"][layer],
        "posk_rows": rep_g(w["cmp_pos_k"][layer]), "posv_rows": rep_g(w["cmp_pos_v"][layer]),
        "projk_bd": bd(w["cmp_proj_k"][layer]), "projv_bd": bd(w["cmp_proj_v"][layer]),
        "slope_rows": jnp.broadcast_to(jnp.asarray(SLOPES, F32)[:, None], (NSA_HEADS, LANES)),
    }

    h = _inproj(x, w["w_in"], w["b_in"], layer)

    mabc_p, sta_p, stc_p = _mix_abc_prompt(h, p, batch, seq)
    md_p = _nsa_prompt(h, p, batch, seq)

    hs = h[n_prompt_rows:n_tok]
    sta_t = jnp.transpose(state_conv_a[layer], (1, 0, 2))
    stc_t = jnp.transpose(state_conv_c[layer], (1, 0, 2))
    mabc_s, a_in_s, ccx_s, v_s = _mix_abc_sample(hs[:, :7 * D_GRP], sta_t, stc_t, p)
    q_s = hs[:, COL_Q:COL_KV].reshape(bs, NSA_HEADS, HEAD_DIM)
    grp = jnp.arange(NSA_HEADS) // HEADS_PER_KV
    qm = jnp.where((jnp.arange(LANES)[None, :] // HEAD_DIM == grp[:, None])[None],
                   jnp.tile(q_s, (1, 1, KV_HEADS)), 0.0)
    kv_s = hs[:, COL_KV:COL_G].reshape(bs, 3, 2, LANES)
    graw = jnp.pad(jnp.transpose(hs[:, COL_G:N_IN].reshape(bs, 3, NSA_HEADS), (0, 2, 1)),
                   ((0, 0), (0, 0), (0, LANES - 3)))
    rows_minor = lambda a: jnp.transpose(a, (0, 1, 3, 4, 5, 2))
    cache_cmp_r, cache_sel_r, win_r = rows_minor(cache_cmp), rows_minor(cache_sel), rows_minor(state_win)
    oc, sel = _nsa_dec_cmp(page_table, qm, cache_cmp_r, p, layer, past_len)
    n_steps = past_len // PAGE_SIZE // PAGES_PER_STEP
    n_past_blocks = past_len // SEL_BLOCK
    blocks_per_step = n_past_blocks // n_steps
    selr = jnp.transpose(sel[:, :, :n_past_blocks].reshape(bs, NSA_HEADS, n_steps, blocks_per_step), (0, 2, 1, 3))
    selr = jnp.pad(selr, ((0, 0), (0, 0), (0, 0), (0, LANES - blocks_per_step)))
    selnew = jnp.broadcast_to(sel[:, :, n_past_blocks:n_past_blocks + 1], (bs, NSA_HEADS, LANES))
    d_s = _nsa_dec_sel(page_table, qm, selr, selnew, kv_s[:, 1, 0][:, None], kv_s[:, 1, 1][:, None],
                       kv_s[:, 2, 0][:, None], kv_s[:, 2, 1][:, None], win_r, oc, graw, cache_sel_r, p, layer,
                       past_len)
    d_s = d_s.reshape(bs, NSA_HEADS, KV_HEADS, HEAD_DIM)
    md_s = jnp.concatenate([d_s[:, :HEADS_PER_KV, 0], d_s[:, HEADS_PER_KV:, 1]], axis=1).reshape(bs, D_GRP)

    pad_rows = nt - n_tok
    s_rows = nt - n_prompt_rows
    mabc_s = jnp.pad(mabc_s, ((0, s_rows - bs), (0, 0)))
    md_s = jnp.pad(md_s.astype(BF16), ((0, s_rows - bs), (0, 0)))
    wr = jnp.concatenate([w["w_group"][layer], w["w_router"][layer]], axis=1)
    wr = jnp.pad(wr, ((0, 0), (0, LANES - wr.shape[1])))
    br = jnp.pad(jnp.concatenate([w["b_group"][layer], w["b_router"][layer]]), (0, LANES - N_GROUPS - N_EXPERTS))[None]
    x1, ids, wts = _outproj(mabc_p, md_p, mabc_s, md_s, x, w["w_out"][layer].astype(BF16), row(w["ln1_g"]),
                            row(w["ln1_b"]), wr, br, alpha)

    n_slots = (-(-(2 * nt) // MOE_TM) + N_EXPERTS) * MOE_TM
    route = _route_slots(ids, wts, n_slots)
    y_pairs = _moe(route, x1, w["w_gate_e"], w["w_up_e"], w["w_down_e"], layer)
    x2 = _ln2(x1, y_pairs, row(w["ln2_g"]), row(w["ln2_b"]), alpha)

    kv_t = _kv_rows_minor(h, batch, seq).reshape(batch, 3, 2, KV_HEADS, HEAD_DIM, seq)
    kv_p = jnp.transpose(kv_t, (0, 5, 1, 2, 3, 4))
    w_buf = state_win.shape[2]
    kv_s6 = hs[:, COL_KV:COL_G].reshape(bs, 1, 3, 2, KV_HEADS, HEAD_DIM)
    st_p = (kv_p[:, :, 0], kv_p[:, :, 1], kv_p[:, seq - w_buf:, 2],
            sta_p[:, 32 - (CONV_A_WIDTH - 1):], stc_p[:, 8 - (CONV_C_WIDTH - 1):])
    st_s = (kv_s6[:, :, 0], kv_s6[:, :, 1],
            jnp.concatenate([state_win[layer][:, 1:], kv_s6[:, :, 2]], axis=1),
            jnp.concatenate([state_conv_a[layer][:, 1:], a_in_s[:, None]], axis=1),
            jnp.concatenate([state_conv_c[layer][:, 1:], ccx_s[:, None]], axis=1),
            v_s[:, None])
    return x2, st_p, st_s


def kernel(x_prompt, x_sample, cache_cmp_kv, cache_sel_kv, state_win_kv, state_conv_a, state_conv_c, page_table, w_in, b_in, conv_a_w, conv_a_b, ln_a_g, ln_a_b, ln_v_g, ln_v_b, spatial_w, spatial_b, conv_c_w, cmp_pos_k, cmp_pos_v, cmp_proj_k, cmp_proj_v, w_out, ln1_g, ln1_b, ln2_g, ln2_b, w_group, b_group, w_router, b_router, w_gate_e, w_up_e, w_down_e):
    batch, seq, d_model = x_prompt.shape
    bs, dec_seq, _ = x_sample.shape
    depth = w_in.shape[0]
    past_len = page_table.shape[1] * PAGE_SIZE
    assert d_model == D_MODEL and dec_seq == 1 and w_in.shape[2] == N_IN
    assert seq % KEY_CHUNK == 0 and seq % MIX_TT == 0
    assert past_len % (PAGE_SIZE * PAGES_PER_STEP) == 0 and state_win_kv.shape[2] == WINDOW and past_len >= WINDOW
    n_prompt_rows = batch * seq
    n_tok = n_prompt_rows + bs
    nt = -(-n_tok // UNIFIED_ROW_MULTIPLE) * UNIFIED_ROW_MULTIPLE
    x = jnp.concatenate([x_prompt.reshape(n_prompt_rows, d_model), x_sample.reshape(bs, d_model),
                         jnp.zeros((nt - n_tok, d_model), F32)], axis=0)
    weights = dict(w_in=w_in.astype(BF16), b_in=b_in, conv_a_w=conv_a_w, conv_a_b=conv_a_b, ln_a_g=ln_a_g, ln_a_b=ln_a_b,
                   ln_v_g=ln_v_g, ln_v_b=ln_v_b, spatial_w=spatial_w, spatial_b=spatial_b, conv_c_w=conv_c_w,
                   cmp_pos_k=cmp_pos_k, cmp_pos_v=cmp_pos_v, cmp_proj_k=cmp_proj_k, cmp_proj_v=cmp_proj_v,
                   w_out=w_out, ln1_g=ln1_g, ln1_b=ln1_b, ln2_g=ln2_g, ln2_b=ln2_b, w_group=w_group,
                   b_group=b_group, w_router=w_router, b_router=b_router, w_gate_e=w_gate_e, w_up_e=w_up_e,
                   w_down_e=w_down_e)
    st_p, st_s = [], []
    for layer in range(depth):
        x, sp, ss = _layer(x, layer, n_prompt_rows, batch, seq, bs, past_len, page_table,
                           (cache_cmp_kv, cache_sel_kv), (state_win_kv, state_conv_a, state_conv_c), weights)
        st_p.append(sp)
        st_s.append(ss)
    y_prompt = x[:n_prompt_rows].reshape(batch, seq, d_model)
    y_sample = x[n_prompt_rows:n_tok].reshape(bs, 1, d_model)
    return (y_prompt, y_sample,
            jnp.stack([s[0] for s in st_p]), jnp.stack([s[1] for s in st_p]), jnp.stack([s[2] for s in st_p]),
            jnp.stack([s[3] for s in st_p]), jnp.stack([s[4] for s in st_p]),
            jnp.stack([s[0] for s in st_s]), jnp.stack([s[1] for s in st_s]), jnp.stack([s[2] for s in st_s]),
            jnp.stack([s[3] for s in st_s]), jnp.stack([s[4] for s in st_s]), jnp.stack([s[5] for s in st_s]))
```

```python
import functools

import jax
import jax.numpy as jnp
from jax import lax
from jax.experimental import pallas as pl
from jax.experimental.pallas import tpu as pltpu

F32 = jnp.float32
BF16 = jnp.bfloat16

D_MODEL = 2048
D_GRP = 512
NSA_HEADS = 8
HEAD_DIM = 64
KV_HEADS = 2
HEADS_PER_KV = 4
CONV_A_WIDTH = 31
CONV_C_WIDTH = 3
CHUNK = 128
GMLP_HEADS = 4
CMP_BLOCK = 32
SEL_BLOCK = 64
N_SELECT = 16
WINDOW = 512
PAGE_SIZE = 128
N_GROUPS = 4
EXPERTS_PER_GROUP = 8
N_EXPERTS = 32
D_EXPERT = 512
LN_EPS = 1e-5
NEG_INF = -1e30
FORCE_SCORE = 1e9
NOT_A_BLOCK = -3e38
MASKED_SCORE = -1e35
N_IN = 4888
COL_Q = 3584
COL_KV = 4096
COL_G = 4864
SCALE = HEAD_DIM ** -0.5
SLOPES = tuple(2.0 ** (-(h + 1)) for h in range(NSA_HEADS))

LANES = 128
SUBLANES = 8
VMEM_LIMIT = 56 * 1024 * 1024

TOK_TILE = 256
UNIFIED_ROW_MULTIPLE = 512
INPROJ_ROW_TILES = 8
INPROJ_TN = 512
MIX_TT = 256
Q_TILE = 128
KEY_CHUNK = 512
PAGES_PER_STEP = 64
MOE_TM = 256
WEIGHT_DMA_BANDS = 4


def _dot(a, b):
    return jnp.dot(a, b, preferred_element_type=F32)


def _dot_nt(a, b):
    return lax.dot_general(a, b, (((1,), (1,)), ((), ())), preferred_element_type=F32)


def _layer_norm(x, g, b):
    mu = jnp.mean(x, axis=-1, keepdims=True)
    xc = x - mu
    var = jnp.mean(xc * xc, axis=-1, keepdims=True)
    return xc * lax.rsqrt(var + LN_EPS) * g + b


def _split3(x):
    hi = x.astype(BF16)
    r = x - hi.astype(F32)
    mid = r.astype(BF16)
    lo = (r - mid.astype(F32)).astype(BF16)
    return hi, mid, lo


TM_CHUNKS = D_MODEL // LANES


def _store_token_major(ref, x):
    rows = x.shape[0]
    for c in range(TM_CHUNKS):
        ref[pl.ds(c, rows, stride=TM_CHUNKS), :] = x[:, c * LANES:(c + 1) * LANES]


def _load_token_major(ref):
    rows = ref.shape[0] // TM_CHUNKS
    return jnp.concatenate([ref[pl.ds(c, rows, stride=TM_CHUNKS), :] for c in range(TM_CHUNKS)], axis=1)


def _pages_per_step(past_len):
    return min(PAGES_PER_STEP, past_len // PAGE_SIZE)


def _params(sem):
    return pltpu.CompilerParams(dimension_semantics=sem, vmem_limit_bytes=VMEM_LIMIT)


def _inproj_kernel(x_ref, w_ref, b_ref, o_ref, xb_ref):
    @pl.when(pl.program_id(1) == 0)
    def _():
        xb_ref[...] = x_ref[...].astype(BF16)

    o_ref[...] = _dot_nt(xb_ref[...], w_ref[...].astype(BF16)) + b_ref[...]


def _inproj(x, w_in_t, b_in, layer):
    nt, d = x.shape
    n_in = w_in_t.shape[1]
    tn = INPROJ_TN
    tm = nt // INPROJ_ROW_TILES
    assert nt % (INPROJ_ROW_TILES * SUBLANES) == 0
    return pl.pallas_call(
        _inproj_kernel,
        grid=(nt // tm, pl.cdiv(n_in, tn)),
        in_specs=[
            pl.BlockSpec((tm, d), lambda i, j: (i, 0)),
            pl.BlockSpec((None, tn, d), lambda i, j: (layer, j, 0)),
            pl.BlockSpec((1, tn), lambda i, j: (0, j)),
        ],
        out_specs=pl.BlockSpec((tm, tn), lambda i, j: (i, j)),
        out_shape=jax.ShapeDtypeStruct((nt, n_in), F32),
        scratch_shapes=[pltpu.VMEM((tm, d), BF16)],
        compiler_params=_params(("parallel", "arbitrary")),
        name="inproj",
    )(x, w_in_t, b_in[layer][None, :])


def _mix_abc_kernel(aval, agate, bu, bv, cb, cc, cx, caw, cab, lag, lab, lvg, lvb, sw, sb, ccw,
                    out_ref, sta_ref, stc_ref, abuf, cbuf, *, tt):
    t = pl.program_id(1)
    halo_a = 32
    halo_c = 8

    @pl.when(t == 0)
    def _():
        abuf[0:halo_a, :] = jnp.zeros((halo_a, D_GRP), F32)
        cbuf[0:halo_c, :] = jnp.zeros((halo_c, D_GRP), F32)

    abuf[halo_a:halo_a + tt, :] = aval[...] * jax.nn.sigmoid(agate[...])
    rc = 64
    off_a = halo_a - (CONV_A_WIDTH - 1)
    for r in range(tt // rc):
        acc = jnp.zeros((rc, D_GRP), F32)
        for k in range(CONV_A_WIDTH):
            acc = acc + caw[k:k + 1, :] * abuf[r * rc + off_a + k:r * rc + off_a + k + rc, :]
        y = _layer_norm(acc + cab[...], lag[...], lab[...])
        out_ref[r * rc:(r + 1) * rc, 0:D_GRP] = (y * jax.nn.sigmoid(y)).astype(BF16)
    sta_ref[...] = abuf[tt:tt + halo_a, :]
    abuf[0:halo_a, :] = abuf[tt:tt + halo_a, :]

    row = lax.broadcasted_iota(jnp.int32, (CHUNK, CHUNK), 0)
    col = lax.broadcasted_iota(jnp.int32, (CHUNK, CHUNK), 1)
    for c in range(tt // CHUNK):
        rows = slice(c * CHUNK, (c + 1) * CHUNK)
        v = _layer_norm(jax.nn.gelu(bv[rows, :]), lvg[...], lvb[...]).astype(BF16)
        gu = jax.nn.gelu(bu[rows, :])
        for h in range(GMLP_HEADS):
            lanes = slice(h * CHUNK, (h + 1) * CHUNK)
            wh = jnp.where(col <= row, sw[h], 0.0).astype(BF16)
            s = _dot(wh, v[:, lanes]) + sb[:, lanes]
            out_ref[rows, D_GRP + h * CHUNK:D_GRP + (h + 1) * CHUNK] = (gu[:, lanes] * s).astype(BF16)

    cbuf[halo_c:halo_c + tt, :] = cc[...] * cx[...]
    off_c = halo_c - (CONV_C_WIDTH - 1)
    conv = jnp.zeros((tt, D_GRP), F32)
    for k in range(CONV_C_WIDTH):
        conv = conv + ccw[k:k + 1, :] * cbuf[off_c + k:off_c + k + tt, :]
    out_ref[:, 2 * D_GRP:3 * D_GRP] = (cb[...] * conv).astype(BF16)
    stc_ref[...] = cbuf[tt:tt + halo_c, :]
    cbuf[0:halo_c, :] = cbuf[tt:tt + halo_c, :]


def _mix_abc_prompt(h, p, batch, seq):
    tt = MIX_TT
    nt = seq // tt
    col_spec = lambda c: pl.BlockSpec((tt, D_GRP), lambda b, t, c=c: (b * nt + t, c))
    full = lambda a: pl.BlockSpec(a.shape, lambda b, t: (0,) * a.ndim)
    small = [p["conv_a_w"], p["conv_a_b"], p["ln_a_g"], p["ln_a_b"], p["ln_v_g"], p["ln_v_b"],
             p["spatial_w"], p["spatial_b_rows"], p["conv_c_w"]]
    return pl.pallas_call(
        functools.partial(_mix_abc_kernel, tt=tt),
        grid=(batch, nt),
        in_specs=[col_spec(c) for c in range(7)] + [full(a) for a in small],
        out_specs=[
            pl.BlockSpec((tt, 3 * D_GRP), lambda b, t: (b * nt + t, 0)),
            pl.BlockSpec((None, 32, D_GRP), lambda b, t: (b, 0, 0)),
            pl.BlockSpec((None, 8, D_GRP), lambda b, t: (b, 0, 0)),
        ],
        out_shape=[
            jax.ShapeDtypeStruct((batch * seq, 3 * D_GRP), BF16),
            jax.ShapeDtypeStruct((batch, 32, D_GRP), F32),
            jax.ShapeDtypeStruct((batch, 8, D_GRP), F32),
        ],
        scratch_shapes=[pltpu.VMEM((32 + tt, D_GRP), F32), pltpu.VMEM((8 + tt, D_GRP), F32)],
        compiler_params=_params(("parallel", "arbitrary")),
        name="mix_abc_prompt",
    )(*([h] * 7), *small)


def _place_head(q_ref, hh, g, lane_half):
    tile = q_ref[:, (hh // 2) * LANES:(hh // 2 + 1) * LANES]
    if hh % 2 != g:
        tile = pltpu.roll(tile, HEAD_DIM, axis=1)
    return jnp.where(lane_half == g, tile * SCALE, 0.0).astype(BF16)


def _nsa_prompt_kernel(q_ref, g_ref, kvc_ref, kvs_ref, kvw_ref, posk_ref, posv_ref, pk_ref, pv_ref, pair_ref,
                       out_ref,
                       kcmp, vcmp, ks, vs, kw, vw, m_s, acc_s, sc_s, sa_s, sb_s, *, seq):
    i = pl.program_id(1)
    tq_n = Q_TILE
    n_cmp = seq // CMP_BLOCK
    n_sel = seq // SEL_BLOCK

    @pl.when(i == 0)
    def _():
        step = 256
        lane = lax.broadcasted_iota(jnp.int32, (step, LANES), 1)
        for r in range(seq // step):
            rows = slice(r * step, (r + 1) * step)
            blk = step // CMP_BLOCK
            kb = (kvc_ref[rows, 0:LANES].reshape(blk, CMP_BLOCK, LANES) * posk_ref[...][None]).sum(axis=1)
            vb = (kvc_ref[rows, LANES:2 * LANES].reshape(blk, CMP_BLOCK, LANES) * posv_ref[...][None]).sum(axis=1)
            kcmp[r * blk:(r + 1) * blk, :] = _dot(kb.astype(BF16), pk_ref[...].astype(BF16)).astype(BF16)
            vcmp[r * blk:(r + 1) * blk, :] = _dot(vb.astype(BF16), pv_ref[...].astype(BF16)).astype(BF16)
            pos = r * step + lax.broadcasted_iota(jnp.int32, (step, 1), 0)
            blk_idx = pos // SEL_BLOCK
            feat = jnp.where(lane == HEAD_DIM, blk_idx.astype(F32),
                             jnp.where(lane == HEAD_DIM + 1, (pos % SEL_BLOCK).astype(F32), 0.0))
            onehot = jnp.where(lane == blk_idx, 1.0, 0.0)
            k_sel = kvs_ref[rows, 0:LANES]
            v_sel = kvs_ref[rows, LANES:2 * LANES]
            for g in range(KV_HEADS):
                k_g = k_sel if g == 0 else pltpu.roll(k_sel, HEAD_DIM, axis=1)
                ks[g, rows, :] = jnp.concatenate([jnp.where(lane < HEAD_DIM, k_g, feat), onehot],
                                                 axis=1).astype(BF16)
                vs[g, rows, :] = jnp.where(lane // HEAD_DIM == g, v_sel, 1.0).astype(BF16)
            kw[WINDOW + r * step:WINDOW + (r + 1) * step, :] = kvw_ref[rows, 0:LANES].astype(BF16)
            vw[WINDOW + r * step:WINDOW + (r + 1) * step, :] = kvw_ref[rows, LANES:2 * LANES].astype(BF16)
        kw[0:WINDOW, :] = jnp.zeros((WINDOW, LANES), BF16)
        vw[0:WINDOW, :] = jnp.zeros((WINDOW, LANES), BF16)

    t0 = i * tq_n
    tq = t0 + lax.broadcasted_iota(jnp.int32, (tq_n, 1), 0)
    lane_half = lax.broadcasted_iota(jnp.int32, (tq_n, LANES), 1) // HEAD_DIM
    gate = jax.nn.sigmoid(g_ref[...])

    n_idx = lax.broadcasted_iota(jnp.int32, (1, n_cmp), 1)
    ok_c = (n_idx * CMP_BLOCK + (CMP_BLOCK - 1)) <= tq
    dist_c = tq.astype(F32) - (n_idx.astype(F32) * CMP_BLOCK + 0.5 * (CMP_BLOCK - 1))
    tq_l = t0 + lax.broadcasted_iota(jnp.int32, (1, tq_n), 1)
    cur = tq_l // SEL_BLOCK
    j_idx = lax.broadcasted_iota(jnp.int32, (n_sel, tq_n), 0)
    ok_s = j_idx <= cur
    forced = (j_idx == 0) | (j_idx >= cur - 1)
    n_win = WINDOW + tq_n
    off_w = lax.broadcasted_iota(jnp.int32, (1, n_win), 1)
    dist_w = lax.broadcasted_iota(jnp.int32, (tq_n, 1), 0) + WINDOW - off_w
    ok_w = (dist_w >= 0) & (dist_w <= WINDOW) & ((t0 - WINDOW + off_w) >= 0)
    nbase_w = jnp.where(ok_w, -dist_w.astype(F32), MASKED_SCORE)

    group_of = lambda hh: hh // HEADS_PER_KV
    rows_of = lambda hh: slice(hh * tq_n, (hh + 1) * tq_n)
    q_all = jnp.concatenate([_place_head(q_ref, hh, group_of(hh), lane_half) for hh in range(NSA_HEADS)],
                            axis=0)

    s_all = _dot_nt(q_all, kcmp[...])
    psum = [jnp.zeros((tq_n, n_cmp), F32) for _ in range(KV_HEADS)]
    p_list = []
    for hh in range(NSA_HEADS):
        s = s_all[rows_of(hh)] - SLOPES[hh] * dist_c
        s = jnp.where(ok_c, s, NEG_INF)
        e = jnp.exp(s - jnp.max(s, axis=1, keepdims=True))
        p = jnp.where(ok_c, e * (1.0 / jnp.sum(e, axis=1, keepdims=True)), 0.0)
        psum[group_of(hh)] = psum[group_of(hh)] + p
        p_list.append(p.astype(BF16))
    o_c = _dot(jnp.concatenate(p_list, axis=0), vcmp[...])

    n_top = float(min(N_SELECT, n_sel))
    pair = pair_ref[...]
    unsel = []
    for g in range(KV_HEADS):
        hi, mid, lo = _split3(psum[g])
        imp_t = _dot_nt(pair, hi) + _dot_nt(pair, mid) + _dot_nt(pair, lo)
        score = jnp.where(ok_s & forced, FORCE_SCORE, jnp.where(ok_s, imp_t, NEG_INF))
        sc_s[...] = score
        blocks = [score[b * SUBLANES:(b + 1) * SUBLANES] for b in range(n_sel // SUBLANES)]
        ranks = [jnp.zeros((SUBLANES, tq_n), F32) for _ in blocks]
        for k in range(n_sel):
            rk = sc_s[k:k + 1, :]
            for b, sb in enumerate(blocks):
                if b < k // SUBLANES:
                    beats = rk > sb
                elif b > k // SUBLANES:
                    beats = rk >= sb
                else:
                    beats = (rk > sb) | ((rk == sb) & (j_idx[b * SUBLANES:(b + 1) * SUBLANES] > k))
                ranks[b] = ranks[b] + jnp.where(beats, 1.0, 0.0)
        rank = jnp.concatenate(ranks, axis=0)
        unsel_t = jnp.where(rank < n_top, 0.0, MASKED_SCORE)
        unsel.append(jnp.concatenate([unsel_t, jnp.zeros((LANES - n_sel, tq_n), F32)], axis=0).T.astype(BF16))

    lane_q = lax.broadcasted_iota(jnp.int32, (tq_n, LANES), 1)

    def q_aug(hh):
        tile = q_ref[:, (hh // 2) * LANES:(hh // 2 + 1) * LANES]
        if hh % 2 == 1:
            tile = pltpu.roll(tile, HEAD_DIM, axis=1)
        feat = jnp.where(lane_q == HEAD_DIM, SEL_BLOCK * SLOPES[hh], jnp.where(lane_q == HEAD_DIM + 1, SLOPES[hh], 0.0))
        low = jnp.where(lane_q < HEAD_DIM, tile * SCALE, feat).astype(BF16)
        return jnp.concatenate([low, unsel[group_of(hh)]], axis=1)

    q_sel = [jnp.concatenate([q_aug(hh) for hh in range(g * HEADS_PER_KV, (g + 1) * HEADS_PER_KV)], axis=0)
             for g in range(KV_HEADS)]
    m_s[...] = jnp.full(m_s.shape, NEG_INF, F32)
    acc_s[...] = jnp.zeros(acc_s.shape, F32)

    def scores(c, buf):
        k0 = pl.multiple_of(c * KEY_CHUNK, KEY_CHUNK)
        for g in range(KV_HEADS):
            buf[g] = _dot_nt(q_sel[g], ks[g, pl.ds(k0, KEY_CHUNK), :])

    def softmax_pv(c, buf, on_diagonal):
        k0 = pl.multiple_of(c * KEY_CHUNK, KEY_CHUNK)
        if on_diagonal:
            causal = (k0 + lax.broadcasted_iota(jnp.int32, (1, KEY_CHUNK), 1)) <= tq
        masked = (lambda s: jnp.where(causal, s, MASKED_SCORE)) if on_diagonal else (lambda s: s)
        for g in range(KV_HEADS):
            alphas, p_list = [], []
            for h in range(HEADS_PER_KV):
                rows = rows_of(g * HEADS_PER_KV + h)
                local = slice(h * tq_n, (h + 1) * tq_n)
                m_old = m_s[rows, 0:1]
                m_new = jnp.maximum(m_old, jnp.max(masked(buf[g, local, :]), axis=1, keepdims=True))
                alphas.append(jnp.exp(m_old - m_new))
                p_list.append(jnp.exp(masked(buf[g, local, :]) - m_new).astype(BF16))
                m_s[rows, :] = jnp.broadcast_to(m_new, (tq_n, LANES))
            rows_g = slice(g * HEADS_PER_KV * tq_n, (g + 1) * HEADS_PER_KV * tq_n)
            acc_s[rows_g, :] = (jnp.concatenate(alphas, axis=0) * acc_s[rows_g, :]
                                + _dot(jnp.concatenate(p_list, axis=0), vs[g, pl.ds(k0, KEY_CHUNK), :]))

    n_full = t0 // KEY_CHUNK
    scores(0, sa_s)

    def chunk_pair(j, carry):
        scores(2 * j + 1, sb_s)
        softmax_pv(2 * j, sa_s, on_diagonal=False)
        scores(2 * j + 2, sa_s)
        softmax_pv(2 * j + 1, sb_s, on_diagonal=False)
        return carry

    lax.fori_loop(0, n_full // 2, chunk_pair, 0)

    @pl.when(n_full % 2 == 0)
    def _():
        softmax_pv(n_full, sa_s, on_diagonal=True)

    @pl.when(n_full % 2 == 1)
    def _():
        scores(n_full, sb_s)
        softmax_pv(n_full - 1, sa_s, on_diagonal=False)
        softmax_pv(n_full, sb_s, on_diagonal=True)
    o_s = jnp.concatenate(
        [acc_s[rows_of(hh), :] * (1.0 / acc_s[rows_of(hh), (1 - group_of(hh)) * HEAD_DIM:(1 - group_of(hh)) * HEAD_DIM + 1])
         for hh in range(NSA_HEADS)], axis=0)

    w0 = pl.multiple_of(t0, Q_TILE)
    s_all = _dot_nt(q_all, kw[pl.ds(w0, n_win), :])
    p_list = []
    for hh in range(NSA_HEADS):
        s = s_all[rows_of(hh)] + SLOPES[hh] * nbase_w
        e = jnp.exp(s - jnp.max(s, axis=1, keepdims=True))
        p_list.append((e * (1.0 / jnp.sum(e, axis=1, keepdims=True))).astype(BF16))
    o_w = _dot(jnp.concatenate(p_list, axis=0), vw[pl.ds(w0, n_win), :])

    for hh in range(NSA_HEADS):
        g = group_of(hh)
        rows = rows_of(hh)
        o = (gate[:, hh:hh + 1] * o_c[rows]
             + gate[:, NSA_HEADS + hh:NSA_HEADS + hh + 1] * o_s[rows]
             + gate[:, 2 * NSA_HEADS + hh:2 * NSA_HEADS + hh + 1] * o_w[rows])
        o = o[:, g * HEAD_DIM:(g + 1) * HEAD_DIM]
        out_ref[:, hh * HEAD_DIM:(hh + 1) * HEAD_DIM] = o.astype(BF16)


def _nsa_prompt(h, p, batch, seq):
    nq = seq // Q_TILE
    n_cmp = seq // CMP_BLOCK
    n_sel = seq // SEL_BLOCK
    full = lambda a: pl.BlockSpec(a.shape, lambda b, i: (0,) * a.ndim)
    pair = (jnp.arange(n_cmp)[None, :] // (SEL_BLOCK // CMP_BLOCK) == jnp.arange(n_sel)[:, None]).astype(BF16)
    small = [p["posk_rows"], p["posv_rows"], p["projk_bd"], p["projv_bd"], pair]
    kv_spec = lambda br: pl.BlockSpec((seq, 2 * LANES), lambda b, i, br=br: (b, COL_KV // (2 * LANES) + br))
    return pl.pallas_call(
        functools.partial(_nsa_prompt_kernel, seq=seq),
        grid=(batch, nq),
        in_specs=[
            pl.BlockSpec((Q_TILE, D_GRP), lambda b, i: (b * nq + i, COL_Q // D_GRP)),
            pl.BlockSpec((Q_TILE, LANES), lambda b, i: (b * nq + i, COL_G // LANES)),
            kv_spec(0), kv_spec(1), kv_spec(2),
        ] + [full(a) for a in small],
        out_specs=pl.BlockSpec((Q_TILE, D_GRP), lambda b, i: (b * nq + i, 0)),
        out_shape=jax.ShapeDtypeStruct((batch * seq, D_GRP), BF16),
        scratch_shapes=[
            pltpu.VMEM((n_cmp, LANES), BF16), pltpu.VMEM((n_cmp, LANES), BF16),
            pltpu.VMEM((KV_HEADS, seq, 2 * LANES), BF16), pltpu.VMEM((KV_HEADS, seq, LANES), BF16),
            pltpu.VMEM((seq + WINDOW, LANES), BF16), pltpu.VMEM((seq + WINDOW, LANES), BF16),
            pltpu.VMEM((NSA_HEADS * Q_TILE, LANES), F32),
            pltpu.VMEM((NSA_HEADS * Q_TILE, LANES), F32),
            pltpu.VMEM((n_sel, Q_TILE), F32),
            pltpu.VMEM((KV_HEADS, HEADS_PER_KV * Q_TILE, KEY_CHUNK), F32),
            pltpu.VMEM((KV_HEADS, HEADS_PER_KV * Q_TILE, KEY_CHUNK), F32),
        ],
        compiler_params=_params(("parallel", "arbitrary")),
        name="nsa_prompt",
    )(h, h, h, h, h, *small)


def _mix_abc_sample_kernel(hs, sta, stc, caw, cab, lag, lab, lvg, lvb, sw0, sb0, ccw,
                           out_ref, ain_ref, ccx_ref, v_ref):
    g = D_GRP
    a_in = hs[:, 0:g] * jax.nn.sigmoid(hs[:, g:2 * g])
    acc = caw[CONV_A_WIDTH - 1:CONV_A_WIDTH, :] * a_in
    for k in range(CONV_A_WIDTH - 1):
        acc = acc + caw[k:k + 1, :] * sta[k]
    y = _layer_norm(acc + cab[...], lag[...], lab[...])
    out_ref[:, 0:g] = (y * jax.nn.sigmoid(y)).astype(BF16)
    ain_ref[...] = a_in

    v = _layer_norm(jax.nn.gelu(hs[:, 3 * g:4 * g]), lvg[...], lvb[...])
    v_ref[...] = v
    s = sw0[...].astype(BF16).astype(F32) * v.astype(BF16).astype(F32) + sb0[...]
    out_ref[:, g:2 * g] = (jax.nn.gelu(hs[:, 2 * g:3 * g]) * s).astype(BF16)

    ccx = hs[:, 5 * g:6 * g] * hs[:, 6 * g:7 * g]
    ccx_ref[...] = ccx
    conv = ccw[CONV_C_WIDTH - 1:CONV_C_WIDTH, :] * ccx
    for k in range(CONV_C_WIDTH - 1):
        conv = conv + ccw[k:k + 1, :] * stc[k]
    out_ref[:, 2 * g:3 * g] = (hs[:, 4 * g:5 * g] * conv).astype(BF16)


def _mix_abc_sample(hs_abc, sta_t, stc_t, p):
    bs = hs_abc.shape[0]
    args = [hs_abc, sta_t, stc_t, p["conv_a_w"], p["conv_a_b"], p["ln_a_g"], p["ln_a_b"], p["ln_v_g"], p["ln_v_b"],
            p["spatial_w00"], p["spatial_b0"], p["conv_c_w"]]
    full = lambda a: pl.BlockSpec(a.shape, lambda i: (0,) * a.ndim)
    return pl.pallas_call(
        _mix_abc_sample_kernel,
        grid=(1,),
        in_specs=[full(a) for a in args],
        out_specs=[pl.BlockSpec((bs, 3 * D_GRP), lambda i: (0, 0))] + [pl.BlockSpec((bs, D_GRP), lambda i: (0, 0))] * 3,
        out_shape=[jax.ShapeDtypeStruct((bs, 3 * D_GRP), BF16)] + [jax.ShapeDtypeStruct((bs, D_GRP), F32)] * 3,
        compiler_params=_params(("arbitrary",)),
        name="mix_abc_sample",
    )(*args)


def _nsa_dec_cmp_kernel(pt_ref, qm_ref, slope_ref, posk_ref, posv_ref, pk_ref, pv_ref, pair_ref, *rest,
                        past_len, pages_per_step):
    pages = rest[:pages_per_step]
    oc_ref, sel_ref, kbar, vbar = rest[pages_per_step:]
    pc = pl.program_id(1)
    n_cmp = past_len // CMP_BLOCK
    n_sel = past_len // SEL_BLOCK + 1
    blk_pp = PAGE_SIZE // CMP_BLOCK
    rows_of = lambda page, kv: page[kv].reshape(LANES, PAGE_SIZE).T
    for kk in range(pages_per_step // 2):
        two_k = jnp.concatenate([rows_of(pages[2 * kk], 0), rows_of(pages[2 * kk + 1], 0)], axis=0)
        two_v = jnp.concatenate([rows_of(pages[2 * kk], 1), rows_of(pages[2 * kk + 1], 1)], axis=0)
        kb = (two_k.reshape(2 * blk_pp, CMP_BLOCK, LANES) * posk_ref[...][None]).sum(axis=1)
        vb = (two_v.reshape(2 * blk_pp, CMP_BLOCK, LANES) * posv_ref[...][None]).sum(axis=1)
        r0 = pl.multiple_of(pc * (pages_per_step * blk_pp) + kk * 2 * blk_pp, SUBLANES)
        kbar[pl.ds(r0, 2 * blk_pp), :] = kb
        vbar[pl.ds(r0, 2 * blk_pp), :] = vb

    @pl.when(pc == pl.num_programs(1) - 1)
    def _():
        kc = _dot(kbar[...].astype(BF16), pk_ref[...].astype(BF16)).astype(BF16)
        vc = _dot(vbar[...].astype(BF16), pv_ref[...].astype(BF16)).astype(BF16)
        q = (qm_ref[...] * SCALE).astype(BF16)
        slope = slope_ref[:, 0:1]
        n_idx = lax.broadcasted_iota(jnp.int32, (1, n_cmp), 1)
        center = n_idx.astype(F32) * CMP_BLOCK + 0.5 * (CMP_BLOCK - 1)
        ok_c = (n_idx * CMP_BLOCK + (CMP_BLOCK - 1)) <= past_len
        s = _dot_nt(q, kc) - slope * (float(past_len) - center)
        s = jnp.where(ok_c, s, NEG_INF)
        e = jnp.exp(s - jnp.max(s, axis=1, keepdims=True))
        p = jnp.where(ok_c, e / jnp.sum(e, axis=1, keepdims=True), 0.0)
        oc_ref[...] = _dot(p.astype(BF16), vc)
        psum = jnp.concatenate(
            [jnp.broadcast_to(jnp.sum(p[g * HEADS_PER_KV:(g + 1) * HEADS_PER_KV], axis=0, keepdims=True),
                              (HEADS_PER_KV, n_cmp)) for g in range(KV_HEADS)], axis=0)
        hi, mid, lo = _split3(psum)
        pair = pair_ref[...]
        imp = _dot(hi, pair) + _dot(mid, pair) + _dot(lo, pair)
        n_lanes = imp.shape[1]
        j_idx = lax.broadcasted_iota(jnp.int32, (1, n_lanes), 1)
        cur = past_len // SEL_BLOCK
        real = j_idx < n_sel
        ok_s = j_idx <= cur
        forced = (j_idx == 0) | (j_idx >= cur - 1)
        score = jnp.where(ok_s & forced, FORCE_SCORE, jnp.where(ok_s, imp, NEG_INF))
        score = jnp.where(real, score, NOT_A_BLOCK)
        rank = jnp.zeros(score.shape, F32)
        for k in range(n_sel):
            sk = score[:, k:k + 1]
            beats = (sk > score) | ((sk == score) & (j_idx > k))
            rank = rank + jnp.where(beats, 1.0, 0.0)
        sel_ref[...] = jnp.where((rank < float(min(N_SELECT, n_sel))) & real, 1.0, 0.0)


def _nsa_dec_cmp(page_table, qm, cache, p, layer, past_len):
    bs = qm.shape[0]
    n_pages = past_len // PAGE_SIZE
    pps = _pages_per_step(past_len)
    n_cmp = past_len // CMP_BLOCK
    n_sel = past_len // SEL_BLOCK + 1
    n_lanes = -(-n_sel // LANES) * LANES
    pair = (jnp.arange(n_cmp)[:, None] // (SEL_BLOCK // CMP_BLOCK) == jnp.arange(n_lanes)[None, :]).astype(BF16)
    small = [p["slope_rows"], p["posk_rows"], p["posv_rows"], p["projk_bd"], p["projv_bd"], pair]
    full = lambda a: pl.BlockSpec(a.shape, lambda b, pc, pt: (0,) * a.ndim)
    page_spec = lambda k: pl.BlockSpec((None, None, 2, KV_HEADS, HEAD_DIM, PAGE_SIZE),
                                       lambda b, pc, pt, k=k: (layer, pt[b, pc * pps + k], 0, 0, 0, 0))
    grid_spec = pltpu.PrefetchScalarGridSpec(
        num_scalar_prefetch=1,
        grid=(bs, n_pages // pps),
        in_specs=[pl.BlockSpec((None, NSA_HEADS, LANES), lambda b, pc, pt: (b, 0, 0))]
                 + [full(a) for a in small] + [page_spec(k) for k in range(pps)],
        out_specs=[pl.BlockSpec((None, NSA_HEADS, LANES), lambda b, pc, pt: (b, 0, 0)),
                   pl.BlockSpec((None, NSA_HEADS, n_lanes), lambda b, pc, pt: (b, 0, 0))],
        scratch_shapes=[pltpu.VMEM((n_cmp, LANES), F32), pltpu.VMEM((n_cmp, LANES), F32)],
    )
    return pl.pallas_call(
        functools.partial(_nsa_dec_cmp_kernel, past_len=past_len, pages_per_step=pps),
        grid_spec=grid_spec,
        out_shape=[jax.ShapeDtypeStruct((bs, NSA_HEADS, LANES), F32),
                   jax.ShapeDtypeStruct((bs, NSA_HEADS, n_lanes), F32)],
        compiler_params=_params(("parallel", "arbitrary")),
        name="nsa_dec_cmp",
    )(page_table, qm, *small, *([cache] * pps))


def _nsa_dec_sel_kernel(pt_ref, qm_ref, slope_ref, selr_ref, selnew_ref, knew_ref, vnew_ref, kwnew_ref, vwnew_ref,
                        win_ref, oc_ref, graw_ref, *rest, past_len, pages_per_step):
    pages = rest[:pages_per_step]
    out_ref, m_s, l_s, acc_s = rest[pages_per_step:]
    pc = pl.program_id(1)
    keys_per_step = pages_per_step * PAGE_SIZE
    q = (qm_ref[...] * SCALE).astype(BF16)
    qf = q.astype(F32)
    slope = slope_ref[:, 0:1]

    @pl.when(pc == 0)
    def _():
        m_s[...] = jnp.full(m_s.shape, NEG_INF, F32)
        l_s[...] = jnp.zeros(l_s.shape, F32)
        acc_s[...] = jnp.zeros(acc_s.shape, F32)

    lane = lax.broadcasted_iota(jnp.int32, (1, PAGE_SIZE), 1)
    selr = selr_ref[...]
    blocks_pp = PAGE_SIZE // SEL_BLOCK
    s_parts, ok_parts = [], []
    for k in range(pages_per_step):
        s = _dot(q, pages[k][0].reshape(LANES, PAGE_SIZE).astype(BF16))
        spos = pc * keys_per_step + k * PAGE_SIZE + lane
        dist = past_len - spos
        in_sel = selr[:, blocks_pp * k:blocks_pp * k + 1]
        for bb in range(1, blocks_pp):
            in_sel = jnp.where(lane // SEL_BLOCK == bb, selr[:, blocks_pp * k + bb:blocks_pp * k + bb + 1], in_sel)
        s_parts.append(s - slope * dist.astype(F32))
        ok_parts.append((in_sel > 0.5) & (dist >= 0))
    ok = jnp.concatenate(ok_parts, axis=1)
    s = jnp.where(ok, jnp.concatenate(s_parts, axis=1), NEG_INF)
    m_old = m_s[:, 0:1]
    m_new = jnp.maximum(m_old, jnp.max(s, axis=1, keepdims=True))
    alpha = jnp.exp(m_old - m_new)
    p = jnp.where(ok, jnp.exp(s - m_new), 0.0)
    l_new = alpha * l_s[:, 0:1] + jnp.sum(p, axis=1, keepdims=True)
    pb = p.astype(BF16)
    acc = alpha * acc_s[...]
    for k in range(pages_per_step):
        acc = acc + _dot_nt(pb[:, k * PAGE_SIZE:(k + 1) * PAGE_SIZE],
                            pages[k][1].reshape(LANES, PAGE_SIZE).astype(BF16))
    acc_s[...] = acc
    m_s[...] = jnp.broadcast_to(m_new, m_s.shape)
    l_s[...] = jnp.broadcast_to(l_new, l_s.shape)

    @pl.when(pc == pl.num_programs(1) - 1)
    def _():
        kn = knew_ref[...].astype(BF16).astype(F32)
        vn = vnew_ref[...].astype(BF16).astype(F32)
        s_n = jnp.sum(qf * kn, axis=1, keepdims=True)
        ok_n = selnew_ref[:, 0:1] > 0.5
        s_n = jnp.where(ok_n, s_n, NEG_INF)
        m_o = m_s[:, 0:1]
        m_f = jnp.maximum(m_o, s_n)
        a_f = jnp.exp(m_o - m_f)
        p_n = jnp.where(ok_n, jnp.exp(s_n - m_f), 0.0)
        l_f = a_f * l_s[:, 0:1] + p_n
        o_s = (a_f * acc_s[...] + p_n.astype(BF16).astype(F32) * vn) / l_f

        w_buf = win_ref.shape[-1]
        idx = lax.broadcasted_iota(jnp.int32, (1, w_buf), 1)
        dist_w = w_buf - idx
        ok_w = (dist_w <= WINDOW) & ((past_len - dist_w) >= 0)
        s_w = _dot(q, win_ref[0].reshape(LANES, w_buf).astype(BF16)) - slope * dist_w.astype(F32)
        s_w = jnp.where(ok_w, s_w, NEG_INF)
        kwn = kwnew_ref[...].astype(BF16).astype(F32)
        vwn = vwnew_ref[...].astype(BF16).astype(F32)
        s_wn = jnp.sum(qf * kwn, axis=1, keepdims=True)
        m_w = jnp.maximum(jnp.max(s_w, axis=1, keepdims=True), s_wn)
        e_w = jnp.exp(s_w - m_w)
        e_wn = jnp.exp(s_wn - m_w)
        den = jnp.sum(e_w, axis=1, keepdims=True) + e_wn
        p_w = (e_w / den).astype(BF16)
        p_wn = (e_wn / den).astype(BF16).astype(F32)
        o_w = _dot_nt(p_w, win_ref[1].reshape(LANES, w_buf).astype(BF16)) + p_wn * vwn

        gate = jax.nn.sigmoid(graw_ref[...])
        out_ref[...] = gate[:, 0:1] * oc_ref[...] + gate[:, 1:2] * o_s + gate[:, 2:3] * o_w


def _nsa_dec_sel(page_table, qm, selr, selnew, knew_s, vnew_s, knew_w, vnew_w, win, oc, graw, cache, p, layer,
                 past_len):
    bs = qm.shape[0]
    n_pages = past_len // PAGE_SIZE
    pps = _pages_per_step(past_len)
    w_buf = win.shape[-1]
    per_b =lambda a: pl.BlockSpec((None,) + a.shape[1:], lambda b, pc, pt: (b,) + (0,) * (a.ndim - 1))
    page_spec = lambda k: pl.BlockSpec((None, None, 2, KV_HEADS, HEAD_DIM, PAGE_SIZE),
                                       lambda b, pc, pt, k=k: (layer, pt[b, pc * pps + k], 0, 0, 0, 0))
    slope = p["slope_rows"]
    grid_spec = pltpu.PrefetchScalarGridSpec(
        num_scalar_prefetch=1,
        grid=(bs, n_pages // pps),
        in_specs=[
            per_b(qm),
            pl.BlockSpec(slope.shape, lambda b, pc, pt: (0, 0)),
            pl.BlockSpec((None, None, NSA_HEADS, LANES), lambda b, pc, pt: (b, pc, 0, 0)),
            per_b(selnew), per_b(knew_s), per_b(vnew_s), per_b(knew_w), per_b(vnew_w),
            pl.BlockSpec((None, None, 2, KV_HEADS, HEAD_DIM, w_buf), lambda b, pc, pt: (layer, b, 0, 0, 0, 0)),
            per_b(oc), per_b(graw),
        ] + [page_spec(k) for k in range(pps)],
        out_specs=pl.BlockSpec((None, NSA_HEADS, LANES), lambda b, pc, pt: (b, 0, 0)),
        scratch_shapes=[pltpu.VMEM((NSA_HEADS, LANES), F32)] * 3,
    )
    return pl.pallas_call(
        functools.partial(_nsa_dec_sel_kernel, past_len=past_len, pages_per_step=pps),
        grid_spec=grid_spec,
        out_shape=jax.ShapeDtypeStruct((bs, NSA_HEADS, LANES), F32),
        compiler_params=_params(("parallel", "arbitrary")),
        name="nsa_dec_sel",
    )(page_table, qm, slope, selr, selnew, knew_s, vnew_s, knew_w, vnew_w, win, oc, graw, *([cache] * pps))


def _outproj_kernel(mabc_p, md_p, mabc_s, md_s, x, wo, g1, b1, wr, br, x1_ref, ids_ref, wts_ref, *,
                    alpha, n_prompt_tiles):
    i = pl.program_id(0)

    @pl.when(i < n_prompt_tiles)
    def _():
        _outproj_tile(mabc_p, md_p, x, wo, g1, b1, wr, br, x1_ref, ids_ref, wts_ref, alpha)

    @pl.when(i >= n_prompt_tiles)
    def _():
        _outproj_tile(mabc_s, md_s, x, wo, g1, b1, wr, br, x1_ref, ids_ref, wts_ref, alpha)


def _outproj_tile(mabc, md, x, wo, g1, b1, wr, br, x1_ref, ids_ref, wts_ref, alpha):
    k_abc = mabc.shape[1]
    mix = _dot(mabc[...], wo[0:k_abc, :]) + _dot(md[...], wo[k_abc:, :])
    x1 = _layer_norm(alpha * x[...] + mix, g1[...], b1[...])
    _store_token_major(x1_ref, x1)
    x_hi, x_lo, _ = _split3(x1)
    w_hi, w_lo, _ = _split3(wr[...])
    logits = _dot(x_hi, w_hi) + (_dot(x_lo, w_hi) + _dot(x_hi, w_lo)) + br[...]
    lane = lax.broadcasted_iota(jnp.int32, logits.shape, 1)
    is_g = lane < N_GROUPS
    gl = jnp.where(is_g, logits, -jnp.inf)
    gmax = jnp.max(gl, axis=1, keepdims=True)
    gsel = jnp.min(jnp.where(gl == gmax, lane, LANES), axis=1, keepdims=True)
    ggate = 1.0 / jnp.sum(jnp.where(is_g, jnp.exp(gl - gmax), 0.0), axis=1, keepdims=True)
    lo = N_GROUPS + gsel * EXPERTS_PER_GROUP
    el = jnp.where((lane >= lo) & (lane < lo + EXPERTS_PER_GROUP), logits, -jnp.inf)
    v1 = jnp.max(el, axis=1, keepdims=True)
    i1 = jnp.min(jnp.where(el == v1, lane, LANES), axis=1, keepdims=True)
    el2 = jnp.where(lane == i1, -jnp.inf, el)
    v2 = jnp.max(el2, axis=1, keepdims=True)
    i2 = jnp.min(jnp.where(el2 == v2, lane, LANES), axis=1, keepdims=True)
    e21 = jnp.exp(v2 - v1)
    w1 = ggate / (1.0 + e21)
    w2 = ggate * e21 / (1.0 + e21)
    ids_ref[...] = jnp.where(lane == 0, i1 - N_GROUPS, jnp.where(lane == 1, i2 - N_GROUPS, 0))
    wts_ref[...] = jnp.where(lane == 0, w1, jnp.where(lane == 1, w2, 0.0))


def _outproj(mabc_p, md_p, mabc_s, md_s, x, wo_bf, g1, b1, wr_bf, br, alpha):
    nt = x.shape[0]
    tm = TOK_TILE
    n_p = mabc_p.shape[0] // tm
    assert mabc_p.shape[0] % tm == 0 and mabc_s.shape[0] == nt - mabc_p.shape[0]
    row = lambda w: pl.BlockSpec((tm, w), lambda i: (i, 0))
    prow = lambda w: pl.BlockSpec((tm, w), lambda i: (jnp.minimum(i, n_p - 1), 0))
    srow = lambda w: pl.BlockSpec((tm, w), lambda i: (jnp.maximum(i - n_p, 0), 0))
    full = lambda a: pl.BlockSpec(a.shape, lambda i: (0,) * a.ndim)
    return pl.pallas_call(
        functools.partial(_outproj_kernel, alpha=alpha, n_prompt_tiles=n_p),
        grid=(nt // tm,),
        in_specs=[prow(mabc_p.shape[1]), prow(md_p.shape[1]), srow(mabc_s.shape[1]), srow(md_s.shape[1]),
                  row(D_MODEL), full(wo_bf), full(g1), full(b1), full(wr_bf), full(br)],
        out_specs=[pl.BlockSpec((tm * TM_CHUNKS, LANES), lambda i: (i, 0)), row(LANES), row(LANES)],
        out_shape=[jax.ShapeDtypeStruct((nt * TM_CHUNKS, LANES), F32),
                   jax.ShapeDtypeStruct((nt, LANES), jnp.int32), jax.ShapeDtypeStruct((nt, LANES), F32)],
        compiler_params=_params(("parallel",)),
        name="outproj_ln1_route",
    )(mabc_p, md_p, mabc_s, md_s, x, wo_bf, g1, b1, wr_bf, br)


def _rows_wait(hbm, buf, sem):
    pltpu.make_async_copy(hbm.at[pl.ds(0, buf.shape[0])], buf, sem).wait()


def _moe_kernel(te_ref, tv_ref, first_ref, nxt_ref, st_ref, dst_ref,
                x_hbm, w_ref, wg_hbm, wu_hbm, wd_hbm, y_hbm,
                xb0, xb1, ob0, ob1, wgf, wuf, wdf, wgb, wub, wdb, gsem, ssem, wsem, *, layer):
    i = pl.program_id(0)
    tm = xb0.shape[0] // TM_CHUNKS
    xbufs, obufs = (xb0, xb1), (ob0, ob1)

    def token_rows(n):
        start = n * TM_CHUNKS
        return pl.ds(start if isinstance(n, int) else pl.multiple_of(start, TM_CHUNKS), TM_CHUNKS)

    valid = tv_ref[i] > 0
    prev_valid = tv_ref[jnp.maximum(i - 1, 0)] > 0

    def weights_copy(e, start):
        for src, dst, k in ((wg_hbm, wgf, 0), (wu_hbm, wuf, 1), (wd_hbm, wdf, 2)):
            band = dst.shape[0] // WEIGHT_DMA_BANDS
            for b in range(WEIGHT_DMA_BANDS):
                rows = pl.ds(b * band, band)
                cp = pltpu.make_async_copy(src.at[layer, e, rows, :], dst.at[rows, :], wsem.at[k])
                cp.start() if start else cp.wait()

    def start_gather(tile, k):
        for r in range(tm):
            pltpu.make_async_copy(x_hbm.at[token_rows(st_ref[tile * tm + r]), :], xbufs[k].at[token_rows(r), :],
                                  gsem.at[k]).start()

    def start_scatter(tile, k):
        for r in range(tm):
            pltpu.make_async_copy(obufs[k].at[token_rows(r), :], y_hbm.at[token_rows(dst_ref[(tile + 2) * tm + r]), :],
                                  ssem.at[k]).start()

    @pl.when(i == 0)
    def _():
        weights_copy(te_ref[0], start=True)
        start_gather(0, 0)
        ob0[...] = jnp.zeros(ob0.shape, F32)
        ob1[...] = jnp.zeros(ob1.shape, F32)
        start_scatter(-2, 0)

    @pl.when(valid & (first_ref[i] > 0))
    def _():
        weights_copy(te_ref[i], start=False)
        wgb[...] = wgf[...].astype(BF16)
        wub[...] = wuf[...].astype(BF16)
        wdb[...] = wdf[...].astype(BF16)

        @pl.when(nxt_ref[i] >= 0)
        def _():
            weights_copy(nxt_ref[i], start=True)

    def tile_step(k):
        _rows_wait(x_hbm, xbufs[k], gsem.at[k])
        _rows_wait(y_hbm, obufs[k], ssem.at[k])
        start_gather(i + 1, 1 - k)
        start_scatter(i - 1, 1 - k)
        x = _load_token_major(xbufs[k]).astype(BF16)
        hg = _dot(x, wgb[...])
        hu = _dot(x, wub[...])
        hidden = (hg * jax.nn.sigmoid(hg)) * hu * w_ref[...]
        _store_token_major(obufs[k], _dot(hidden.astype(BF16), wdb[...]))

    def drain(k):
        _rows_wait(x_hbm, xbufs[k], gsem.at[k])
        _rows_wait(y_hbm, obufs[k], ssem.at[k])
        start_scatter(i - 1, 1 - k)
        _rows_wait(y_hbm, obufs[1 - k], ssem.at[1 - k])

    for k in range(2):
        @pl.when(valid & (i % 2 == k))
        def _(k=k):
            tile_step(k)

        @pl.when(jnp.logical_not(valid) & prev_valid & (i % 2 == k))
        def _(k=k):
            drain(k)


def _moe(route, x1, w_gate_e, w_up_e, w_down_e, layer):
    nt = x1.shape[0] // TM_CHUNKS
    n_slots = route["slot_token"].shape[0]
    tm = MOE_TM
    any_spec = pl.BlockSpec(memory_space=pl.ANY)
    grid_spec = pltpu.PrefetchScalarGridSpec(
        num_scalar_prefetch=6,
        grid=(n_slots // tm,),
        in_specs=[any_spec, pl.BlockSpec((tm, 1), lambda i, *_: (i, 0)), any_spec, any_spec, any_spec],
        out_specs=any_spec,
        scratch_shapes=[
            pltpu.VMEM((tm * TM_CHUNKS, LANES), F32), pltpu.VMEM((tm * TM_CHUNKS, LANES), F32),
            pltpu.VMEM((tm * TM_CHUNKS, LANES), F32), pltpu.VMEM((tm * TM_CHUNKS, LANES), F32),
            pltpu.VMEM((D_MODEL, D_EXPERT), F32), pltpu.VMEM((D_MODEL, D_EXPERT), F32),
            pltpu.VMEM((D_EXPERT, D_MODEL), F32),
            pltpu.VMEM((D_MODEL, D_EXPERT), BF16), pltpu.VMEM((D_MODEL, D_EXPERT), BF16),
            pltpu.VMEM((D_EXPERT, D_MODEL), BF16),
            pltpu.SemaphoreType.DMA((2,)), pltpu.SemaphoreType.DMA((2,)), pltpu.SemaphoreType.DMA((3,)),
        ],
    )
    lead_dest = 2 * nt + jnp.arange(2 * tm, dtype=jnp.int32)
    return pl.pallas_call(
        functools.partial(_moe_kernel, layer=layer),
        grid_spec=grid_spec,
        out_shape=jax.ShapeDtypeStruct(((2 * nt + 2 * tm) * TM_CHUNKS, LANES), F32),
        compiler_params=_params(("arbitrary",)),
        name="moe_experts",
    )(route["tile_expert"], route["tile_valid"], route["tile_first"], route["tile_next_expert"],
      route["slot_token"], jnp.concatenate([lead_dest, route["slot_dest"]]), x1, route["slot_w"][:, None],
      w_gate_e, w_up_e, w_down_e)


def _ln2_kernel(x1, y0, y1, g2, b2, o_ref, *, alpha):
    y = _load_token_major(y0) + _load_token_major(y1)
    o_ref[...] = _layer_norm(alpha * _load_token_major(x1) + y, g2[...], b2[...])


def _ln2_split_kernel(x1, y0, y1, g2, b2, prompt_ref, rest_ref, *, alpha, n_prompt_tiles):
    y = _load_token_major(y0) + _load_token_major(y1)
    x2 = _layer_norm(alpha * _load_token_major(x1) + y, g2[...], b2[...])
    i = pl.program_id(0)

    @pl.when(i < n_prompt_tiles)
    def _():
        prompt_ref[...] = x2

    @pl.when(i >= n_prompt_tiles)
    def _():
        rest_ref[...] = x2


def _ln2(x1, y_pairs, g2, b2, alpha, split_at=None):
    nt = x1.shape[0] // TM_CHUNKS
    tm = TOK_TILE
    n_tiles = nt // tm
    first = pl.BlockSpec((tm * TM_CHUNKS, LANES), lambda i: (i, 0))
    second = pl.BlockSpec((tm * TM_CHUNKS, LANES), lambda i: (n_tiles + i, 0))
    row = pl.BlockSpec((tm, D_MODEL), lambda i: (i, 0))
    full = lambda a: pl.BlockSpec(a.shape, lambda i: (0,) * a.ndim)
    if split_at is not None:
        n_p = split_at // tm
        assert split_at % tm == 0 and 0 < n_p < n_tiles
        return pl.pallas_call(
            functools.partial(_ln2_split_kernel, alpha=alpha, n_prompt_tiles=n_p),
            grid=(n_tiles,),
            in_specs=[first, first, second, full(g2), full(b2)],
            out_specs=[pl.BlockSpec((tm, D_MODEL), lambda i: (jnp.minimum(i, n_p - 1), 0)),
                       pl.BlockSpec((tm, D_MODEL), lambda i: (jnp.maximum(i - n_p, 0), 0))],
            out_shape=[jax.ShapeDtypeStruct((split_at, D_MODEL), F32),
                       jax.ShapeDtypeStruct((nt - split_at, D_MODEL), F32)],
            compiler_params=_params(("arbitrary",)),
            name="combine_ln2_final",
        )(x1, y_pairs, y_pairs, g2, b2)
    return pl.pallas_call(
        functools.partial(_ln2_kernel, alpha=alpha),
        grid=(n_tiles,),
        in_specs=[first, first, second, full(g2), full(b2)],
        out_specs=row,
        out_shape=jax.ShapeDtypeStruct((nt, D_MODEL), F32),
        compiler_params=_params(("parallel",)),
        name="combine_ln2",
    )(x1, y_pairs, y_pairs, g2, b2)


def _route_slots(ids, wts, n_slots):
    tm = MOE_TM
    nt = ids.shape[0]
    e_flat = ids[:, :2].reshape(-1)
    w_flat = wts[:, :2].reshape(-1)
    n_pairs = e_flat.shape[0]
    take = lambda table, idx: table.at[idx].get(mode="promise_in_bounds")
    experts = jnp.arange(N_EXPERTS, dtype=jnp.int32)
    order = jnp.argsort(e_flat, stable=True).astype(jnp.int32)
    counts = jnp.sum(e_flat[:, None] == experts[None, :], axis=0).astype(jnp.int32)
    padded = (counts + tm - 1) // tm * tm
    ends = jnp.cumsum(padded)
    offs = ends - padded
    starts = jnp.cumsum(counts) - counts
    tile_start = jnp.arange(n_slots // tm, dtype=jnp.int32) * tm
    total = ends[-1]
    tile_valid = (tile_start < total).astype(jnp.int32)
    last_start = jnp.maximum(total - tm, 0)
    tile_expert = jnp.sum(jnp.minimum(tile_start, last_start)[:, None] >= ends[None, :], axis=1).astype(jnp.int32)
    tile_expert = jnp.minimum(tile_expert, N_EXPERTS - 1)
    in_run = tile_start[:, None] + jnp.arange(tm, dtype=jnp.int32)[None, :] - take(offs, tile_expert)[:, None]
    filled = (in_run < take(counts, tile_expert)[:, None]) & (tile_valid[:, None] > 0)
    src = jnp.clip(take(starts, tile_expert)[:, None] + in_run, 0, n_pairs - 1).reshape(-1)
    pair = take(order, src)
    filled = filled.reshape(-1)
    slot_token = jnp.where(filled, pair // 2, 0)
    slot_w = jnp.where(filled, take(w_flat, pair), 0.0)
    slot_idx = jnp.arange(n_slots, dtype=jnp.int32)
    slot_dest = jnp.where(filled, (pair % 2) * nt + pair // 2, 2 * nt + slot_idx % (2 * tm))
    prev_expert = jnp.concatenate([jnp.full((1,), -1, jnp.int32), tile_expert[:-1]])
    tile_first = (tile_expert != prev_expert).astype(jnp.int32)
    later = (experts[None, :] > experts[:, None]) & (counts[None, :] > 0)
    next_expert = jnp.min(jnp.where(later, experts[None, :], N_EXPERTS), axis=1)
    next_expert = jnp.where(next_expert == N_EXPERTS, -1, next_expert).astype(jnp.int32)
    return dict(slot_token=slot_token, slot_w=slot_w, slot_dest=slot_dest, tile_expert=tile_expert,
                tile_valid=tile_valid, tile_first=tile_first, tile_next_expert=take(next_expert, tile_expert))


def _transpose_kernel(x_ref, o_ref):
    o_ref[...] = x_ref[...].T


def _kv_rows_minor(h, batch, seq):
    tt = KEY_CHUNK
    nt = seq // tt
    return pl.pallas_call(
        _transpose_kernel,
        grid=(batch, 3, nt),
        in_specs=[pl.BlockSpec((tt, 2 * LANES), lambda b, br, t: (b * nt + t, COL_KV // (2 * LANES) + br))],
        out_specs=pl.BlockSpec((None, None, 2 * LANES, tt), lambda b, br, t: (b, br, 0, t)),
        out_shape=jax.ShapeDtypeStruct((batch, 3, 2 * LANES, seq), F32),
        compiler_params=_params(("parallel", "parallel", "parallel")),
        name="kv_rows_minor",
    )(h)


def _layer(x, layer, n_prompt_rows, batch, seq, bs, past_len, page_table, caches, states, weights):
    (cache_cmp, cache_sel, state_win, state_conv_a, state_conv_c) = (caches[0], caches[1], states[0], states[1],
                                                                      states[2])
    w = weights
    nt = x.shape[0]
    n_tok = n_prompt_rows + bs
    depth = w["w_in"].shape[0]
    alpha = (2.0 * depth) ** 0.25
    rep_g = lambda a: jnp.repeat(a, HEAD_DIM, axis=1)
    bd = lambda a: jnp.zeros((LANES, LANES), F32).at[:HEAD_DIM, :HEAD_DIM].set(a[0]).at[HEAD_DIM:, HEAD_DIM:].set(a[1])
    row = lambda a: a[layer][None, :]
    p = {
        "conv_a_w": w["conv_a_w"][layer], "conv_a_b": row(w["conv_a_b"]),
        "ln_a_g": row(w["ln_a_g"]), "ln_a_b": row(w["ln_a_b"]), "ln_v_g": row(w["ln_v_g"]), "ln_v_b": row(w["ln_v_b"]),
        "spatial_w": w["spatial_w"][layer],
        "spatial_b_rows": jnp.repeat(w["spatial_b"][layer].T, CHUNK, axis=1),
        "spatial_w00": jnp.repeat(w["spatial_w"][layer][:, 0, 0], CHUNK)[None, :],
        "spatial_b0": jnp.repeat(w["spatial_b"][layer][:, 0], CHUNK)[None, :],
        "conv_c_w": w["conv_c_w"][layer],
        "posk_rows": rep_g(w["cmp_pos_k"][layer]), "posv_rows": rep_g(w["cmp_pos_v"][layer]),
        "projk_bd": bd(w["cmp_proj_k"][layer]), "projv_bd": bd(w["cmp_proj_v"][layer]),
        "slope_rows": jnp.broadcast_to(jnp.asarray(SLOPES, F32)[:, None], (NSA_HEADS, LANES)),
    }

    h = _inproj(x, w["w_in"], w["b_in"], layer)

    mabc_p, sta_p, stc_p = _mix_abc_prompt(h, p, batch, seq)
    md_p = _nsa_prompt(h, p, batch, seq)

    hs = h[n_prompt_rows:n_tok]
    sta_t = jnp.transpose(state_conv_a[layer], (1, 0, 2))
    stc_t = jnp.transpose(state_conv_c[layer], (1, 0, 2))
    mabc_s, a_in_s, ccx_s, v_s = _mix_abc_sample(hs[:, :7 * D_GRP], sta_t, stc_t, p)
    q_s = hs[:, COL_Q:COL_KV].reshape(bs, NSA_HEADS, HEAD_DIM)
    grp = jnp.arange(NSA_HEADS) // HEADS_PER_KV
    qm = jnp.where((jnp.arange(LANES)[None, :] // HEAD_DIM == grp[:, None])[None],
                   jnp.tile(q_s, (1, 1, KV_HEADS)), 0.0)
    kv_s = hs[:, COL_KV:COL_G].reshape(bs, 3, 2, LANES)
    graw = jnp.pad(jnp.transpose(hs[:, COL_G:N_IN].reshape(bs, 3, NSA_HEADS), (0, 2, 1)),
                   ((0, 0), (0, 0), (0, LANES - 3)))
    rows_minor = lambda a: jnp.transpose(a, (0, 1, 3, 4, 5, 2))
    cache_cmp_r, cache_sel_r, win_r = rows_minor(cache_cmp), rows_minor(cache_sel), rows_minor(state_win)
    oc, sel = _nsa_dec_cmp(page_table, qm, cache_cmp_r, p, layer, past_len)
    n_steps = past_len // PAGE_SIZE // _pages_per_step(past_len)
    n_past_blocks = past_len // SEL_BLOCK
    blocks_per_step = n_past_blocks // n_steps
    selr = jnp.transpose(sel[:, :, :n_past_blocks].reshape(bs, NSA_HEADS, n_steps, blocks_per_step), (0, 2, 1, 3))
    selr = jnp.pad(selr, ((0, 0), (0, 0), (0, 0), (0, LANES - blocks_per_step)))
    selnew = jnp.broadcast_to(sel[:, :, n_past_blocks:n_past_blocks + 1], (bs, NSA_HEADS, LANES))
    d_s = _nsa_dec_sel(page_table, qm, selr, selnew, kv_s[:, 1, 0][:, None], kv_s[:, 1, 1][:, None],
                       kv_s[:, 2, 0][:, None], kv_s[:, 2, 1][:, None], win_r, oc, graw, cache_sel_r, p, layer,
                       past_len)
    d_s = d_s.reshape(bs, NSA_HEADS, KV_HEADS, HEAD_DIM)
    md_s = jnp.concatenate([d_s[:, :HEADS_PER_KV, 0], d_s[:, HEADS_PER_KV:, 1]], axis=1).reshape(bs, D_GRP)

    pad_rows = nt - n_tok
    s_rows = nt - n_prompt_rows
    mabc_s = jnp.pad(mabc_s, ((0, s_rows - bs), (0, 0)))
    md_s = jnp.pad(md_s.astype(BF16), ((0, s_rows - bs), (0, 0)))
    wr = jnp.concatenate([w["w_group"][layer], w["w_router"][layer]], axis=1)
    wr = jnp.pad(wr, ((0, 0), (0, LANES - wr.shape[1])))
    br = jnp.pad(jnp.concatenate([w["b_group"][layer], w["b_router"][layer]]), (0, LANES - N_GROUPS - N_EXPERTS))[None]
    x1, ids, wts = _outproj(mabc_p, md_p, mabc_s, md_s, x, w["w_out"][layer].astype(BF16), row(w["ln1_g"]),
                            row(w["ln1_b"]), wr, br, alpha)

    n_slots = (-(-(2 * nt) // MOE_TM) + N_EXPERTS) * MOE_TM
    route = _route_slots(ids, wts, n_slots)
    y_pairs = _moe(route, x1, w["w_gate_e"], w["w_up_e"], w["w_down_e"], layer)
    x2 = _ln2(x1, y_pairs, row(w["ln2_g"]), row(w["ln2_b"]), alpha,
              split_at=n_prompt_rows if layer == depth - 1 else None)

    kv_t = _kv_rows_minor(h, batch, seq).reshape(batch, 3, 2, KV_HEADS, HEAD_DIM, seq)
    kv_p = jnp.transpose(kv_t, (0, 5, 1, 2, 3, 4))
    w_buf = state_win.shape[2]
    kv_s6 = hs[:, COL_KV:COL_G].reshape(bs, 1, 3, 2, KV_HEADS, HEAD_DIM)
    st_p = (kv_p[:, :, 0], kv_p[:, :, 1], kv_p[:, seq - w_buf:, 2],
            sta_p[:, 32 - (CONV_A_WIDTH - 1):], stc_p[:, 8 - (CONV_C_WIDTH - 1):])
    st_s = (kv_s6[:, :, 0], kv_s6[:, :, 1],
            jnp.concatenate([state_win[layer][:, 1:], kv_s6[:, :, 2]], axis=1),
            jnp.concatenate([state_conv_a[layer][:, 1:], a_in_s[:, None]], axis=1),
            jnp.concatenate([state_conv_c[layer][:, 1:], ccx_s[:, None]], axis=1),
            v_s[:, None])
    return x2, st_p, st_s


def kernel(x_prompt, x_sample, cache_cmp_kv, cache_sel_kv, state_win_kv, state_conv_a, state_conv_c, page_table, w_in, b_in, conv_a_w, conv_a_b, ln_a_g, ln_a_b, ln_v_g, ln_v_b, spatial_w, spatial_b, conv_c_w, cmp_pos_k, cmp_pos_v, cmp_proj_k, cmp_proj_v, w_out, ln1_g, ln1_b, ln2_g, ln2_b, w_group, b_group, w_router, b_router, w_gate_e, w_up_e, w_down_e):
    batch, seq, d_model = x_prompt.shape
    bs, dec_seq, _ = x_sample.shape
    depth = w_in.shape[0]
    past_len = page_table.shape[1] * PAGE_SIZE
    assert d_model == D_MODEL and dec_seq == 1 and w_in.shape[2] == N_IN
    assert seq % KEY_CHUNK == 0 and seq % MIX_TT == 0
    assert past_len % (PAGE_SIZE * _pages_per_step(past_len)) == 0
    assert state_win_kv.shape[2] == WINDOW and past_len >= WINDOW
    n_prompt_rows = batch * seq
    n_tok = n_prompt_rows + bs
    nt = -(-n_tok // UNIFIED_ROW_MULTIPLE) * UNIFIED_ROW_MULTIPLE
    x = jnp.concatenate([x_prompt.reshape(n_prompt_rows, d_model), x_sample.reshape(bs, d_model),
                         jnp.zeros((nt - n_tok, d_model), F32)], axis=0)
    weights = dict(w_in=jnp.transpose(w_in, (0, 2, 1)), b_in=b_in, conv_a_w=conv_a_w, conv_a_b=conv_a_b, ln_a_g=ln_a_g, ln_a_b=ln_a_b,
                   ln_v_g=ln_v_g, ln_v_b=ln_v_b, spatial_w=spatial_w, spatial_b=spatial_b, conv_c_w=conv_c_w,
                   cmp_pos_k=cmp_pos_k, cmp_pos_v=cmp_pos_v, cmp_proj_k=cmp_proj_k, cmp_proj_v=cmp_proj_v,
                   w_out=w_out, ln1_g=ln1_g, ln1_b=ln1_b, ln2_g=ln2_g, ln2_b=ln2_b, w_group=w_group,
                   b_group=b_group, w_router=w_router, b_router=b_router, w_gate_e=w_gate_e, w_up_e=w_up_e,
                   w_down_e=w_down_e)
    st_p, st_s = [], []
    for layer in range(depth):
        x, sp, ss = _layer(x, layer, n_prompt_rows, batch, seq, bs, past_len, page_table,
                           (cache_cmp_kv, cache_sel_kv), (state_win_kv, state_conv_a, state_conv_c), weights)
        st_p.append(sp)
        st_s.append(ss)
    x_prompt_rows, x_rest_rows = x
    y_prompt = x_prompt_rows.reshape(batch, seq, d_model)
    y_sample = x_rest_rows[:bs].reshape(bs, 1, d_model)
    return (y_prompt, y_sample,
            jnp.stack([s[0] for s in st_p]), jnp.stack([s[1] for s in st_p]), jnp.stack([s[2] for s in st_p]),
            jnp.stack([s[3] for s in st_p]), jnp.stack([s[4] for s in st_p]),
            jnp.stack([s[0] for s in st_s]), jnp.stack([s[1] for s in st_s]), jnp.stack([s[2] for s in st_s]),
            jnp.stack([s[3] for s in st_s]), jnp.stack([s[4] for s in st_s]), jnp.stack([s[5] for s in st_s]))
```

```python
import functools

import jax
import jax.numpy as jnp
from jax import lax
from jax.experimental import pallas as pl
from jax.experimental.pallas import tpu as pltpu

F32 = jnp.float32
BF16 = jnp.bfloat16

D_MODEL = 2048
D_GRP = 512
NSA_HEADS = 8
HEAD_DIM = 64
KV_HEADS = 2
HEADS_PER_KV = 4
CONV_A_WIDTH = 31
CONV_C_WIDTH = 3
CHUNK = 128
GMLP_HEADS = 4
CMP_BLOCK = 32
SEL_BLOCK = 64
N_SELECT = 16
WINDOW = 512
PAGE_SIZE = 128
N_GROUPS = 4
EXPERTS_PER_GROUP = 8
N_EXPERTS = 32
D_EXPERT = 512
LN_EPS = 1e-5
NEG_INF = -1e30
FORCE_SCORE = 1e9
NOT_A_BLOCK = -3e38
MASKED_SCORE = -1e35
N_IN = 4888
COL_Q = 3584
COL_KV = 4096
COL_G = 4864
SCALE = HEAD_DIM ** -0.5
SLOPES = tuple(2.0 ** (-(h + 1)) for h in range(NSA_HEADS))

LANES = 128
SUBLANES = 8
VMEM_LIMIT = 56 * 1024 * 1024

TOK_TILE = 256
UNIFIED_ROW_MULTIPLE = 512
INPROJ_ROW_TILES = 8
INPROJ_TN = 512
MIX_TT = 256
Q_TILE = 128
KEY_CHUNK = 512
PAGES_PER_STEP = 64
MOE_TM = 256
WEIGHT_DMA_BANDS = 4


def _dot(a, b):
    return jnp.dot(a, b, preferred_element_type=F32)


def _dot_nt(a, b):
    return lax.dot_general(a, b, (((1,), (1,)), ((), ())), preferred_element_type=F32)


def _layer_norm(x, g, b):
    mu = jnp.mean(x, axis=-1, keepdims=True)
    xc = x - mu
    var = jnp.mean(xc * xc, axis=-1, keepdims=True)
    return xc * lax.rsqrt(var + LN_EPS) * g + b


def _split3(x):
    hi = x.astype(BF16)
    r = x - hi.astype(F32)
    mid = r.astype(BF16)
    lo = (r - mid.astype(F32)).astype(BF16)
    return hi, mid, lo


TM_CHUNKS = D_MODEL // LANES


def _store_token_major(ref, x):
    rows = x.shape[0]
    for c in range(TM_CHUNKS):
        ref[pl.ds(c, rows, stride=TM_CHUNKS), :] = x[:, c * LANES:(c + 1) * LANES]


def _load_token_major(ref):
    rows = ref.shape[0] // TM_CHUNKS
    return jnp.concatenate([ref[pl.ds(c, rows, stride=TM_CHUNKS), :] for c in range(TM_CHUNKS)], axis=1)


def _pages_per_step(past_len):
    return min(PAGES_PER_STEP, past_len // PAGE_SIZE)


def _params(sem):
    return pltpu.CompilerParams(dimension_semantics=sem, vmem_limit_bytes=VMEM_LIMIT)


def _inproj_kernel(x_ref, w_ref, b_ref, o_ref, xb_ref):
    @pl.when(pl.program_id(1) == 0)
    def _():
        xb_ref[...] = x_ref[...].astype(BF16)

    o_ref[...] = _dot_nt(xb_ref[...], w_ref[...].astype(BF16)) + b_ref[...]


def _inproj(x, w_in_t, b_in, layer):
    nt, d = x.shape
    n_in = w_in_t.shape[1]
    tn = INPROJ_TN
    tm = nt // INPROJ_ROW_TILES
    assert nt % (INPROJ_ROW_TILES * SUBLANES) == 0
    return pl.pallas_call(
        _inproj_kernel,
        grid=(nt // tm, pl.cdiv(n_in, tn)),
        in_specs=[
            pl.BlockSpec((tm, d), lambda i, j: (i, 0)),
            pl.BlockSpec((None, tn, d), lambda i, j: (layer, j, 0)),
            pl.BlockSpec((1, tn), lambda i, j: (0, j)),
        ],
        out_specs=pl.BlockSpec((tm, tn), lambda i, j: (i, j)),
        out_shape=jax.ShapeDtypeStruct((nt, n_in), F32),
        scratch_shapes=[pltpu.VMEM((tm, d), BF16)],
        compiler_params=_params(("parallel", "arbitrary")),
        name="inproj",
    )(x, w_in_t, b_in[layer][None, :])


def _mix_abc_kernel(aval, agate, bu, bv, cb, cc, cx, caw, cab, lag, lab, lvg, lvb, sw, sb, ccw,
                    out_ref, sta_ref, stc_ref, abuf, cbuf, *, tt):
    t = pl.program_id(1)
    halo_a = 32
    halo_c = 8

    @pl.when(t == 0)
    def _():
        abuf[0:halo_a, :] = jnp.zeros((halo_a, D_GRP), F32)
        cbuf[0:halo_c, :] = jnp.zeros((halo_c, D_GRP), F32)

    abuf[halo_a:halo_a + tt, :] = aval[...] * jax.nn.sigmoid(agate[...])
    rc = 64
    off_a = halo_a - (CONV_A_WIDTH - 1)
    for r in range(tt // rc):
        acc = jnp.zeros((rc // SUBLANES, SUBLANES, D_GRP), F32)
        for b in range(SUBLANES):
            taps = [(a, SUBLANES * a + b - off_a) for a in range((off_a + CONV_A_WIDTH) // SUBLANES + 1)
                    if 0 <= SUBLANES * a + b - off_a < CONV_A_WIDTH]
            reach = SUBLANES * max(a for a, _ in taps) + rc
            shifted = abuf[r * rc + b:r * rc + b + reach, :].reshape(reach // SUBLANES, SUBLANES, D_GRP)
            for a, k in taps:
                acc = acc + caw[k][None] * shifted[a:a + rc // SUBLANES]
        y = _layer_norm(acc.reshape(rc, D_GRP) + cab[...], lag[...], lab[...])
        out_ref[r * rc:(r + 1) * rc, 0:D_GRP] = (y * jax.nn.sigmoid(y)).astype(BF16)
    sta_ref[...] = abuf[tt:tt + halo_a, :]
    abuf[0:halo_a, :] = abuf[tt:tt + halo_a, :]

    row = lax.broadcasted_iota(jnp.int32, (CHUNK, CHUNK), 0)
    col = lax.broadcasted_iota(jnp.int32, (CHUNK, CHUNK), 1)
    for c in range(tt // CHUNK):
        rows = slice(c * CHUNK, (c + 1) * CHUNK)
        v = _layer_norm(jax.nn.gelu(bv[rows, :]), lvg[...], lvb[...]).astype(BF16)
        gu = jax.nn.gelu(bu[rows, :])
        for h in range(GMLP_HEADS):
            lanes = slice(h * CHUNK, (h + 1) * CHUNK)
            wh = jnp.where(col <= row, sw[h], 0.0).astype(BF16)
            s = _dot(wh, v[:, lanes]) + sb[:, lanes]
            out_ref[rows, D_GRP + h * CHUNK:D_GRP + (h + 1) * CHUNK] = (gu[:, lanes] * s).astype(BF16)

    cbuf[halo_c:halo_c + tt, :] = cc[...] * cx[...]
    off_c = halo_c - (CONV_C_WIDTH - 1)
    conv = jnp.zeros((tt, D_GRP), F32)
    for k in range(CONV_C_WIDTH):
        conv = conv + ccw[k:k + 1, :] * cbuf[off_c + k:off_c + k + tt, :]
    out_ref[:, 2 * D_GRP:3 * D_GRP] = (cb[...] * conv).astype(BF16)
    stc_ref[...] = cbuf[tt:tt + halo_c, :]
    cbuf[0:halo_c, :] = cbuf[tt:tt + halo_c, :]


def _mix_abc_prompt(h, p, batch, seq):
    tt = MIX_TT
    nt = seq // tt
    col_spec = lambda c: pl.BlockSpec((tt, D_GRP), lambda b, t, c=c: (b * nt + t, c))
    full = lambda a: pl.BlockSpec(a.shape, lambda b, t: (0,) * a.ndim)
    small = [p["conv_a_w8"], p["conv_a_b"], p["ln_a_g"], p["ln_a_b"], p["ln_v_g"], p["ln_v_b"],
             p["spatial_w"], p["spatial_b_rows"], p["conv_c_w"]]
    return pl.pallas_call(
        functools.partial(_mix_abc_kernel, tt=tt),
        grid=(batch, nt),
        in_specs=[col_spec(c) for c in range(7)] + [full(a) for a in small],
        out_specs=[
            pl.BlockSpec((tt, 3 * D_GRP), lambda b, t: (b * nt + t, 0)),
            pl.BlockSpec((None, 32, D_GRP), lambda b, t: (b, 0, 0)),
            pl.BlockSpec((None, 8, D_GRP), lambda b, t: (b, 0, 0)),
        ],
        out_shape=[
            jax.ShapeDtypeStruct((batch * seq, 3 * D_GRP), BF16),
            jax.ShapeDtypeStruct((batch, 32, D_GRP), F32),
            jax.ShapeDtypeStruct((batch, 8, D_GRP), F32),
        ],
        scratch_shapes=[pltpu.VMEM((32 + tt, D_GRP), F32), pltpu.VMEM((8 + tt, D_GRP), F32)],
        compiler_params=_params(("parallel", "arbitrary")),
        name="mix_abc_prompt",
    )(*([h] * 7), *small)


def _place_head(q_ref, hh, g, lane_half):
    tile = q_ref[:, (hh // 2) * LANES:(hh // 2 + 1) * LANES]
    if hh % 2 != g:
        tile = pltpu.roll(tile, HEAD_DIM, axis=1)
    return jnp.where(lane_half == g, tile * SCALE, 0.0).astype(BF16)


def _nsa_prompt_kernel(q_ref, g_ref, kvc_ref, kvs_ref, kvw_ref, posk_ref, posv_ref, pk_ref, pv_ref, pair_ref,
                       out_ref,
                       kcmp, vcmp, ks, vs, kw, vw, m_s, acc_s, sc_s, sa_s, sb_s, *, seq):
    i = pl.program_id(1)
    tq_n = Q_TILE
    n_cmp = seq // CMP_BLOCK
    n_sel = seq // SEL_BLOCK

    @pl.when(i == 0)
    def _():
        step = 256
        lane = lax.broadcasted_iota(jnp.int32, (step, LANES), 1)
        for r in range(seq // step):
            rows = slice(r * step, (r + 1) * step)
            blk = step // CMP_BLOCK
            kb = (kvc_ref[rows, 0:LANES].reshape(blk, CMP_BLOCK, LANES) * posk_ref[...][None]).sum(axis=1)
            vb = (kvc_ref[rows, LANES:2 * LANES].reshape(blk, CMP_BLOCK, LANES) * posv_ref[...][None]).sum(axis=1)
            kcmp[r * blk:(r + 1) * blk, :] = _dot(kb.astype(BF16), pk_ref[...].astype(BF16)).astype(BF16)
            vcmp[r * blk:(r + 1) * blk, :] = _dot(vb.astype(BF16), pv_ref[...].astype(BF16)).astype(BF16)
            pos = r * step + lax.broadcasted_iota(jnp.int32, (step, 1), 0)
            blk_idx = pos // SEL_BLOCK
            feat = jnp.where(lane == HEAD_DIM, blk_idx.astype(F32),
                             jnp.where(lane == HEAD_DIM + 1, (pos % SEL_BLOCK).astype(F32), 0.0))
            onehot = jnp.where(lane == blk_idx, 1.0, 0.0)
            k_sel = kvs_ref[rows, 0:LANES]
            v_sel = kvs_ref[rows, LANES:2 * LANES]
            for g in range(KV_HEADS):
                k_g = k_sel if g == 0 else pltpu.roll(k_sel, HEAD_DIM, axis=1)
                ks[g, rows, :] = jnp.concatenate([jnp.where(lane < HEAD_DIM, k_g, feat), onehot],
                                                 axis=1).astype(BF16)
                vs[g, rows, :] = jnp.where(lane // HEAD_DIM == g, v_sel, 1.0).astype(BF16)
            kw[WINDOW + r * step:WINDOW + (r + 1) * step, :] = kvw_ref[rows, 0:LANES].astype(BF16)
            vw[WINDOW + r * step:WINDOW + (r + 1) * step, :] = kvw_ref[rows, LANES:2 * LANES].astype(BF16)
        kw[0:WINDOW, :] = jnp.zeros((WINDOW, LANES), BF16)
        vw[0:WINDOW, :] = jnp.zeros((WINDOW, LANES), BF16)

    t0 = i * tq_n
    tq = t0 + lax.broadcasted_iota(jnp.int32, (tq_n, 1), 0)
    lane_half = lax.broadcasted_iota(jnp.int32, (tq_n, LANES), 1) // HEAD_DIM
    gate = jax.nn.sigmoid(g_ref[...])

    n_idx = lax.broadcasted_iota(jnp.int32, (1, n_cmp), 1)
    ok_c = (n_idx * CMP_BLOCK + (CMP_BLOCK - 1)) <= tq
    dist_c = tq.astype(F32) - (n_idx.astype(F32) * CMP_BLOCK + 0.5 * (CMP_BLOCK - 1))
    tq_l = t0 + lax.broadcasted_iota(jnp.int32, (1, tq_n), 1)
    cur = tq_l // SEL_BLOCK
    j_idx = lax.broadcasted_iota(jnp.int32, (n_sel, tq_n), 0)
    ok_s = j_idx <= cur
    forced = (j_idx == 0) | (j_idx >= cur - 1)
    n_win = WINDOW + tq_n
    off_w = lax.broadcasted_iota(jnp.int32, (1, n_win), 1)
    dist_w = lax.broadcasted_iota(jnp.int32, (tq_n, 1), 0) + WINDOW - off_w
    ok_w = (dist_w >= 0) & (dist_w <= WINDOW) & ((t0 - WINDOW + off_w) >= 0)
    nbase_w = jnp.where(ok_w, -dist_w.astype(F32), MASKED_SCORE)

    group_of = lambda hh: hh // HEADS_PER_KV
    rows_of = lambda hh: slice(hh * tq_n, (hh + 1) * tq_n)
    q_all = jnp.concatenate([_place_head(q_ref, hh, group_of(hh), lane_half) for hh in range(NSA_HEADS)],
                            axis=0)

    s_all = _dot_nt(q_all, kcmp[...])
    psum = [jnp.zeros((tq_n, n_cmp), F32) for _ in range(KV_HEADS)]
    p_list = []
    for hh in range(NSA_HEADS):
        s = s_all[rows_of(hh)] - SLOPES[hh] * dist_c
        s = jnp.where(ok_c, s, NEG_INF)
        e = jnp.exp(s - jnp.max(s, axis=1, keepdims=True))
        p = jnp.where(ok_c, e * (1.0 / jnp.sum(e, axis=1, keepdims=True)), 0.0)
        psum[group_of(hh)] = psum[group_of(hh)] + p
        p_list.append(p.astype(BF16))
    o_c = _dot(jnp.concatenate(p_list, axis=0), vcmp[...])

    n_top = float(min(N_SELECT, n_sel))
    pair = pair_ref[...]
    unsel = []
    for g in range(KV_HEADS):
        hi, mid, lo = _split3(psum[g])
        imp_t = _dot_nt(pair, hi) + _dot_nt(pair, mid) + _dot_nt(pair, lo)
        score = jnp.where(ok_s & forced, FORCE_SCORE, jnp.where(ok_s, imp_t, NEG_INF))
        sc_s[...] = score
        blocks = [score[b * SUBLANES:(b + 1) * SUBLANES] for b in range(n_sel // SUBLANES)]
        ranks = [jnp.zeros((SUBLANES, tq_n), F32) for _ in blocks]
        for k in range(n_sel):
            rk = sc_s[k:k + 1, :]
            for b, sb in enumerate(blocks):
                if b < k // SUBLANES:
                    beats = rk > sb
                elif b > k // SUBLANES:
                    beats = rk >= sb
                else:
                    beats = (rk > sb) | ((rk == sb) & (j_idx[b * SUBLANES:(b + 1) * SUBLANES] > k))
                ranks[b] = ranks[b] + jnp.where(beats, 1.0, 0.0)
        rank = jnp.concatenate(ranks, axis=0)
        unsel_t = jnp.where(rank < n_top, 0.0, MASKED_SCORE)
        unsel.append(jnp.concatenate([unsel_t, jnp.zeros((LANES - n_sel, tq_n), F32)], axis=0).T.astype(BF16))

    lane_q = lax.broadcasted_iota(jnp.int32, (tq_n, LANES), 1)

    def q_aug(hh):
        tile = q_ref[:, (hh // 2) * LANES:(hh // 2 + 1) * LANES]
        if hh % 2 == 1:
            tile = pltpu.roll(tile, HEAD_DIM, axis=1)
        feat = jnp.where(lane_q == HEAD_DIM, SEL_BLOCK * SLOPES[hh], jnp.where(lane_q == HEAD_DIM + 1, SLOPES[hh], 0.0))
        low = jnp.where(lane_q < HEAD_DIM, tile * SCALE, feat).astype(BF16)
        return jnp.concatenate([low, unsel[group_of(hh)]], axis=1)

    q_sel = [jnp.concatenate([q_aug(hh) for hh in range(g * HEADS_PER_KV, (g + 1) * HEADS_PER_KV)], axis=0)
             for g in range(KV_HEADS)]
    m_s[...] = jnp.full(m_s.shape, NEG_INF, F32)
    acc_s[...] = jnp.zeros(acc_s.shape, F32)

    def scores(c, buf):
        k0 = pl.multiple_of(c * KEY_CHUNK, KEY_CHUNK)
        for g in range(KV_HEADS):
            buf[g] = _dot_nt(q_sel[g], ks[g, pl.ds(k0, KEY_CHUNK), :])

    def softmax_pv(c, buf, on_diagonal):
        k0 = pl.multiple_of(c * KEY_CHUNK, KEY_CHUNK)
        if on_diagonal:
            causal = (k0 + lax.broadcasted_iota(jnp.int32, (1, KEY_CHUNK), 1)) <= tq
        masked = (lambda s: jnp.where(causal, s, MASKED_SCORE)) if on_diagonal else (lambda s: s)
        for g in range(KV_HEADS):
            alphas, p_list = [], []
            for h in range(HEADS_PER_KV):
                rows = rows_of(g * HEADS_PER_KV + h)
                local = slice(h * tq_n, (h + 1) * tq_n)
                m_old = m_s[rows, 0:1]
                m_new = jnp.maximum(m_old, jnp.max(masked(buf[g, local, :]), axis=1, keepdims=True))
                alphas.append(jnp.exp(m_old - m_new))
                p_list.append(jnp.exp(masked(buf[g, local, :]) - m_new).astype(BF16))
                m_s[rows, :] = jnp.broadcast_to(m_new, (tq_n, LANES))
            rows_g = slice(g * HEADS_PER_KV * tq_n, (g + 1) * HEADS_PER_KV * tq_n)
            acc_s[rows_g, :] = (jnp.concatenate(alphas, axis=0) * acc_s[rows_g, :]
                                + _dot(jnp.concatenate(p_list, axis=0), vs[g, pl.ds(k0, KEY_CHUNK), :]))

    n_full = t0 // KEY_CHUNK
    scores(0, sa_s)

    def chunk_pair(j, carry):
        scores(2 * j + 1, sb_s)
        softmax_pv(2 * j, sa_s, on_diagonal=False)
        scores(2 * j + 2, sa_s)
        softmax_pv(2 * j + 1, sb_s, on_diagonal=False)
        return carry

    lax.fori_loop(0, n_full // 2, chunk_pair, 0)

    @pl.when(n_full % 2 == 0)
    def _():
        softmax_pv(n_full, sa_s, on_diagonal=True)

    @pl.when(n_full % 2 == 1)
    def _():
        scores(n_full, sb_s)
        softmax_pv(n_full - 1, sa_s, on_diagonal=False)
        softmax_pv(n_full, sb_s, on_diagonal=True)
    o_s = jnp.concatenate(
        [acc_s[rows_of(hh), :] * (1.0 / acc_s[rows_of(hh), (1 - group_of(hh)) * HEAD_DIM:(1 - group_of(hh)) * HEAD_DIM + 1])
         for hh in range(NSA_HEADS)], axis=0)

    w0 = pl.multiple_of(t0, Q_TILE)
    s_all = _dot_nt(q_all, kw[pl.ds(w0, n_win), :])
    p_list = []
    for hh in range(NSA_HEADS):
        s = s_all[rows_of(hh)] + SLOPES[hh] * nbase_w
        e = jnp.exp(s - jnp.max(s, axis=1, keepdims=True))
        p_list.append((e * (1.0 / jnp.sum(e, axis=1, keepdims=True))).astype(BF16))
    o_w = _dot(jnp.concatenate(p_list, axis=0), vw[pl.ds(w0, n_win), :])

    for hh in range(NSA_HEADS):
        g = group_of(hh)
        rows = rows_of(hh)
        o = (gate[:, hh:hh + 1] * o_c[rows]
             + gate[:, NSA_HEADS + hh:NSA_HEADS + hh + 1] * o_s[rows]
             + gate[:, 2 * NSA_HEADS + hh:2 * NSA_HEADS + hh + 1] * o_w[rows])
        o = o[:, g * HEAD_DIM:(g + 1) * HEAD_DIM]
        out_ref[:, hh * HEAD_DIM:(hh + 1) * HEAD_DIM] = o.astype(BF16)


def _nsa_prompt(h, p, batch, seq):
    nq = seq // Q_TILE
    n_cmp = seq // CMP_BLOCK
    n_sel = seq // SEL_BLOCK
    full = lambda a: pl.BlockSpec(a.shape, lambda b, i: (0,) * a.ndim)
    pair = (jnp.arange(n_cmp)[None, :] // (SEL_BLOCK // CMP_BLOCK) == jnp.arange(n_sel)[:, None]).astype(BF16)
    small = [p["posk_rows"], p["posv_rows"], p["projk_bd"], p["projv_bd"], pair]
    kv_spec = lambda br: pl.BlockSpec((seq, 2 * LANES), lambda b, i, br=br: (b, COL_KV // (2 * LANES) + br))
    return pl.pallas_call(
        functools.partial(_nsa_prompt_kernel, seq=seq),
        grid=(batch, nq),
        in_specs=[
            pl.BlockSpec((Q_TILE, D_GRP), lambda b, i: (b * nq + i, COL_Q // D_GRP)),
            pl.BlockSpec((Q_TILE, LANES), lambda b, i: (b * nq + i, COL_G // LANES)),
            kv_spec(0), kv_spec(1), kv_spec(2),
        ] + [full(a) for a in small],
        out_specs=pl.BlockSpec((Q_TILE, D_GRP), lambda b, i: (b * nq + i, 0)),
        out_shape=jax.ShapeDtypeStruct((batch * seq, D_GRP), BF16),
        scratch_shapes=[
            pltpu.VMEM((n_cmp, LANES), BF16), pltpu.VMEM((n_cmp, LANES), BF16),
            pltpu.VMEM((KV_HEADS, seq, 2 * LANES), BF16), pltpu.VMEM((KV_HEADS, seq, LANES), BF16),
            pltpu.VMEM((seq + WINDOW, LANES), BF16), pltpu.VMEM((seq + WINDOW, LANES), BF16),
            pltpu.VMEM((NSA_HEADS * Q_TILE, LANES), F32),
            pltpu.VMEM((NSA_HEADS * Q_TILE, LANES), F32),
            pltpu.VMEM((n_sel, Q_TILE), F32),
            pltpu.VMEM((KV_HEADS, HEADS_PER_KV * Q_TILE, KEY_CHUNK), F32),
            pltpu.VMEM((KV_HEADS, HEADS_PER_KV * Q_TILE, KEY_CHUNK), F32),
        ],
        compiler_params=_params(("parallel", "arbitrary")),
        name="nsa_prompt",
    )(h, h, h, h, h, *small)


def _mix_abc_sample_kernel(hs, sta, stc, caw, cab, lag, lab, lvg, lvb, sw0, sb0, ccw,
                           out_ref, ain_ref, ccx_ref, v_ref):
    g = D_GRP
    a_in = hs[:, 0:g] * jax.nn.sigmoid(hs[:, g:2 * g])
    acc = caw[CONV_A_WIDTH - 1:CONV_A_WIDTH, :] * a_in
    for k in range(CONV_A_WIDTH - 1):
        acc = acc + caw[k:k + 1, :] * sta[k]
    y = _layer_norm(acc + cab[...], lag[...], lab[...])
    out_ref[:, 0:g] = (y * jax.nn.sigmoid(y)).astype(BF16)
    ain_ref[...] = a_in

    v = _layer_norm(jax.nn.gelu(hs[:, 3 * g:4 * g]), lvg[...], lvb[...])
    v_ref[...] = v
    s = sw0[...].astype(BF16).astype(F32) * v.astype(BF16).astype(F32) + sb0[...]
    out_ref[:, g:2 * g] = (jax.nn.gelu(hs[:, 2 * g:3 * g]) * s).astype(BF16)

    ccx = hs[:, 5 * g:6 * g] * hs[:, 6 * g:7 * g]
    ccx_ref[...] = ccx
    conv = ccw[CONV_C_WIDTH - 1:CONV_C_WIDTH, :] * ccx
    for k in range(CONV_C_WIDTH - 1):
        conv = conv + ccw[k:k + 1, :] * stc[k]
    out_ref[:, 2 * g:3 * g] = (hs[:, 4 * g:5 * g] * conv).astype(BF16)


def _mix_abc_sample(hs_abc, sta_t, stc_t, p):
    bs = hs_abc.shape[0]
    args = [hs_abc, sta_t, stc_t, p["conv_a_w"], p["conv_a_b"], p["ln_a_g"], p["ln_a_b"], p["ln_v_g"], p["ln_v_b"],
            p["spatial_w00"], p["spatial_b0"], p["conv_c_w"]]
    full = lambda a: pl.BlockSpec(a.shape, lambda i: (0,) * a.ndim)
    return pl.pallas_call(
        _mix_abc_sample_kernel,
        grid=(1,),
        in_specs=[full(a) for a in args],
        out_specs=[pl.BlockSpec((bs, 3 * D_GRP), lambda i: (0, 0))] + [pl.BlockSpec((bs, D_GRP), lambda i: (0, 0))] * 3,
        out_shape=[jax.ShapeDtypeStruct((bs, 3 * D_GRP), BF16)] + [jax.ShapeDtypeStruct((bs, D_GRP), F32)] * 3,
        compiler_params=_params(("arbitrary",)),
        name="mix_abc_sample",
    )(*args)


def _nsa_dec_cmp_kernel(pt_ref, qm_ref, slope_ref, posk_ref, posv_ref, pk_ref, pv_ref, pair_ref, *rest,
                        past_len, pages_per_step):
    pages = rest[:pages_per_step]
    oc_ref, sel_ref, kbar, vbar = rest[pages_per_step:]
    pc = pl.program_id(1)
    n_cmp = past_len // CMP_BLOCK
    n_sel = past_len // SEL_BLOCK + 1
    blk_pp = PAGE_SIZE // CMP_BLOCK
    rows_of = lambda page, kv: page[kv].reshape(LANES, PAGE_SIZE).T
    for kk in range(pages_per_step // 2):
        two_k = jnp.concatenate([rows_of(pages[2 * kk], 0), rows_of(pages[2 * kk + 1], 0)], axis=0)
        two_v = jnp.concatenate([rows_of(pages[2 * kk], 1), rows_of(pages[2 * kk + 1], 1)], axis=0)
        kb = (two_k.reshape(2 * blk_pp, CMP_BLOCK, LANES) * posk_ref[...][None]).sum(axis=1)
        vb = (two_v.reshape(2 * blk_pp, CMP_BLOCK, LANES) * posv_ref[...][None]).sum(axis=1)
        r0 = pl.multiple_of(pc * (pages_per_step * blk_pp) + kk * 2 * blk_pp, SUBLANES)
        kbar[pl.ds(r0, 2 * blk_pp), :] = kb
        vbar[pl.ds(r0, 2 * blk_pp), :] = vb

    @pl.when(pc == pl.num_programs(1) - 1)
    def _():
        kc = _dot(kbar[...].astype(BF16), pk_ref[...].astype(BF16)).astype(BF16)
        vc = _dot(vbar[...].astype(BF16), pv_ref[...].astype(BF16)).astype(BF16)
        q = (qm_ref[...] * SCALE).astype(BF16)
        slope = slope_ref[:, 0:1]
        n_idx = lax.broadcasted_iota(jnp.int32, (1, n_cmp), 1)
        center = n_idx.astype(F32) * CMP_BLOCK + 0.5 * (CMP_BLOCK - 1)
        ok_c = (n_idx * CMP_BLOCK + (CMP_BLOCK - 1)) <= past_len
        s = _dot_nt(q, kc) - slope * (float(past_len) - center)
        s = jnp.where(ok_c, s, NEG_INF)
        e = jnp.exp(s - jnp.max(s, axis=1, keepdims=True))
        p = jnp.where(ok_c, e / jnp.sum(e, axis=1, keepdims=True), 0.0)
        oc_ref[...] = _dot(p.astype(BF16), vc)
        psum = jnp.concatenate(
            [jnp.broadcast_to(jnp.sum(p[g * HEADS_PER_KV:(g + 1) * HEADS_PER_KV], axis=0, keepdims=True),
                              (HEADS_PER_KV, n_cmp)) for g in range(KV_HEADS)], axis=0)
        hi, mid, lo = _split3(psum)
        pair = pair_ref[...]
        imp = _dot(hi, pair) + _dot(mid, pair) + _dot(lo, pair)
        n_lanes = imp.shape[1]
        j_idx = lax.broadcasted_iota(jnp.int32, (1, n_lanes), 1)
        cur = past_len // SEL_BLOCK
        real = j_idx < n_sel
        ok_s = j_idx <= cur
        forced = (j_idx == 0) | (j_idx >= cur - 1)
        score = jnp.where(ok_s & forced, FORCE_SCORE, jnp.where(ok_s, imp, NEG_INF))
        score = jnp.where(real, score, NOT_A_BLOCK)
        rank = jnp.zeros(score.shape, F32)
        for k in range(n_sel):
            sk = score[:, k:k + 1]
            beats = (sk > score) | ((sk == score) & (j_idx > k))
            rank = rank + jnp.where(beats, 1.0, 0.0)
        sel_ref[...] = jnp.where((rank < float(min(N_SELECT, n_sel))) & real, 1.0, 0.0)


def _nsa_dec_cmp(page_table, qm, cache, p, layer, past_len):
    bs = qm.shape[0]
    n_pages = past_len // PAGE_SIZE
    pps = _pages_per_step(past_len)
    n_cmp = past_len // CMP_BLOCK
    n_sel = past_len // SEL_BLOCK + 1
    n_lanes = -(-n_sel // LANES) * LANES
    pair = (jnp.arange(n_cmp)[:, None] // (SEL_BLOCK // CMP_BLOCK) == jnp.arange(n_lanes)[None, :]).astype(BF16)
    small = [p["slope_rows"], p["posk_rows"], p["posv_rows"], p["projk_bd"], p["projv_bd"], pair]
    full = lambda a: pl.BlockSpec(a.shape, lambda b, pc, pt: (0,) * a.ndim)
    page_spec = lambda k: pl.BlockSpec((None, None, 2, KV_HEADS, HEAD_DIM, PAGE_SIZE),
                                       lambda b, pc, pt, k=k: (layer, pt[b, pc * pps + k], 0, 0, 0, 0))
    grid_spec = pltpu.PrefetchScalarGridSpec(
        num_scalar_prefetch=1,
        grid=(bs, n_pages // pps),
        in_specs=[pl.BlockSpec((None, NSA_HEADS, LANES), lambda b, pc, pt: (b, 0, 0))]
                 + [full(a) for a in small] + [page_spec(k) for k in range(pps)],
        out_specs=[pl.BlockSpec((None, NSA_HEADS, LANES), lambda b, pc, pt: (b, 0, 0)),
                   pl.BlockSpec((None, NSA_HEADS, n_lanes), lambda b, pc, pt: (b, 0, 0))],
        scratch_shapes=[pltpu.VMEM((n_cmp, LANES), F32), pltpu.VMEM((n_cmp, LANES), F32)],
    )
    return pl.pallas_call(
        functools.partial(_nsa_dec_cmp_kernel, past_len=past_len, pages_per_step=pps),
        grid_spec=grid_spec,
        out_shape=[jax.ShapeDtypeStruct((bs, NSA_HEADS, LANES), F32),
                   jax.ShapeDtypeStruct((bs, NSA_HEADS, n_lanes), F32)],
        compiler_params=_params(("parallel", "arbitrary")),
        name="nsa_dec_cmp",
    )(page_table, qm, *small, *([cache] * pps))


def _nsa_dec_sel_kernel(pt_ref, qm_ref, slope_ref, selr_ref, selnew_ref, knew_ref, vnew_ref, kwnew_ref, vwnew_ref,
                        win_ref, oc_ref, graw_ref, *rest, past_len, pages_per_step):
    pages = rest[:pages_per_step]
    out_ref, m_s, l_s, acc_s = rest[pages_per_step:]
    pc = pl.program_id(1)
    keys_per_step = pages_per_step * PAGE_SIZE
    q = (qm_ref[...] * SCALE).astype(BF16)
    qf = q.astype(F32)
    slope = slope_ref[:, 0:1]

    @pl.when(pc == 0)
    def _():
        m_s[...] = jnp.full(m_s.shape, NEG_INF, F32)
        l_s[...] = jnp.zeros(l_s.shape, F32)
        acc_s[...] = jnp.zeros(acc_s.shape, F32)

    lane = lax.broadcasted_iota(jnp.int32, (1, PAGE_SIZE), 1)
    selr = selr_ref[...]
    blocks_pp = PAGE_SIZE // SEL_BLOCK
    s_parts, ok_parts = [], []
    for k in range(pages_per_step):
        s = _dot(q, pages[k][0].reshape(LANES, PAGE_SIZE).astype(BF16))
        spos = pc * keys_per_step + k * PAGE_SIZE + lane
        dist = past_len - spos
        in_sel = selr[:, blocks_pp * k:blocks_pp * k + 1]
        for bb in range(1, blocks_pp):
            in_sel = jnp.where(lane // SEL_BLOCK == bb, selr[:, blocks_pp * k + bb:blocks_pp * k + bb + 1], in_sel)
        s_parts.append(s - slope * dist.astype(F32))
        ok_parts.append((in_sel > 0.5) & (dist >= 0))
    ok = jnp.concatenate(ok_parts, axis=1)
    s = jnp.where(ok, jnp.concatenate(s_parts, axis=1), NEG_INF)
    m_old = m_s[:, 0:1]
    m_new = jnp.maximum(m_old, jnp.max(s, axis=1, keepdims=True))
    alpha = jnp.exp(m_old - m_new)
    p = jnp.where(ok, jnp.exp(s - m_new), 0.0)
    l_new = alpha * l_s[:, 0:1] + jnp.sum(p, axis=1, keepdims=True)
    pb = p.astype(BF16)
    acc = alpha * acc_s[...]
    for k in range(pages_per_step):
        acc = acc + _dot_nt(pb[:, k * PAGE_SIZE:(k + 1) * PAGE_SIZE],
                            pages[k][1].reshape(LANES, PAGE_SIZE).astype(BF16))
    acc_s[...] = acc
    m_s[...] = jnp.broadcast_to(m_new, m_s.shape)
    l_s[...] = jnp.broadcast_to(l_new, l_s.shape)

    @pl.when(pc == pl.num_programs(1) - 1)
    def _():
        kn = knew_ref[...].astype(BF16).astype(F32)
        vn = vnew_ref[...].astype(BF16).astype(F32)
        s_n = jnp.sum(qf * kn, axis=1, keepdims=True)
        ok_n = selnew_ref[:, 0:1] > 0.5
        s_n = jnp.where(ok_n, s_n, NEG_INF)
        m_o = m_s[:, 0:1]
        m_f = jnp.maximum(m_o, s_n)
        a_f = jnp.exp(m_o - m_f)
        p_n = jnp.where(ok_n, jnp.exp(s_n - m_f), 0.0)
        l_f = a_f * l_s[:, 0:1] + p_n
        o_s = (a_f * acc_s[...] + p_n.astype(BF16).astype(F32) * vn) / l_f

        w_buf = win_ref.shape[-1]
        idx = lax.broadcasted_iota(jnp.int32, (1, w_buf), 1)
        dist_w = w_buf - idx
        ok_w = (dist_w <= WINDOW) & ((past_len - dist_w) >= 0)
        s_w = _dot(q, win_ref[0].reshape(LANES, w_buf).astype(BF16)) - slope * dist_w.astype(F32)
        s_w = jnp.where(ok_w, s_w, NEG_INF)
        kwn = kwnew_ref[...].astype(BF16).astype(F32)
        vwn = vwnew_ref[...].astype(BF16).astype(F32)
        s_wn = jnp.sum(qf * kwn, axis=1, keepdims=True)
        m_w = jnp.maximum(jnp.max(s_w, axis=1, keepdims=True), s_wn)
        e_w = jnp.exp(s_w - m_w)
        e_wn = jnp.exp(s_wn - m_w)
        den = jnp.sum(e_w, axis=1, keepdims=True) + e_wn
        p_w = (e_w / den).astype(BF16)
        p_wn = (e_wn / den).astype(BF16).astype(F32)
        o_w = _dot_nt(p_w, win_ref[1].reshape(LANES, w_buf).astype(BF16)) + p_wn * vwn

        gate = jax.nn.sigmoid(graw_ref[...])
        out_ref[...] = gate[:, 0:1] * oc_ref[...] + gate[:, 1:2] * o_s + gate[:, 2:3] * o_w


def _nsa_dec_sel(page_table, qm, selr, selnew, knew_s, vnew_s, knew_w, vnew_w, win, oc, graw, cache, p, layer,
                 past_len):
    bs = qm.shape[0]
    n_pages = past_len // PAGE_SIZE
    pps = _pages_per_step(past_len)
    w_buf = win.shape[-1]
    per_b =lambda a: pl.BlockSpec((None,) + a.shape[1:], lambda b, pc, pt: (b,) + (0,) * (a.ndim - 1))
    page_spec = lambda k: pl.BlockSpec((None, None, 2, KV_HEADS, HEAD_DIM, PAGE_SIZE),
                                       lambda b, pc, pt, k=k: (layer, pt[b, pc * pps + k], 0, 0, 0, 0))
    slope = p["slope_rows"]
    grid_spec = pltpu.PrefetchScalarGridSpec(
        num_scalar_prefetch=1,
        grid=(bs, n_pages // pps),
        in_specs=[
            per_b(qm),
            pl.BlockSpec(slope.shape, lambda b, pc, pt: (0, 0)),
            pl.BlockSpec((None, None, NSA_HEADS, LANES), lambda b, pc, pt: (b, pc, 0, 0)),
            per_b(selnew), per_b(knew_s), per_b(vnew_s), per_b(knew_w), per_b(vnew_w),
            pl.BlockSpec((None, None, 2, KV_HEADS, HEAD_DIM, w_buf), lambda b, pc, pt: (layer, b, 0, 0, 0, 0)),
            per_b(oc), per_b(graw),
        ] + [page_spec(k) for k in range(pps)],
        out_specs=pl.BlockSpec((None, NSA_HEADS, LANES), lambda b, pc, pt: (b, 0, 0)),
        scratch_shapes=[pltpu.VMEM((NSA_HEADS, LANES), F32)] * 3,
    )
    return pl.pallas_call(
        functools.partial(_nsa_dec_sel_kernel, past_len=past_len, pages_per_step=pps),
        grid_spec=grid_spec,
        out_shape=jax.ShapeDtypeStruct((bs, NSA_HEADS, LANES), F32),
        compiler_params=_params(("parallel", "arbitrary")),
        name="nsa_dec_sel",
    )(page_table, qm, slope, selr, selnew, knew_s, vnew_s, knew_w, vnew_w, win, oc, graw, *([cache] * pps))


def _outproj_kernel(mabc_p, md_p, mabc_s, md_s, x, wo, g1, b1, wr, br, x1_ref, ids_ref, wts_ref, *,
                    alpha, n_prompt_tiles):
    i = pl.program_id(0)

    @pl.when(i < n_prompt_tiles)
    def _():
        _outproj_tile(mabc_p, md_p, x, wo, g1, b1, wr, br, x1_ref, ids_ref, wts_ref, alpha)

    @pl.when(i >= n_prompt_tiles)
    def _():
        _outproj_tile(mabc_s, md_s, x, wo, g1, b1, wr, br, x1_ref, ids_ref, wts_ref, alpha)


def _outproj_tile(mabc, md, x, wo, g1, b1, wr, br, x1_ref, ids_ref, wts_ref, alpha):
    k_abc = mabc.shape[1]
    mix = _dot(mabc[...], wo[0:k_abc, :]) + _dot(md[...], wo[k_abc:, :])
    x1 = _layer_norm(alpha * x[...] + mix, g1[...], b1[...])
    _store_token_major(x1_ref, x1)
    x_hi, x_lo, _ = _split3(x1)
    w_hi, w_lo, _ = _split3(wr[...])
    logits = _dot(x_hi, w_hi) + (_dot(x_lo, w_hi) + _dot(x_hi, w_lo)) + br[...]
    lane = lax.broadcasted_iota(jnp.int32, logits.shape, 1)
    is_g = lane < N_GROUPS
    gl = jnp.where(is_g, logits, -jnp.inf)
    gmax = jnp.max(gl, axis=1, keepdims=True)
    gsel = jnp.min(jnp.where(gl == gmax, lane, LANES), axis=1, keepdims=True)
    ggate = 1.0 / jnp.sum(jnp.where(is_g, jnp.exp(gl - gmax), 0.0), axis=1, keepdims=True)
    lo = N_GROUPS + gsel * EXPERTS_PER_GROUP
    el = jnp.where((lane >= lo) & (lane < lo + EXPERTS_PER_GROUP), logits, -jnp.inf)
    v1 = jnp.max(el, axis=1, keepdims=True)
    i1 = jnp.min(jnp.where(el == v1, lane, LANES), axis=1, keepdims=True)
    el2 = jnp.where(lane == i1, -jnp.inf, el)
    v2 = jnp.max(el2, axis=1, keepdims=True)
    i2 = jnp.min(jnp.where(el2 == v2, lane, LANES), axis=1, keepdims=True)
    e21 = jnp.exp(v2 - v1)
    w1 = ggate / (1.0 + e21)
    w2 = ggate * e21 / (1.0 + e21)
    ids_ref[...] = jnp.where(lane == 0, i1 - N_GROUPS, jnp.where(lane == 1, i2 - N_GROUPS, 0))
    wts_ref[...] = jnp.where(lane == 0, w1, jnp.where(lane == 1, w2, 0.0))


def _outproj(mabc_p, md_p, mabc_s, md_s, x, wo_bf, g1, b1, wr_bf, br, alpha):
    nt = x.shape[0]
    tm = TOK_TILE
    n_p = mabc_p.shape[0] // tm
    assert mabc_p.shape[0] % tm == 0 and mabc_s.shape[0] == nt - mabc_p.shape[0]
    row = lambda w: pl.BlockSpec((tm, w), lambda i: (i, 0))
    prow = lambda w: pl.BlockSpec((tm, w), lambda i: (jnp.minimum(i, n_p - 1), 0))
    srow = lambda w: pl.BlockSpec((tm, w), lambda i: (jnp.maximum(i - n_p, 0), 0))
    full = lambda a: pl.BlockSpec(a.shape, lambda i: (0,) * a.ndim)
    return pl.pallas_call(
        functools.partial(_outproj_kernel, alpha=alpha, n_prompt_tiles=n_p),
        grid=(nt // tm,),
        in_specs=[prow(mabc_p.shape[1]), prow(md_p.shape[1]), srow(mabc_s.shape[1]), srow(md_s.shape[1]),
                  row(D_MODEL), full(wo_bf), full(g1), full(b1), full(wr_bf), full(br)],
        out_specs=[pl.BlockSpec((tm * TM_CHUNKS, LANES), lambda i: (i, 0)), row(LANES), row(LANES)],
        out_shape=[jax.ShapeDtypeStruct((nt * TM_CHUNKS, LANES), F32),
                   jax.ShapeDtypeStruct((nt, LANES), jnp.int32), jax.ShapeDtypeStruct((nt, LANES), F32)],
        compiler_params=_params(("parallel",)),
        name="outproj_ln1_route",
    )(mabc_p, md_p, mabc_s, md_s, x, wo_bf, g1, b1, wr_bf, br)


def _rows_wait(hbm, buf, sem):
    pltpu.make_async_copy(hbm.at[pl.ds(0, buf.shape[0])], buf, sem).wait()


def _moe_kernel(te_ref, tv_ref, first_ref, nxt_ref, st_ref, dst_ref,
                x_hbm, w_ref, wg_hbm, wu_hbm, wd_hbm, y_hbm,
                xb0, xb1, ob0, ob1, wgf, wuf, wdf, wgb, wub, wdb, gsem, ssem, wsem, *, layer):
    i = pl.program_id(0)
    tm = xb0.shape[0] // TM_CHUNKS
    xbufs, obufs = (xb0, xb1), (ob0, ob1)

    def token_rows(n):
        start = n * TM_CHUNKS
        return pl.ds(start if isinstance(n, int) else pl.multiple_of(start, TM_CHUNKS), TM_CHUNKS)

    valid = tv_ref[i] > 0
    prev_valid = tv_ref[jnp.maximum(i - 1, 0)] > 0

    def weights_copy(e, start):
        for src, dst, k in ((wg_hbm, wgf, 0), (wu_hbm, wuf, 1), (wd_hbm, wdf, 2)):
            band = dst.shape[0] // WEIGHT_DMA_BANDS
            for b in range(WEIGHT_DMA_BANDS):
                rows = pl.ds(b * band, band)
                cp = pltpu.make_async_copy(src.at[layer, e, rows, :], dst.at[rows, :], wsem.at[k])
                cp.start() if start else cp.wait()

    def start_gather(tile, k):
        for r in range(tm):
            pltpu.make_async_copy(x_hbm.at[token_rows(st_ref[tile * tm + r]), :], xbufs[k].at[token_rows(r), :],
                                  gsem.at[k]).start()

    def start_scatter(tile, k):
        for r in range(tm):
            pltpu.make_async_copy(obufs[k].at[token_rows(r), :], y_hbm.at[token_rows(dst_ref[(tile + 2) * tm + r]), :],
                                  ssem.at[k]).start()

    @pl.when(i == 0)
    def _():
        weights_copy(te_ref[0], start=True)
        start_gather(0, 0)
        ob0[...] = jnp.zeros(ob0.shape, F32)
        ob1[...] = jnp.zeros(ob1.shape, F32)
        start_scatter(-2, 0)

    @pl.when(valid & (first_ref[i] > 0))
    def _():
        weights_copy(te_ref[i], start=False)
        wgb[...] = wgf[...].astype(BF16)
        wub[...] = wuf[...].astype(BF16)
        wdb[...] = wdf[...].astype(BF16)

        @pl.when(nxt_ref[i] >= 0)
        def _():
            weights_copy(nxt_ref[i], start=True)

    def tile_step(k):
        _rows_wait(x_hbm, xbufs[k], gsem.at[k])
        _rows_wait(y_hbm, obufs[k], ssem.at[k])
        start_gather(i + 1, 1 - k)
        start_scatter(i - 1, 1 - k)
        x = _load_token_major(xbufs[k]).astype(BF16)
        hg = _dot(x, wgb[...])
        hu = _dot(x, wub[...])
        hidden = (hg * jax.nn.sigmoid(hg)) * hu * w_ref[...]
        _store_token_major(obufs[k], _dot(hidden.astype(BF16), wdb[...]))

    def drain(k):
        _rows_wait(x_hbm, xbufs[k], gsem.at[k])
        _rows_wait(y_hbm, obufs[k], ssem.at[k])
        start_scatter(i - 1, 1 - k)
        _rows_wait(y_hbm, obufs[1 - k], ssem.at[1 - k])

    for k in range(2):
        @pl.when(valid & (i % 2 == k))
        def _(k=k):
            tile_step(k)

        @pl.when(jnp.logical_not(valid) & prev_valid & (i % 2 == k))
        def _(k=k):
            drain(k)


def _moe(route, x1, w_gate_e, w_up_e, w_down_e, layer):
    nt = x1.shape[0] // TM_CHUNKS
    n_slots = route["slot_token"].shape[0]
    tm = MOE_TM
    any_spec = pl.BlockSpec(memory_space=pl.ANY)
    grid_spec = pltpu.PrefetchScalarGridSpec(
        num_scalar_prefetch=6,
        grid=(n_slots // tm,),
        in_specs=[any_spec, pl.BlockSpec((tm, 1), lambda i, *_: (i, 0)), any_spec, any_spec, any_spec],
        out_specs=any_spec,
        scratch_shapes=[
            pltpu.VMEM((tm * TM_CHUNKS, LANES), F32), pltpu.VMEM((tm * TM_CHUNKS, LANES), F32),
            pltpu.VMEM((tm * TM_CHUNKS, LANES), F32), pltpu.VMEM((tm * TM_CHUNKS, LANES), F32),
            pltpu.VMEM((D_MODEL, D_EXPERT), F32), pltpu.VMEM((D_MODEL, D_EXPERT), F32),
            pltpu.VMEM((D_EXPERT, D_MODEL), F32),
            pltpu.VMEM((D_MODEL, D_EXPERT), BF16), pltpu.VMEM((D_MODEL, D_EXPERT), BF16),
            pltpu.VMEM((D_EXPERT, D_MODEL), BF16),
            pltpu.SemaphoreType.DMA((2,)), pltpu.SemaphoreType.DMA((2,)), pltpu.SemaphoreType.DMA((3,)),
        ],
    )
    lead_dest = 2 * nt + jnp.arange(2 * tm, dtype=jnp.int32)
    return pl.pallas_call(
        functools.partial(_moe_kernel, layer=layer),
        grid_spec=grid_spec,
        out_shape=jax.ShapeDtypeStruct(((2 * nt + 2 * tm) * TM_CHUNKS, LANES), F32),
        compiler_params=_params(("arbitrary",)),
        name="moe_experts",
    )(route["tile_expert"], route["tile_valid"], route["tile_first"], route["tile_next_expert"],
      route["slot_token"], jnp.concatenate([lead_dest, route["slot_dest"]]), x1, route["slot_w"][:, None],
      w_gate_e, w_up_e, w_down_e)


def _ln2_kernel(x1, y0, y1, g2, b2, o_ref, *, alpha):
    y = _load_token_major(y0) + _load_token_major(y1)
    o_ref[...] = _layer_norm(alpha * _load_token_major(x1) + y, g2[...], b2[...])


def _ln2_split_kernel(x1, y0, y1, g2, b2, prompt_ref, rest_ref, *, alpha, n_prompt_tiles):
    y = _load_token_major(y0) + _load_token_major(y1)
    x2 = _layer_norm(alpha * _load_token_major(x1) + y, g2[...], b2[...])
    i = pl.program_id(0)

    @pl.when(i < n_prompt_tiles)
    def _():
        prompt_ref[...] = x2

    @pl.when(i >= n_prompt_tiles)
    def _():
        rest_ref[...] = x2


def _ln2(x1, y_pairs, g2, b2, alpha, split_at=None):
    nt = x1.shape[0] // TM_CHUNKS
    tm = TOK_TILE
    n_tiles = nt // tm
    first = pl.BlockSpec((tm * TM_CHUNKS, LANES), lambda i: (i, 0))
    second = pl.BlockSpec((tm * TM_CHUNKS, LANES), lambda i: (n_tiles + i, 0))
    row = pl.BlockSpec((tm, D_MODEL), lambda i: (i, 0))
    full = lambda a: pl.BlockSpec(a.shape, lambda i: (0,) * a.ndim)
    if split_at is not None:
        n_p = split_at // tm
        assert split_at % tm == 0 and 0 < n_p < n_tiles
        return pl.pallas_call(
            functools.partial(_ln2_split_kernel, alpha=alpha, n_prompt_tiles=n_p),
            grid=(n_tiles,),
            in_specs=[first, first, second, full(g2), full(b2)],
            out_specs=[pl.BlockSpec((tm, D_MODEL), lambda i: (jnp.minimum(i, n_p - 1), 0)),
                       pl.BlockSpec((tm, D_MODEL), lambda i: (jnp.maximum(i - n_p, 0), 0))],
            out_shape=[jax.ShapeDtypeStruct((split_at, D_MODEL), F32),
                       jax.ShapeDtypeStruct((nt - split_at, D_MODEL), F32)],
            compiler_params=_params(("arbitrary",)),
            name="combine_ln2_final",
        )(x1, y_pairs, y_pairs, g2, b2)
    return pl.pallas_call(
        functools.partial(_ln2_kernel, alpha=alpha),
        grid=(n_tiles,),
        in_specs=[first, first, second, full(g2), full(b2)],
        out_specs=row,
        out_shape=jax.ShapeDtypeStruct((nt, D_MODEL), F32),
        compiler_params=_params(("parallel",)),
        name="combine_ln2",
    )(x1, y_pairs, y_pairs, g2, b2)


def _route_slots(ids, wts, n_slots):
    tm = MOE_TM
    nt = ids.shape[0]
    e_flat = ids[:, :2].reshape(-1)
    w_flat = wts[:, :2].reshape(-1)
    n_pairs = e_flat.shape[0]
    take = lambda table, idx: table.at[idx].get(mode="promise_in_bounds")
    experts = jnp.arange(N_EXPERTS, dtype=jnp.int32)
    order = jnp.argsort(e_flat, stable=True).astype(jnp.int32)
    counts = jnp.sum(e_flat[:, None] == experts[None, :], axis=0).astype(jnp.int32)
    padded = (counts + tm - 1) // tm * tm
    ends = jnp.cumsum(padded)
    offs = ends - padded
    starts = jnp.cumsum(counts) - counts
    tile_start = jnp.arange(n_slots // tm, dtype=jnp.int32) * tm
    total = ends[-1]
    tile_valid = (tile_start < total).astype(jnp.int32)
    last_start = jnp.maximum(total - tm, 0)
    tile_expert = jnp.sum(jnp.minimum(tile_start, last_start)[:, None] >= ends[None, :], axis=1).astype(jnp.int32)
    tile_expert = jnp.minimum(tile_expert, N_EXPERTS - 1)
    in_run = tile_start[:, None] + jnp.arange(tm, dtype=jnp.int32)[None, :] - take(offs, tile_expert)[:, None]
    filled = (in_run < take(counts, tile_expert)[:, None]) & (tile_valid[:, None] > 0)
    src = jnp.clip(take(starts, tile_expert)[:, None] + in_run, 0, n_pairs - 1).reshape(-1)
    pair = take(order, src)
    filled = filled.reshape(-1)
    slot_token = jnp.where(filled, pair // 2, 0)
    slot_w = jnp.where(filled, take(w_flat, pair), 0.0)
    slot_idx = jnp.arange(n_slots, dtype=jnp.int32)
    slot_dest = jnp.where(filled, (pair % 2) * nt + pair // 2, 2 * nt + slot_idx % (2 * tm))
    prev_expert = jnp.concatenate([jnp.full((1,), -1, jnp.int32), tile_expert[:-1]])
    tile_first = (tile_expert != prev_expert).astype(jnp.int32)
    later = (experts[None, :] > experts[:, None]) & (counts[None, :] > 0)
    next_expert = jnp.min(jnp.where(later, experts[None, :], N_EXPERTS), axis=1)
    next_expert = jnp.where(next_expert == N_EXPERTS, -1, next_expert).astype(jnp.int32)
    return dict(slot_token=slot_token, slot_w=slot_w, slot_dest=slot_dest, tile_expert=tile_expert,
                tile_valid=tile_valid, tile_first=tile_first, tile_next_expert=take(next_expert, tile_expert))


def _transpose_kernel(x_ref, o_ref):
    o_ref[...] = x_ref[...].T


def _kv_rows_minor(h, batch, seq):
    tt = KEY_CHUNK
    nt = seq // tt
    return pl.pallas_call(
        _transpose_kernel,
        grid=(batch, 3, nt),
        in_specs=[pl.BlockSpec((tt, 2 * LANES), lambda b, br, t: (b * nt + t, COL_KV // (2 * LANES) + br))],
        out_specs=pl.BlockSpec((None, None, 2 * LANES, tt), lambda b, br, t: (b, br, 0, t)),
        out_shape=jax.ShapeDtypeStruct((batch, 3, 2 * LANES, seq), F32),
        compiler_params=_params(("parallel", "parallel", "parallel")),
        name="kv_rows_minor",
    )(h)


def _layer(x, layer, n_prompt_rows, batch, seq, bs, past_len, page_table, caches, states, weights):
    (cache_cmp, cache_sel, state_win, state_conv_a, state_conv_c) = (caches[0], caches[1], states[0], states[1],
                                                                      states[2])
    w = weights
    nt = x.shape[0]
    n_tok = n_prompt_rows + bs
    depth = w["w_in"].shape[0]
    alpha = (2.0 * depth) ** 0.25
    rep_g = lambda a: jnp.repeat(a, HEAD_DIM, axis=1)
    bd = lambda a: jnp.zeros((LANES, LANES), F32).at[:HEAD_DIM, :HEAD_DIM].set(a[0]).at[HEAD_DIM:, HEAD_DIM:].set(a[1])
    row = lambda a: a[layer][None, :]
    p = {
        "conv_a_w": w["conv_a_w"][layer], "conv_a_b": row(w["conv_a_b"]),
        "conv_a_w8": jnp.repeat(w["conv_a_w"][layer][:, None, :], SUBLANES, axis=1),
        "ln_a_g": row(w["ln_a_g"]), "ln_a_b": row(w["ln_a_b"]), "ln_v_g": row(w["ln_v_g"]), "ln_v_b": row(w["ln_v_b"]),
        "spatial_w": w["spatial_w"][layer],
        "spatial_b_rows": jnp.repeat(w["spatial_b"][layer].T, CHUNK, axis=1),
        "spatial_w00": jnp.repeat(w["spatial_w"][layer][:, 0, 0], CHUNK)[None, :],
        "spatial_b0": jnp.repeat(w["spatial_b"][layer][:, 0], CHUNK)[None, :],
        "conv_c_w": w["conv_c_w"][layer],
        "posk_rows": rep_g(w["cmp_pos_k"][layer]), "posv_rows": rep_g(w["cmp_pos_v"][layer]),
        "projk_bd": bd(w["cmp_proj_k"][layer]), "projv_bd": bd(w["cmp_proj_v"][layer]),
        "slope_rows": jnp.broadcast_to(jnp.asarray(SLOPES, F32)[:, None], (NSA_HEADS, LANES)),
    }

    h = _inproj(x, w["w_in"], w["b_in"], layer)

    mabc_p, sta_p, stc_p = _mix_abc_prompt(h, p, batch, seq)
    md_p = _nsa_prompt(h, p, batch, seq)

    hs = h[n_prompt_rows:n_tok]
    sta_t = jnp.transpose(state_conv_a[layer], (1, 0, 2))
    stc_t = jnp.transpose(state_conv_c[layer], (1, 0, 2))
    mabc_s, a_in_s, ccx_s, v_s = _mix_abc_sample(hs[:, :7 * D_GRP], sta_t, stc_t, p)
    q_s = hs[:, COL_Q:COL_KV].reshape(bs, NSA_HEADS, HEAD_DIM)
    grp = jnp.arange(NSA_HEADS) // HEADS_PER_KV
    qm = jnp.where((jnp.arange(LANES)[None, :] // HEAD_DIM == grp[:, None])[None],
                   jnp.tile(q_s, (1, 1, KV_HEADS)), 0.0)
    kv_s = hs[:, COL_KV:COL_G].reshape(bs, 3, 2, LANES)
    graw = jnp.pad(jnp.transpose(hs[:, COL_G:N_IN].reshape(bs, 3, NSA_HEADS), (0, 2, 1)),
                   ((0, 0), (0, 0), (0, LANES - 3)))
    rows_minor = lambda a: jnp.transpose(a, (0, 1, 3, 4, 5, 2))
    cache_cmp_r, cache_sel_r, win_r = rows_minor(cache_cmp), rows_minor(cache_sel), rows_minor(state_win)
    oc, sel = _nsa_dec_cmp(page_table, qm, cache_cmp_r, p, layer, past_len)
    n_steps = past_len // PAGE_SIZE // _pages_per_step(past_len)
    n_past_blocks = past_len // SEL_BLOCK
    blocks_per_step = n_past_blocks // n_steps
    selr = jnp.transpose(sel[:, :, :n_past_blocks].reshape(bs, NSA_HEADS, n_steps, blocks_per_step), (0, 2, 1, 3))
    selr = jnp.pad(selr, ((0, 0), (0, 0), (0, 0), (0, LANES - blocks_per_step)))
    selnew = jnp.broadcast_to(sel[:, :, n_past_blocks:n_past_blocks + 1], (bs, NSA_HEADS, LANES))
    d_s = _nsa_dec_sel(page_table, qm, selr, selnew, kv_s[:, 1, 0][:, None], kv_s[:, 1, 1][:, None],
                       kv_s[:, 2, 0][:, None], kv_s[:, 2, 1][:, None], win_r, oc, graw, cache_sel_r, p, layer,
                       past_len)
    d_s = d_s.reshape(bs, NSA_HEADS, KV_HEADS, HEAD_DIM)
    md_s = jnp.concatenate([d_s[:, :HEADS_PER_KV, 0], d_s[:, HEADS_PER_KV:, 1]], axis=1).reshape(bs, D_GRP)

    pad_rows = nt - n_tok
    s_rows = nt - n_prompt_rows
    mabc_s = jnp.pad(mabc_s, ((0, s_rows - bs), (0, 0)))
    md_s = jnp.pad(md_s.astype(BF16), ((0, s_rows - bs), (0, 0)))
    wr = jnp.concatenate([w["w_group"][layer], w["w_router"][layer]], axis=1)
    wr = jnp.pad(wr, ((0, 0), (0, LANES - wr.shape[1])))
    br = jnp.pad(jnp.concatenate([w["b_group"][layer], w["b_router"][layer]]), (0, LANES - N_GROUPS - N_EXPERTS))[None]
    x1, ids, wts = _outproj(mabc_p, md_p, mabc_s, md_s, x, w["w_out"][layer].astype(BF16), row(w["ln1_g"]),
                            row(w["ln1_b"]), wr, br, alpha)

    n_slots = (-(-(2 * nt) // MOE_TM) + N_EXPERTS) * MOE_TM
    route = _route_slots(ids, wts, n_slots)
    y_pairs = _moe(route, x1, w["w_gate_e"], w["w_up_e"], w["w_down_e"], layer)
    x2 = _ln2(x1, y_pairs, row(w["ln2_g"]), row(w["ln2_b"]), alpha,
              split_at=n_prompt_rows if layer == depth - 1 else None)

    kv_t = _kv_rows_minor(h, batch, seq).reshape(batch, 3, 2, KV_HEADS, HEAD_DIM, seq)
    kv_p = jnp.transpose(kv_t, (0, 5, 1, 2, 3, 4))
    w_buf = state_win.shape[2]
    kv_s6 = hs[:, COL_KV:COL_G].reshape(bs, 1, 3, 2, KV_HEADS, HEAD_DIM)
    st_p = (kv_p[:, :, 0], kv_p[:, :, 1], kv_p[:, seq - w_buf:, 2],
            sta_p[:, 32 - (CONV_A_WIDTH - 1):], stc_p[:, 8 - (CONV_C_WIDTH - 1):])
    st_s = (kv_s6[:, :, 0], kv_s6[:, :, 1],
            jnp.concatenate([state_win[layer][:, 1:], kv_s6[:, :, 2]], axis=1),
            jnp.concatenate([state_conv_a[layer][:, 1:], a_in_s[:, None]], axis=1),
            jnp.concatenate([state_conv_c[layer][:, 1:], ccx_s[:, None]], axis=1),
            v_s[:, None])
    return x2, st_p, st_s


def kernel(x_prompt, x_sample, cache_cmp_kv, cache_sel_kv, state_win_kv, state_conv_a, state_conv_c, page_table, w_in, b_in, conv_a_w, conv_a_b, ln_a_g, ln_a_b, ln_v_g, ln_v_b, spatial_w, spatial_b, conv_c_w, cmp_pos_k, cmp_pos_v, cmp_proj_k, cmp_proj_v, w_out, ln1_g, ln1_b, ln2_g, ln2_b, w_group, b_group, w_router, b_router, w_gate_e, w_up_e, w_down_e):
    batch, seq, d_model = x_prompt.shape
    bs, dec_seq, _ = x_sample.shape
    depth = w_in.shape[0]
    past_len = page_table.shape[1] * PAGE_SIZE
    assert d_model == D_MODEL and dec_seq == 1 and w_in.shape[2] == N_IN
    assert seq % KEY_CHUNK == 0 and seq % MIX_TT == 0
    assert past_len % (PAGE_SIZE * _pages_per_step(past_len)) == 0
    assert state_win_kv.shape[2] == WINDOW and past_len >= WINDOW
    n_prompt_rows = batch * seq
    n_tok = n_prompt_rows + bs
    nt = -(-n_tok // UNIFIED_ROW_MULTIPLE) * UNIFIED_ROW_MULTIPLE
    x = jnp.concatenate([x_prompt.reshape(n_prompt_rows, d_model), x_sample.reshape(bs, d_model),
                         jnp.zeros((nt - n_tok, d_model), F32)], axis=0)
    weights = dict(w_in=jnp.transpose(w_in, (0, 2, 1)), b_in=b_in, conv_a_w=conv_a_w, conv_a_b=conv_a_b, ln_a_g=ln_a_g, ln_a_b=ln_a_b,
                   ln_v_g=ln_v_g, ln_v_b=ln_v_b, spatial_w=spatial_w, spatial_b=spatial_b, conv_c_w=conv_c_w,
                   cmp_pos_k=cmp_pos_k, cmp_pos_v=cmp_pos_v, cmp_proj_k=cmp_proj_k, cmp_proj_v=cmp_proj_v,
                   w_out=w_out, ln1_g=ln1_g, ln1_b=ln1_b, ln2_g=ln2_g, ln2_b=ln2_b, w_group=w_group,
                   b_group=b_group, w_router=w_router, b_router=b_router, w_gate_e=w_gate_e, w_up_e=w_up_e,
                   w_down_e=w_down_e)
    st_p, st_s = [], []
    for layer in range(depth):
        x, sp, ss = _layer(x, layer, n_prompt_rows, batch, seq, bs, past_len, page_table,
                           (cache_cmp_kv, cache_sel_kv), (state_win_kv, state_conv_a, state_conv_c), weights)
        st_p.append(sp)
        st_s.append(ss)
    x_prompt_rows, x_rest_rows = x
    y_prompt = x_prompt_rows.reshape(batch, seq, d_model)
    y_sample = x_rest_rows[:bs].reshape(bs, 1, d_model)
    return (y_prompt, y_sample,
            jnp.stack([s[0] for s in st_p]), jnp.stack([s[1] for s in st_p]), jnp.stack([s[2] for s in st_p]),
            jnp.stack([s[3] for s in st_p]), jnp.stack([s[4] for s in st_p]),
            jnp.stack([s[0] for s in st_s]), jnp.stack([s[1] for s in st_s]), jnp.stack([s[2] for s in st_s]),
            jnp.stack([s[3] for s in st_s]), jnp.stack([s[4] for s in st_s]), jnp.stack([s[5] for s in st_s]))
```

```python
import functools

import jax
import jax.numpy as jnp
from jax import lax
from jax.experimental import pallas as pl
from jax.experimental.pallas import tpu as pltpu

F32 = jnp.float32
BF16 = jnp.bfloat16

D_MODEL = 2048
D_GRP = 512
NSA_HEADS = 8
HEAD_DIM = 64
KV_HEADS = 2
HEADS_PER_KV = 4
CONV_A_WIDTH = 31
CONV_C_WIDTH = 3
CHUNK = 128
GMLP_HEADS = 4
CMP_BLOCK = 32
SEL_BLOCK = 64
N_SELECT = 16
WINDOW = 512
PAGE_SIZE = 128
N_GROUPS = 4
EXPERTS_PER_GROUP = 8
N_EXPERTS = 32
D_EXPERT = 512
LN_EPS = 1e-5
NEG_INF = -1e30
FORCE_SCORE = 1e9
NOT_A_BLOCK = -3e38
MASKED_SCORE = -1e35
N_IN = 4888
COL_Q = 3584
COL_KV = 4096
COL_G = 4864
SCALE = HEAD_DIM ** -0.5
SLOPES = tuple(2.0 ** (-(h + 1)) for h in range(NSA_HEADS))

LANES = 128
SUBLANES = 8
VMEM_LIMIT = 56 * 1024 * 1024

TOK_TILE = 256
UNIFIED_ROW_MULTIPLE = 512
INPROJ_ROW_TILES = 8
INPROJ_TN = 512
MIX_TT = 256
Q_TILE = 128
KEY_CHUNK = 512
PAGES_PER_STEP = 64
MOE_TM = 256
WEIGHT_DMA_BANDS = 4


def _dot(a, b):
    return jnp.dot(a, b, preferred_element_type=F32)


def _dot_nt(a, b):
    return lax.dot_general(a, b, (((1,), (1,)), ((), ())), preferred_element_type=F32)


def _layer_norm(x, g, b):
    mu = jnp.mean(x, axis=-1, keepdims=True)
    xc = x - mu
    var = jnp.mean(xc * xc, axis=-1, keepdims=True)
    return xc * lax.rsqrt(var + LN_EPS) * g + b


def _split3(x):
    hi = x.astype(BF16)
    r = x - hi.astype(F32)
    mid = r.astype(BF16)
    lo = (r - mid.astype(F32)).astype(BF16)
    return hi, mid, lo


TM_CHUNKS = D_MODEL // LANES


def _store_token_major(ref, x):
    rows = x.shape[0]
    for c in range(TM_CHUNKS):
        ref[pl.ds(c, rows, stride=TM_CHUNKS), :] = x[:, c * LANES:(c + 1) * LANES]


def _load_token_major(ref):
    rows = ref.shape[0] // TM_CHUNKS
    return jnp.concatenate([ref[pl.ds(c, rows, stride=TM_CHUNKS), :] for c in range(TM_CHUNKS)], axis=1)


def _pages_per_step(past_len):
    return min(PAGES_PER_STEP, past_len // PAGE_SIZE)


def _params(sem):
    return pltpu.CompilerParams(dimension_semantics=sem, vmem_limit_bytes=VMEM_LIMIT)


def _inproj_kernel(x_ref, w_ref, b_ref, o_ref, xb_ref):
    @pl.when(pl.program_id(1) == 0)
    def _():
        xb_ref[...] = x_ref[...].astype(BF16)

    o_ref[...] = _dot_nt(xb_ref[...], w_ref[...].astype(BF16)) + b_ref[...]


def _inproj(x, w_in_t, b_in, layer):
    nt, d = x.shape
    n_in = w_in_t.shape[1]
    tn = INPROJ_TN
    tm = nt // INPROJ_ROW_TILES
    assert nt % (INPROJ_ROW_TILES * SUBLANES) == 0
    return pl.pallas_call(
        _inproj_kernel,
        grid=(nt // tm, pl.cdiv(n_in, tn)),
        in_specs=[
            pl.BlockSpec((tm, d), lambda i, j: (i, 0)),
            pl.BlockSpec((None, tn, d), lambda i, j: (layer, j, 0)),
            pl.BlockSpec((1, tn), lambda i, j: (0, j)),
        ],
        out_specs=pl.BlockSpec((tm, tn), lambda i, j: (i, j)),
        out_shape=jax.ShapeDtypeStruct((nt, n_in), F32),
        scratch_shapes=[pltpu.VMEM((tm, d), BF16)],
        compiler_params=_params(("parallel", "arbitrary")),
        name="inproj",
    )(x, w_in_t, b_in[layer][None, :])


def _mix_abc_kernel(aval, agate, bu, bv, cb, cc, cx, caw, cab, lag, lab, lvg, lvb, sw, sb, ccw,
                    out_ref, sta_ref, stc_ref, abuf, cbuf, *, tt):
    t = pl.program_id(1)
    halo_a = 32
    halo_c = 8

    @pl.when(t == 0)
    def _():
        abuf[0:halo_a, :] = jnp.zeros((halo_a, D_GRP), F32)
        cbuf[0:halo_c, :] = jnp.zeros((halo_c, D_GRP), F32)

    abuf[halo_a:halo_a + tt, :] = aval[...] * jax.nn.sigmoid(agate[...])
    rc = 64
    off_a = halo_a - (CONV_A_WIDTH - 1)
    for r in range(tt // rc):
        acc = jnp.zeros((rc // SUBLANES, SUBLANES, D_GRP), F32)
        for b in range(SUBLANES):
            taps = [(a, SUBLANES * a + b - off_a) for a in range((off_a + CONV_A_WIDTH) // SUBLANES + 1)
                    if 0 <= SUBLANES * a + b - off_a < CONV_A_WIDTH]
            reach = SUBLANES * max(a for a, _ in taps) + rc
            shifted = abuf[r * rc + b:r * rc + b + reach, :].reshape(reach // SUBLANES, SUBLANES, D_GRP)
            for a, k in taps:
                acc = acc + caw[k][None] * shifted[a:a + rc // SUBLANES]
        y = _layer_norm(acc.reshape(rc, D_GRP) + cab[...], lag[...], lab[...])
        out_ref[r * rc:(r + 1) * rc, 0:D_GRP] = (y * jax.nn.sigmoid(y)).astype(BF16)
    sta_ref[...] = abuf[tt:tt + halo_a, :]
    abuf[0:halo_a, :] = abuf[tt:tt + halo_a, :]

    row = lax.broadcasted_iota(jnp.int32, (CHUNK, CHUNK), 0)
    col = lax.broadcasted_iota(jnp.int32, (CHUNK, CHUNK), 1)
    for c in range(tt // CHUNK):
        rows = slice(c * CHUNK, (c + 1) * CHUNK)
        v = _layer_norm(jax.nn.gelu(bv[rows, :]), lvg[...], lvb[...]).astype(BF16)
        gu = jax.nn.gelu(bu[rows, :])
        for h in range(GMLP_HEADS):
            lanes = slice(h * CHUNK, (h + 1) * CHUNK)
            wh = jnp.where(col <= row, sw[h], 0.0).astype(BF16)
            s = _dot(wh, v[:, lanes]) + sb[:, lanes]
            out_ref[rows, D_GRP + h * CHUNK:D_GRP + (h + 1) * CHUNK] = (gu[:, lanes] * s).astype(BF16)

    cbuf[halo_c:halo_c + tt, :] = cc[...] * cx[...]
    off_c = halo_c - (CONV_C_WIDTH - 1)
    conv = jnp.zeros((tt, D_GRP), F32)
    for k in range(CONV_C_WIDTH):
        conv = conv + ccw[k:k + 1, :] * cbuf[off_c + k:off_c + k + tt, :]
    out_ref[:, 2 * D_GRP:3 * D_GRP] = (cb[...] * conv).astype(BF16)
    stc_ref[...] = cbuf[tt:tt + halo_c, :]
    cbuf[0:halo_c, :] = cbuf[tt:tt + halo_c, :]


def _mix_abc_prompt(h, p, batch, seq):
    tt = MIX_TT
    nt = seq // tt
    col_spec = lambda c: pl.BlockSpec((tt, D_GRP), lambda b, t, c=c: (b * nt + t, c))
    full = lambda a: pl.BlockSpec(a.shape, lambda b, t: (0,) * a.ndim)
    small = [p["conv_a_w8"], p["conv_a_b"], p["ln_a_g"], p["ln_a_b"], p["ln_v_g"], p["ln_v_b"],
             p["spatial_w"], p["spatial_b_rows"], p["conv_c_w"]]
    return pl.pallas_call(
        functools.partial(_mix_abc_kernel, tt=tt),
        grid=(batch, nt),
        in_specs=[col_spec(c) for c in range(7)] + [full(a) for a in small],
        out_specs=[
            pl.BlockSpec((tt, 3 * D_GRP), lambda b, t: (b * nt + t, 0)),
            pl.BlockSpec((None, 32, D_GRP), lambda b, t: (b, 0, 0)),
            pl.BlockSpec((None, 8, D_GRP), lambda b, t: (b, 0, 0)),
        ],
        out_shape=[
            jax.ShapeDtypeStruct((batch * seq, 3 * D_GRP), BF16),
            jax.ShapeDtypeStruct((batch, 32, D_GRP), F32),
            jax.ShapeDtypeStruct((batch, 8, D_GRP), F32),
        ],
        scratch_shapes=[pltpu.VMEM((32 + tt, D_GRP), F32), pltpu.VMEM((8 + tt, D_GRP), F32)],
        compiler_params=_params(("parallel", "arbitrary")),
        name="mix_abc_prompt",
    )(*([h] * 7), *small)


def _place_head(q_ref, hh, g, lane_half):
    tile = q_ref[:, (hh // 2) * LANES:(hh // 2 + 1) * LANES]
    if hh % 2 != g:
        tile = pltpu.roll(tile, HEAD_DIM, axis=1)
    return jnp.where(lane_half == g, tile * SCALE, 0.0).astype(BF16)


def _nsa_prompt_kernel(q_ref, g_ref, kvc_ref, kvs_ref, kvw_ref, posk_ref, posv_ref, pk_ref, pv_ref, pair_ref,
                       out_ref,
                       kcmp, vcmp, ks, vs, kw, vw, m_s, acc_s, sc_s, sa_s, sb_s, *, seq):
    i = pl.program_id(1)
    tq_n = Q_TILE
    n_cmp = seq // CMP_BLOCK
    n_sel = seq // SEL_BLOCK

    @pl.when(i == 0)
    def _():
        step = 256
        lane = lax.broadcasted_iota(jnp.int32, (step, LANES), 1)
        for r in range(seq // step):
            rows = slice(r * step, (r + 1) * step)
            blk = step // CMP_BLOCK
            kb = (kvc_ref[rows, 0:LANES].reshape(blk, CMP_BLOCK, LANES) * posk_ref[...][None]).sum(axis=1)
            vb = (kvc_ref[rows, LANES:2 * LANES].reshape(blk, CMP_BLOCK, LANES) * posv_ref[...][None]).sum(axis=1)
            kcmp[r * blk:(r + 1) * blk, :] = _dot(kb.astype(BF16), pk_ref[...].astype(BF16)).astype(BF16)
            vcmp[r * blk:(r + 1) * blk, :] = _dot(vb.astype(BF16), pv_ref[...].astype(BF16)).astype(BF16)
            pos = r * step + lax.broadcasted_iota(jnp.int32, (step, 1), 0)
            blk_idx = pos // SEL_BLOCK
            feat = jnp.where(lane == HEAD_DIM, blk_idx.astype(F32),
                             jnp.where(lane == HEAD_DIM + 1, (pos % SEL_BLOCK).astype(F32), 0.0))
            onehot = jnp.where(lane == blk_idx, 1.0, 0.0)
            k_sel = kvs_ref[rows, 0:LANES]
            v_sel = kvs_ref[rows, LANES:2 * LANES]
            for g in range(KV_HEADS):
                k_g = k_sel if g == 0 else pltpu.roll(k_sel, HEAD_DIM, axis=1)
                ks[g, rows, :] = jnp.concatenate([jnp.where(lane < HEAD_DIM, k_g, feat), onehot],
                                                 axis=1).astype(BF16)
                vs[g, rows, :] = jnp.where(lane // HEAD_DIM == g, v_sel, 1.0).astype(BF16)
            kw[WINDOW + r * step:WINDOW + (r + 1) * step, :] = kvw_ref[rows, 0:LANES].astype(BF16)
            vw[WINDOW + r * step:WINDOW + (r + 1) * step, :] = kvw_ref[rows, LANES:2 * LANES].astype(BF16)
        kw[0:WINDOW, :] = jnp.zeros((WINDOW, LANES), BF16)
        vw[0:WINDOW, :] = jnp.zeros((WINDOW, LANES), BF16)

    t0 = i * tq_n
    tq = t0 + lax.broadcasted_iota(jnp.int32, (tq_n, 1), 0)
    lane_half = lax.broadcasted_iota(jnp.int32, (tq_n, LANES), 1) // HEAD_DIM
    gate = jax.nn.sigmoid(g_ref[...])

    n_idx = lax.broadcasted_iota(jnp.int32, (1, n_cmp), 1)
    ok_c = (n_idx * CMP_BLOCK + (CMP_BLOCK - 1)) <= tq
    dist_c = tq.astype(F32) - (n_idx.astype(F32) * CMP_BLOCK + 0.5 * (CMP_BLOCK - 1))
    tq_l = t0 + lax.broadcasted_iota(jnp.int32, (1, tq_n), 1)
    cur = tq_l // SEL_BLOCK
    j_idx = lax.broadcasted_iota(jnp.int32, (n_sel, tq_n), 0)
    ok_s = j_idx <= cur
    forced = (j_idx == 0) | (j_idx >= cur - 1)
    n_win = WINDOW + tq_n
    off_w = lax.broadcasted_iota(jnp.int32, (1, n_win), 1)
    dist_w = lax.broadcasted_iota(jnp.int32, (tq_n, 1), 0) + WINDOW - off_w
    ok_w = (dist_w >= 0) & (dist_w <= WINDOW) & ((t0 - WINDOW + off_w) >= 0)
    nbase_w = jnp.where(ok_w, -dist_w.astype(F32), MASKED_SCORE)

    group_of = lambda hh: hh // HEADS_PER_KV
    rows_of = lambda hh: slice(hh * tq_n, (hh + 1) * tq_n)
    q_all = jnp.concatenate([_place_head(q_ref, hh, group_of(hh), lane_half) for hh in range(NSA_HEADS)],
                            axis=0)

    s_all = _dot_nt(q_all, kcmp[...])
    psum = [jnp.zeros((tq_n, n_cmp), F32) for _ in range(KV_HEADS)]
    p_list = []
    for hh in range(NSA_HEADS):
        s = s_all[rows_of(hh)] - SLOPES[hh] * dist_c
        s = jnp.where(ok_c, s, NEG_INF)
        e = jnp.exp(s - jnp.max(s, axis=1, keepdims=True))
        p = jnp.where(ok_c, e * (1.0 / jnp.sum(e, axis=1, keepdims=True)), 0.0)
        psum[group_of(hh)] = psum[group_of(hh)] + p
        p_list.append(p.astype(BF16))
    o_c = _dot(jnp.concatenate(p_list, axis=0), vcmp[...])

    n_top = float(min(N_SELECT, n_sel))
    pair = pair_ref[...]
    unsel = []
    for g in range(KV_HEADS):
        hi, mid, lo = _split3(psum[g])
        imp_t = _dot_nt(pair, hi) + _dot_nt(pair, mid) + _dot_nt(pair, lo)
        score = jnp.where(ok_s & forced, FORCE_SCORE, jnp.where(ok_s, imp_t, NEG_INF))
        sc_s[...] = score
        blocks = [score[b * SUBLANES:(b + 1) * SUBLANES] for b in range(n_sel // SUBLANES)]
        ranks = [jnp.zeros((SUBLANES, tq_n), F32) for _ in blocks]
        for k in range(n_sel):
            rk = sc_s[k:k + 1, :]
            for b, sb in enumerate(blocks):
                if b < k // SUBLANES:
                    beats = rk > sb
                elif b > k // SUBLANES:
                    beats = rk >= sb
                else:
                    beats = (rk > sb) | ((rk == sb) & (j_idx[b * SUBLANES:(b + 1) * SUBLANES] > k))
                ranks[b] = ranks[b] + jnp.where(beats, 1.0, 0.0)
        rank = jnp.concatenate(ranks, axis=0)
        unsel_t = jnp.where(rank < n_top, 0.0, MASKED_SCORE)
        unsel.append(jnp.concatenate([unsel_t, jnp.zeros((LANES - n_sel, tq_n), F32)], axis=0).T.astype(BF16))

    lane_q = lax.broadcasted_iota(jnp.int32, (tq_n, LANES), 1)

    def q_aug(hh):
        tile = q_ref[:, (hh // 2) * LANES:(hh // 2 + 1) * LANES]
        if hh % 2 == 1:
            tile = pltpu.roll(tile, HEAD_DIM, axis=1)
        feat = jnp.where(lane_q == HEAD_DIM, SEL_BLOCK * SLOPES[hh], jnp.where(lane_q == HEAD_DIM + 1, SLOPES[hh], 0.0))
        low = jnp.where(lane_q < HEAD_DIM, tile * SCALE, feat).astype(BF16)
        return jnp.concatenate([low, unsel[group_of(hh)]], axis=1)

    q_sel = [jnp.concatenate([q_aug(hh) for hh in range(g * HEADS_PER_KV, (g + 1) * HEADS_PER_KV)], axis=0)
             for g in range(KV_HEADS)]
    m_s[...] = jnp.full(m_s.shape, NEG_INF, F32)
    acc_s[...] = jnp.zeros(acc_s.shape, F32)

    def scores(c, buf):
        k0 = pl.multiple_of(c * KEY_CHUNK, KEY_CHUNK)
        for g in range(KV_HEADS):
            buf[g] = _dot_nt(q_sel[g], ks[g, pl.ds(k0, KEY_CHUNK), :])

    def softmax_pv(c, buf, on_diagonal):
        k0 = pl.multiple_of(c * KEY_CHUNK, KEY_CHUNK)
        if on_diagonal:
            causal = (k0 + lax.broadcasted_iota(jnp.int32, (1, KEY_CHUNK), 1)) <= tq
        masked = (lambda s: jnp.where(causal, s, MASKED_SCORE)) if on_diagonal else (lambda s: s)
        for g in range(KV_HEADS):
            alphas, p_list = [], []
            for h in range(HEADS_PER_KV):
                rows = rows_of(g * HEADS_PER_KV + h)
                local = slice(h * tq_n, (h + 1) * tq_n)
                m_old = m_s[rows, 0:1]
                m_new = jnp.maximum(m_old, jnp.max(masked(buf[g, local, :]), axis=1, keepdims=True))
                alphas.append(jnp.exp(m_old - m_new))
                p_list.append(jnp.exp(masked(buf[g, local, :]) - m_new).astype(BF16))
                m_s[rows, :] = jnp.broadcast_to(m_new, (tq_n, LANES))
            rows_g = slice(g * HEADS_PER_KV * tq_n, (g + 1) * HEADS_PER_KV * tq_n)
            acc_s[rows_g, :] = (jnp.concatenate(alphas, axis=0) * acc_s[rows_g, :]
                                + _dot(jnp.concatenate(p_list, axis=0), vs[g, pl.ds(k0, KEY_CHUNK), :]))

    n_full = t0 // KEY_CHUNK
    scores(0, sa_s)

    def chunk_pair(j, carry):
        scores(2 * j + 1, sb_s)
        softmax_pv(2 * j, sa_s, on_diagonal=False)
        scores(2 * j + 2, sa_s)
        softmax_pv(2 * j + 1, sb_s, on_diagonal=False)
        return carry

    lax.fori_loop(0, n_full // 2, chunk_pair, 0)

    @pl.when(n_full % 2 == 0)
    def _():
        softmax_pv(n_full, sa_s, on_diagonal=True)

    @pl.when(n_full % 2 == 1)
    def _():
        scores(n_full, sb_s)
        softmax_pv(n_full - 1, sa_s, on_diagonal=False)
        softmax_pv(n_full, sb_s, on_diagonal=True)
    o_s = jnp.concatenate(
        [acc_s[rows_of(hh), :] * (1.0 / acc_s[rows_of(hh), (1 - group_of(hh)) * HEAD_DIM:(1 - group_of(hh)) * HEAD_DIM + 1])
         for hh in range(NSA_HEADS)], axis=0)

    w0 = pl.multiple_of(t0, Q_TILE)
    s_all = _dot_nt(q_all, kw[pl.ds(w0, n_win), :])
    p_list = []
    for hh in range(NSA_HEADS):
        s = s_all[rows_of(hh)] + SLOPES[hh] * nbase_w
        e = jnp.exp(s - jnp.max(s, axis=1, keepdims=True))
        p_list.append((e * (1.0 / jnp.sum(e, axis=1, keepdims=True))).astype(BF16))
    o_w = _dot(jnp.concatenate(p_list, axis=0), vw[pl.ds(w0, n_win), :])

    for hh in range(NSA_HEADS):
        g = group_of(hh)
        rows = rows_of(hh)
        o = (gate[:, hh:hh + 1] * o_c[rows]
             + gate[:, NSA_HEADS + hh:NSA_HEADS + hh + 1] * o_s[rows]
             + gate[:, 2 * NSA_HEADS + hh:2 * NSA_HEADS + hh + 1] * o_w[rows])
        o = o[:, g * HEAD_DIM:(g + 1) * HEAD_DIM]
        out_ref[:, hh * HEAD_DIM:(hh + 1) * HEAD_DIM] = o.astype(BF16)


def _nsa_prompt(h, p, batch, seq):
    nq = seq // Q_TILE
    n_cmp = seq // CMP_BLOCK
    n_sel = seq // SEL_BLOCK
    full = lambda a: pl.BlockSpec(a.shape, lambda b, i: (0,) * a.ndim)
    pair = (jnp.arange(n_cmp)[None, :] // (SEL_BLOCK // CMP_BLOCK) == jnp.arange(n_sel)[:, None]).astype(BF16)
    small = [p["posk_rows"], p["posv_rows"], p["projk_bd"], p["projv_bd"], pair]
    kv_spec = lambda br: pl.BlockSpec((seq, 2 * LANES), lambda b, i, br=br: (b, COL_KV // (2 * LANES) + br))
    return pl.pallas_call(
        functools.partial(_nsa_prompt_kernel, seq=seq),
        grid=(batch, nq),
        in_specs=[
            pl.BlockSpec((Q_TILE, D_GRP), lambda b, i: (b * nq + i, COL_Q // D_GRP)),
            pl.BlockSpec((Q_TILE, LANES), lambda b, i: (b * nq + i, COL_G // LANES)),
            kv_spec(0), kv_spec(1), kv_spec(2),
        ] + [full(a) for a in small],
        out_specs=pl.BlockSpec((Q_TILE, D_GRP), lambda b, i: (b * nq + i, 0)),
        out_shape=jax.ShapeDtypeStruct((batch * seq, D_GRP), BF16),
        scratch_shapes=[
            pltpu.VMEM((n_cmp, LANES), BF16), pltpu.VMEM((n_cmp, LANES), BF16),
            pltpu.VMEM((KV_HEADS, seq, 2 * LANES), BF16), pltpu.VMEM((KV_HEADS, seq, LANES), BF16),
            pltpu.VMEM((seq + WINDOW, LANES), BF16), pltpu.VMEM((seq + WINDOW, LANES), BF16),
            pltpu.VMEM((NSA_HEADS * Q_TILE, LANES), F32),
            pltpu.VMEM((NSA_HEADS * Q_TILE, LANES), F32),
            pltpu.VMEM((n_sel, Q_TILE), F32),
            pltpu.VMEM((KV_HEADS, HEADS_PER_KV * Q_TILE, KEY_CHUNK), F32),
            pltpu.VMEM((KV_HEADS, HEADS_PER_KV * Q_TILE, KEY_CHUNK), F32),
        ],
        compiler_params=_params(("parallel", "arbitrary")),
        name="nsa_prompt",
    )(h, h, h, h, h, *small)


def _mix_abc_sample_kernel(hs, sta, stc, caw, cab, lag, lab, lvg, lvb, sw0, sb0, ccw,
                           out_ref, ain_ref, ccx_ref, v_ref):
    g = D_GRP
    a_in = hs[:, 0:g] * jax.nn.sigmoid(hs[:, g:2 * g])
    acc = caw[CONV_A_WIDTH - 1:CONV_A_WIDTH, :] * a_in
    for k in range(CONV_A_WIDTH - 1):
        acc = acc + caw[k:k + 1, :] * sta[k]
    y = _layer_norm(acc + cab[...], lag[...], lab[...])
    out_ref[:, 0:g] = (y * jax.nn.sigmoid(y)).astype(BF16)
    ain_ref[...] = a_in

    v = _layer_norm(jax.nn.gelu(hs[:, 3 * g:4 * g]), lvg[...], lvb[...])
    v_ref[...] = v
    s = sw0[...].astype(BF16).astype(F32) * v.astype(BF16).astype(F32) + sb0[...]
    out_ref[:, g:2 * g] = (jax.nn.gelu(hs[:, 2 * g:3 * g]) * s).astype(BF16)

    ccx = hs[:, 5 * g:6 * g] * hs[:, 6 * g:7 * g]
    ccx_ref[...] = ccx
    conv = ccw[CONV_C_WIDTH - 1:CONV_C_WIDTH, :] * ccx
    for k in range(CONV_C_WIDTH - 1):
        conv = conv + ccw[k:k + 1, :] * stc[k]
    out_ref[:, 2 * g:3 * g] = (hs[:, 4 * g:5 * g] * conv).astype(BF16)


def _mix_abc_sample(hs_abc, sta_t, stc_t, p):
    bs = hs_abc.shape[0]
    args = [hs_abc, sta_t, stc_t, p["conv_a_w"], p["conv_a_b"], p["ln_a_g"], p["ln_a_b"], p["ln_v_g"], p["ln_v_b"],
            p["spatial_w00"], p["spatial_b0"], p["conv_c_w"]]
    full = lambda a: pl.BlockSpec(a.shape, lambda i: (0,) * a.ndim)
    return pl.pallas_call(
        _mix_abc_sample_kernel,
        grid=(1,),
        in_specs=[full(a) for a in args],
        out_specs=[pl.BlockSpec((bs, 3 * D_GRP), lambda i: (0, 0))] + [pl.BlockSpec((bs, D_GRP), lambda i: (0, 0))] * 3,
        out_shape=[jax.ShapeDtypeStruct((bs, 3 * D_GRP), BF16)] + [jax.ShapeDtypeStruct((bs, D_GRP), F32)] * 3,
        compiler_params=_params(("arbitrary",)),
        name="mix_abc_sample",
    )(*args)


def _nsa_dec_cmp_kernel(pt_ref, qm_ref, slope_ref, posk_ref, posv_ref, pk_ref, pv_ref, pair_ref, *rest,
                        past_len, pages_per_step):
    pages = rest[:pages_per_step]
    oc_ref, sel_ref, kbar, vbar = rest[pages_per_step:]
    pc = pl.program_id(1)
    n_cmp = past_len // CMP_BLOCK
    n_sel = past_len // SEL_BLOCK + 1
    blk_pp = PAGE_SIZE // CMP_BLOCK
    rows_of = lambda page, kv: page[kv].reshape(LANES, PAGE_SIZE).T
    for kk in range(pages_per_step // 2):
        two_k = jnp.concatenate([rows_of(pages[2 * kk], 0), rows_of(pages[2 * kk + 1], 0)], axis=0)
        two_v = jnp.concatenate([rows_of(pages[2 * kk], 1), rows_of(pages[2 * kk + 1], 1)], axis=0)
        kb = (two_k.reshape(2 * blk_pp, CMP_BLOCK, LANES) * posk_ref[...][None]).sum(axis=1)
        vb = (two_v.reshape(2 * blk_pp, CMP_BLOCK, LANES) * posv_ref[...][None]).sum(axis=1)
        r0 = pl.multiple_of(pc * (pages_per_step * blk_pp) + kk * 2 * blk_pp, SUBLANES)
        kbar[pl.ds(r0, 2 * blk_pp), :] = kb
        vbar[pl.ds(r0, 2 * blk_pp), :] = vb

    @pl.when(pc == pl.num_programs(1) - 1)
    def _():
        kc = _dot(kbar[...].astype(BF16), pk_ref[...].astype(BF16)).astype(BF16)
        vc = _dot(vbar[...].astype(BF16), pv_ref[...].astype(BF16)).astype(BF16)
        q = (qm_ref[...] * SCALE).astype(BF16)
        slope = slope_ref[:, 0:1]
        n_idx = lax.broadcasted_iota(jnp.int32, (1, n_cmp), 1)
        center = n_idx.astype(F32) * CMP_BLOCK + 0.5 * (CMP_BLOCK - 1)
        ok_c = (n_idx * CMP_BLOCK + (CMP_BLOCK - 1)) <= past_len
        s = _dot_nt(q, kc) - slope * (float(past_len) - center)
        s = jnp.where(ok_c, s, NEG_INF)
        e = jnp.exp(s - jnp.max(s, axis=1, keepdims=True))
        p = jnp.where(ok_c, e / jnp.sum(e, axis=1, keepdims=True), 0.0)
        oc_ref[...] = _dot(p.astype(BF16), vc)
        psum = jnp.concatenate(
            [jnp.broadcast_to(jnp.sum(p[g * HEADS_PER_KV:(g + 1) * HEADS_PER_KV], axis=0, keepdims=True),
                              (HEADS_PER_KV, n_cmp)) for g in range(KV_HEADS)], axis=0)
        hi, mid, lo = _split3(psum)
        pair = pair_ref[...]
        imp = _dot(hi, pair) + _dot(mid, pair) + _dot(lo, pair)
        n_lanes = imp.shape[1]
        j_idx = lax.broadcasted_iota(jnp.int32, (1, n_lanes), 1)
        cur = past_len // SEL_BLOCK
        real = j_idx < n_sel
        ok_s = j_idx <= cur
        forced = (j_idx == 0) | (j_idx >= cur - 1)
        score = jnp.where(ok_s & forced, FORCE_SCORE, jnp.where(ok_s, imp, NEG_INF))
        score = jnp.where(real, score, NOT_A_BLOCK)
        rank = jnp.zeros(score.shape, F32)
        for k in range(n_sel):
            sk = score[:, k:k + 1]
            beats = (sk > score) | ((sk == score) & (j_idx > k))
            rank = rank + jnp.where(beats, 1.0, 0.0)
        sel_ref[...] = jnp.where((rank < float(min(N_SELECT, n_sel))) & real, 1.0, 0.0)


def _nsa_dec_cmp(page_table, qm, cache, p, layer, past_len):
    bs = qm.shape[0]
    n_pages = past_len // PAGE_SIZE
    pps = _pages_per_step(past_len)
    n_cmp = past_len // CMP_BLOCK
    n_sel = past_len // SEL_BLOCK + 1
    n_lanes = -(-n_sel // LANES) * LANES
    pair = (jnp.arange(n_cmp)[:, None] // (SEL_BLOCK // CMP_BLOCK) == jnp.arange(n_lanes)[None, :]).astype(BF16)
    small = [p["slope_rows"], p["posk_rows"], p["posv_rows"], p["projk_bd"], p["projv_bd"], pair]
    full = lambda a: pl.BlockSpec(a.shape, lambda b, pc, pt: (0,) * a.ndim)
    page_spec = lambda k: pl.BlockSpec((None, None, 2, KV_HEADS, HEAD_DIM, PAGE_SIZE),
                                       lambda b, pc, pt, k=k: (layer, pt[b, pc * pps + k], 0, 0, 0, 0))
    grid_spec = pltpu.PrefetchScalarGridSpec(
        num_scalar_prefetch=1,
        grid=(bs, n_pages // pps),
        in_specs=[pl.BlockSpec((None, NSA_HEADS, LANES), lambda b, pc, pt: (b, 0, 0))]
                 + [full(a) for a in small] + [page_spec(k) for k in range(pps)],
        out_specs=[pl.BlockSpec((None, NSA_HEADS, LANES), lambda b, pc, pt: (b, 0, 0)),
                   pl.BlockSpec((None, NSA_HEADS, n_lanes), lambda b, pc, pt: (b, 0, 0))],
        scratch_shapes=[pltpu.VMEM((n_cmp, LANES), F32), pltpu.VMEM((n_cmp, LANES), F32)],
    )
    return pl.pallas_call(
        functools.partial(_nsa_dec_cmp_kernel, past_len=past_len, pages_per_step=pps),
        grid_spec=grid_spec,
        out_shape=[jax.ShapeDtypeStruct((bs, NSA_HEADS, LANES), F32),
                   jax.ShapeDtypeStruct((bs, NSA_HEADS, n_lanes), F32)],
        compiler_params=_params(("parallel", "arbitrary")),
        name="nsa_dec_cmp",
    )(page_table, qm, *small, *([cache] * pps))


def _nsa_dec_sel_kernel(pt_ref, qm_ref, slope_ref, selr_ref, selnew_ref, knew_ref, vnew_ref, kwnew_ref, vwnew_ref,
                        win_ref, oc_ref, graw_ref, *rest, past_len, pages_per_step):
    pages = rest[:pages_per_step]
    out_ref, m_s, l_s, acc_s = rest[pages_per_step:]
    pc = pl.program_id(1)
    keys_per_step = pages_per_step * PAGE_SIZE
    q = (qm_ref[...] * SCALE).astype(BF16)
    qf = q.astype(F32)
    slope = slope_ref[:, 0:1]

    @pl.when(pc == 0)
    def _():
        m_s[...] = jnp.full(m_s.shape, NEG_INF, F32)
        l_s[...] = jnp.zeros(l_s.shape, F32)
        acc_s[...] = jnp.zeros(acc_s.shape, F32)

    lane = lax.broadcasted_iota(jnp.int32, (1, PAGE_SIZE), 1)
    selr = selr_ref[...]
    blocks_pp = PAGE_SIZE // SEL_BLOCK
    s_parts, ok_parts = [], []
    for k in range(pages_per_step):
        s = _dot(q, pages[k][0].reshape(LANES, PAGE_SIZE).astype(BF16))
        spos = pc * keys_per_step + k * PAGE_SIZE + lane
        dist = past_len - spos
        in_sel = selr[:, blocks_pp * k:blocks_pp * k + 1]
        for bb in range(1, blocks_pp):
            in_sel = jnp.where(lane // SEL_BLOCK == bb, selr[:, blocks_pp * k + bb:blocks_pp * k + bb + 1], in_sel)
        s_parts.append(s - slope * dist.astype(F32))
        ok_parts.append((in_sel > 0.5) & (dist >= 0))
    ok = jnp.concatenate(ok_parts, axis=1)
    s = jnp.where(ok, jnp.concatenate(s_parts, axis=1), NEG_INF)
    m_old = m_s[:, 0:1]
    m_new = jnp.maximum(m_old, jnp.max(s, axis=1, keepdims=True))
    alpha = jnp.exp(m_old - m_new)
    p = jnp.where(ok, jnp.exp(s - m_new), 0.0)
    l_new = alpha * l_s[:, 0:1] + jnp.sum(p, axis=1, keepdims=True)
    pb = p.astype(BF16)
    acc = alpha * acc_s[...]
    for k in range(pages_per_step):
        acc = acc + _dot_nt(pb[:, k * PAGE_SIZE:(k + 1) * PAGE_SIZE],
                            pages[k][1].reshape(LANES, PAGE_SIZE).astype(BF16))
    acc_s[...] = acc
    m_s[...] = jnp.broadcast_to(m_new, m_s.shape)
    l_s[...] = jnp.broadcast_to(l_new, l_s.shape)

    @pl.when(pc == pl.num_programs(1) - 1)
    def _():
        kn = knew_ref[...].astype(BF16).astype(F32)
        vn = vnew_ref[...].astype(BF16).astype(F32)
        s_n = jnp.sum(qf * kn, axis=1, keepdims=True)
        ok_n = selnew_ref[:, 0:1] > 0.5
        s_n = jnp.where(ok_n, s_n, NEG_INF)
        m_o = m_s[:, 0:1]
        m_f = jnp.maximum(m_o, s_n)
        a_f = jnp.exp(m_o - m_f)
        p_n = jnp.where(ok_n, jnp.exp(s_n - m_f), 0.0)
        l_f = a_f * l_s[:, 0:1] + p_n
        o_s = (a_f * acc_s[...] + p_n.astype(BF16).astype(F32) * vn) / l_f

        w_buf = win_ref.shape[-1]
        idx = lax.broadcasted_iota(jnp.int32, (1, w_buf), 1)
        dist_w = w_buf - idx
        ok_w = (dist_w <= WINDOW) & ((past_len - dist_w) >= 0)
        s_w = _dot(q, win_ref[0].reshape(LANES, w_buf).astype(BF16)) - slope * dist_w.astype(F32)
        s_w = jnp.where(ok_w, s_w, NEG_INF)
        kwn = kwnew_ref[...].astype(BF16).astype(F32)
        vwn = vwnew_ref[...].astype(BF16).astype(F32)
        s_wn = jnp.sum(qf * kwn, axis=1, keepdims=True)
        m_w = jnp.maximum(jnp.max(s_w, axis=1, keepdims=True), s_wn)
        e_w = jnp.exp(s_w - m_w)
        e_wn = jnp.exp(s_wn - m_w)
        den = jnp.sum(e_w, axis=1, keepdims=True) + e_wn
        p_w = (e_w / den).astype(BF16)
        p_wn = (e_wn / den).astype(BF16).astype(F32)
        o_w = _dot_nt(p_w, win_ref[1].reshape(LANES, w_buf).astype(BF16)) + p_wn * vwn

        gate = jax.nn.sigmoid(graw_ref[...])
        out_ref[...] = gate[:, 0:1] * oc_ref[...] + gate[:, 1:2] * o_s + gate[:, 2:3] * o_w


def _nsa_dec_sel(page_table, qm, selr, selnew, knew_s, vnew_s, knew_w, vnew_w, win, oc, graw, cache, p, layer,
                 past_len):
    bs = qm.shape[0]
    n_pages = past_len // PAGE_SIZE
    pps = _pages_per_step(past_len)
    w_buf = win.shape[-1]
    per_b =lambda a: pl.BlockSpec((None,) + a.shape[1:], lambda b, pc, pt: (b,) + (0,) * (a.ndim - 1))
    page_spec = lambda k: pl.BlockSpec((None, None, 2, KV_HEADS, HEAD_DIM, PAGE_SIZE),
                                       lambda b, pc, pt, k=k: (layer, pt[b, pc * pps + k], 0, 0, 0, 0))
    slope = p["slope_rows"]
    grid_spec = pltpu.PrefetchScalarGridSpec(
        num_scalar_prefetch=1,
        grid=(bs, n_pages // pps),
        in_specs=[
            per_b(qm),
            pl.BlockSpec(slope.shape, lambda b, pc, pt: (0, 0)),
            pl.BlockSpec((None, None, NSA_HEADS, LANES), lambda b, pc, pt: (b, pc, 0, 0)),
            per_b(selnew), per_b(knew_s), per_b(vnew_s), per_b(knew_w), per_b(vnew_w),
            pl.BlockSpec((None, None, 2, KV_HEADS, HEAD_DIM, w_buf), lambda b, pc, pt: (layer, b, 0, 0, 0, 0)),
            per_b(oc), per_b(graw),
        ] + [page_spec(k) for k in range(pps)],
        out_specs=pl.BlockSpec((None, NSA_HEADS, LANES), lambda b, pc, pt: (b, 0, 0)),
        scratch_shapes=[pltpu.VMEM((NSA_HEADS, LANES), F32)] * 3,
    )
    return pl.pallas_call(
        functools.partial(_nsa_dec_sel_kernel, past_len=past_len, pages_per_step=pps),
        grid_spec=grid_spec,
        out_shape=jax.ShapeDtypeStruct((bs, NSA_HEADS, LANES), F32),
        compiler_params=_params(("parallel", "arbitrary")),
        name="nsa_dec_sel",
    )(page_table, qm, slope, selr, selnew, knew_s, vnew_s, knew_w, vnew_w, win, oc, graw, *([cache] * pps))


def _outproj_kernel(mabc_p, md_p, mabc_s, md_s, x, wo, g1, b1, wr, br, x1_ref, ids_ref, wts_ref, *,
                    alpha, n_prompt_tiles):
    i = pl.program_id(0)

    @pl.when(i < n_prompt_tiles)
    def _():
        _outproj_tile(mabc_p, md_p, x, wo, g1, b1, wr, br, x1_ref, ids_ref, wts_ref, alpha)

    @pl.when(i >= n_prompt_tiles)
    def _():
        _outproj_tile(mabc_s, md_s, x, wo, g1, b1, wr, br, x1_ref, ids_ref, wts_ref, alpha)


def _outproj_tile(mabc, md, x, wo, g1, b1, wr, br, x1_ref, ids_ref, wts_ref, alpha):
    k_abc = mabc.shape[1]
    mix = _dot(mabc[...], wo[0:k_abc, :]) + _dot(md[...], wo[k_abc:, :])
    x1 = _layer_norm(alpha * x[...] + mix, g1[...], b1[...])
    _store_token_major(x1_ref, x1)
    x_hi, x_lo, _ = _split3(x1)
    w_hi, w_lo, _ = _split3(wr[...])
    logits = _dot(x_hi, w_hi) + (_dot(x_lo, w_hi) + _dot(x_hi, w_lo)) + br[...]
    lane = lax.broadcasted_iota(jnp.int32, logits.shape, 1)
    is_g = lane < N_GROUPS
    gl = jnp.where(is_g, logits, -jnp.inf)
    gmax = jnp.max(gl, axis=1, keepdims=True)
    gsel = jnp.min(jnp.where(gl == gmax, lane, LANES), axis=1, keepdims=True)
    ggate = 1.0 / jnp.sum(jnp.where(is_g, jnp.exp(gl - gmax), 0.0), axis=1, keepdims=True)
    lo = N_GROUPS + gsel * EXPERTS_PER_GROUP
    el = jnp.where((lane >= lo) & (lane < lo + EXPERTS_PER_GROUP), logits, -jnp.inf)
    v1 = jnp.max(el, axis=1, keepdims=True)
    i1 = jnp.min(jnp.where(el == v1, lane, LANES), axis=1, keepdims=True)
    el2 = jnp.where(lane == i1, -jnp.inf, el)
    v2 = jnp.max(el2, axis=1, keepdims=True)
    i2 = jnp.min(jnp.where(el2 == v2, lane, LANES), axis=1, keepdims=True)
    e21 = jnp.exp(v2 - v1)
    w1 = ggate / (1.0 + e21)
    w2 = ggate * e21 / (1.0 + e21)
    ids_ref[...] = jnp.where(lane == 0, i1 - N_GROUPS, jnp.where(lane == 1, i2 - N_GROUPS, 0))
    wts_ref[...] = jnp.where(lane == 0, w1, jnp.where(lane == 1, w2, 0.0))


def _outproj(mabc_p, md_p, mabc_s, md_s, x, wo_bf, g1, b1, wr_bf, br, alpha):
    nt = x.shape[0]
    tm = TOK_TILE
    n_p = mabc_p.shape[0] // tm
    assert mabc_p.shape[0] % tm == 0 and mabc_s.shape[0] == nt - mabc_p.shape[0]
    row = lambda w: pl.BlockSpec((tm, w), lambda i: (i, 0))
    prow = lambda w: pl.BlockSpec((tm, w), lambda i: (jnp.minimum(i, n_p - 1), 0))
    srow = lambda w: pl.BlockSpec((tm, w), lambda i: (jnp.maximum(i - n_p, 0), 0))
    full = lambda a: pl.BlockSpec(a.shape, lambda i: (0,) * a.ndim)
    return pl.pallas_call(
        functools.partial(_outproj_kernel, alpha=alpha, n_prompt_tiles=n_p),
        grid=(nt // tm,),
        in_specs=[prow(mabc_p.shape[1]), prow(md_p.shape[1]), srow(mabc_s.shape[1]), srow(md_s.shape[1]),
                  row(D_MODEL), full(wo_bf), full(g1), full(b1), full(wr_bf), full(br)],
        out_specs=[pl.BlockSpec((tm * TM_CHUNKS, LANES), lambda i: (i, 0)), row(LANES), row(LANES)],
        out_shape=[jax.ShapeDtypeStruct((nt * TM_CHUNKS, LANES), F32),
                   jax.ShapeDtypeStruct((nt, LANES), jnp.int32), jax.ShapeDtypeStruct((nt, LANES), F32)],
        compiler_params=_params(("parallel",)),
        name="outproj_ln1_route",
    )(mabc_p, md_p, mabc_s, md_s, x, wo_bf, g1, b1, wr_bf, br)


def _rows_wait(hbm, buf, sem):
    pltpu.make_async_copy(hbm.at[pl.ds(0, buf.shape[0])], buf, sem).wait()


def _moe_kernel(te_ref, tv_ref, first_ref, nxt_ref, st_ref, dst_ref,
                x_hbm, w_ref, wg_hbm, wu_hbm, wd_hbm, y_hbm,
                xb0, xb1, ob0, ob1, wgf, wuf, wdf, wgb, wub, wdb, gsem, ssem, wsem, *, layer):
    i = pl.program_id(0)
    tm = xb0.shape[0] // TM_CHUNKS
    xbufs, obufs = (xb0, xb1), (ob0, ob1)

    def token_rows(n):
        start = n * TM_CHUNKS
        return pl.ds(start if isinstance(n, int) else pl.multiple_of(start, TM_CHUNKS), TM_CHUNKS)

    valid = tv_ref[i] > 0
    prev_valid = tv_ref[jnp.maximum(i - 1, 0)] > 0

    def weights_copy(e, start):
        for src, dst, k in ((wg_hbm, wgf, 0), (wu_hbm, wuf, 1), (wd_hbm, wdf, 2)):
            band = dst.shape[0] // WEIGHT_DMA_BANDS
            for b in range(WEIGHT_DMA_BANDS):
                rows = pl.ds(b * band, band)
                cp = pltpu.make_async_copy(src.at[layer, e, rows, :], dst.at[rows, :], wsem.at[k])
                cp.start(priority=1) if start else cp.wait()

    def start_gather(tile, k):
        for r in range(tm):
            pltpu.make_async_copy(x_hbm.at[token_rows(st_ref[tile * tm + r]), :], xbufs[k].at[token_rows(r), :],
                                  gsem.at[k]).start()

    def start_scatter(tile, k):
        for r in range(tm):
            pltpu.make_async_copy(obufs[k].at[token_rows(r), :], y_hbm.at[token_rows(dst_ref[(tile + 2) * tm + r]), :],
                                  ssem.at[k]).start(priority=r % 2)

    @pl.when(i == 0)
    def _():
        weights_copy(te_ref[0], start=True)
        start_gather(0, 0)
        ob0[...] = jnp.zeros(ob0.shape, F32)
        ob1[...] = jnp.zeros(ob1.shape, F32)
        start_scatter(-2, 0)

    @pl.when(valid & (first_ref[i] > 0))
    def _():
        weights_copy(te_ref[i], start=False)
        wgb[...] = wgf[...].astype(BF16)
        wub[...] = wuf[...].astype(BF16)
        wdb[...] = wdf[...].astype(BF16)

        @pl.when(nxt_ref[i] >= 0)
        def _():
            weights_copy(nxt_ref[i], start=True)

    def tile_step(k):
        _rows_wait(x_hbm, xbufs[k], gsem.at[k])
        _rows_wait(y_hbm, obufs[k], ssem.at[k])
        start_gather(i + 1, 1 - k)
        start_scatter(i - 1, 1 - k)
        x = _load_token_major(xbufs[k]).astype(BF16)
        hg = _dot(x, wgb[...])
        hu = _dot(x, wub[...])
        hidden = (hg * jax.nn.sigmoid(hg)) * hu * w_ref[...]
        _store_token_major(obufs[k], _dot(hidden.astype(BF16), wdb[...]))

    def drain(k):
        _rows_wait(x_hbm, xbufs[k], gsem.at[k])
        _rows_wait(y_hbm, obufs[k], ssem.at[k])
        start_scatter(i - 1, 1 - k)
        _rows_wait(y_hbm, obufs[1 - k], ssem.at[1 - k])

    for k in range(2):
        @pl.when(valid & (i % 2 == k))
        def _(k=k):
            tile_step(k)

        @pl.when(jnp.logical_not(valid) & prev_valid & (i % 2 == k))
        def _(k=k):
            drain(k)


def _moe(route, x1, w_gate_e, w_up_e, w_down_e, layer):
    nt = x1.shape[0] // TM_CHUNKS
    n_slots = route["slot_token"].shape[0]
    tm = MOE_TM
    any_spec = pl.BlockSpec(memory_space=pl.ANY)
    grid_spec = pltpu.PrefetchScalarGridSpec(
        num_scalar_prefetch=6,
        grid=(n_slots // tm,),
        in_specs=[any_spec, pl.BlockSpec((tm, 1), lambda i, *_: (i, 0)), any_spec, any_spec, any_spec],
        out_specs=any_spec,
        scratch_shapes=[
            pltpu.VMEM((tm * TM_CHUNKS, LANES), F32), pltpu.VMEM((tm * TM_CHUNKS, LANES), F32),
            pltpu.VMEM((tm * TM_CHUNKS, LANES), F32), pltpu.VMEM((tm * TM_CHUNKS, LANES), F32),
            pltpu.VMEM((D_MODEL, D_EXPERT), F32), pltpu.VMEM((D_MODEL, D_EXPERT), F32),
            pltpu.VMEM((D_EXPERT, D_MODEL), F32),
            pltpu.VMEM((D_MODEL, D_EXPERT), BF16), pltpu.VMEM((D_MODEL, D_EXPERT), BF16),
            pltpu.VMEM((D_EXPERT, D_MODEL), BF16),
            pltpu.SemaphoreType.DMA((2,)), pltpu.SemaphoreType.DMA((2,)), pltpu.SemaphoreType.DMA((3,)),
        ],
    )
    lead_dest = 2 * nt + jnp.arange(2 * tm, dtype=jnp.int32)
    return pl.pallas_call(
        functools.partial(_moe_kernel, layer=layer),
        grid_spec=grid_spec,
        out_shape=jax.ShapeDtypeStruct(((2 * nt + 2 * tm) * TM_CHUNKS, LANES), F32),
        compiler_params=_params(("arbitrary",)),
        name="moe_experts",
    )(route["tile_expert"], route["tile_valid"], route["tile_first"], route["tile_next_expert"],
      route["slot_token"], jnp.concatenate([lead_dest, route["slot_dest"]]), x1, route["slot_w"][:, None],
      w_gate_e, w_up_e, w_down_e)


def _ln2_kernel(x1, y0, y1, g2, b2, o_ref, *, alpha):
    y = _load_token_major(y0) + _load_token_major(y1)
    o_ref[...] = _layer_norm(alpha * _load_token_major(x1) + y, g2[...], b2[...])


def _ln2_split_kernel(x1, y0, y1, g2, b2, prompt_ref, rest_ref, *, alpha, n_prompt_tiles):
    y = _load_token_major(y0) + _load_token_major(y1)
    x2 = _layer_norm(alpha * _load_token_major(x1) + y, g2[...], b2[...])
    i = pl.program_id(0)

    @pl.when(i < n_prompt_tiles)
    def _():
        prompt_ref[...] = x2

    @pl.when(i >= n_prompt_tiles)
    def _():
        rest_ref[...] = x2


def _ln2(x1, y_pairs, g2, b2, alpha, split_at=None):
    nt = x1.shape[0] // TM_CHUNKS
    tm = TOK_TILE
    n_tiles = nt // tm
    first = pl.BlockSpec((tm * TM_CHUNKS, LANES), lambda i: (i, 0))
    second = pl.BlockSpec((tm * TM_CHUNKS, LANES), lambda i: (n_tiles + i, 0))
    row = pl.BlockSpec((tm, D_MODEL), lambda i: (i, 0))
    full = lambda a: pl.BlockSpec(a.shape, lambda i: (0,) * a.ndim)
    if split_at is not None:
        n_p = split_at // tm
        assert split_at % tm == 0 and 0 < n_p < n_tiles
        return pl.pallas_call(
            functools.partial(_ln2_split_kernel, alpha=alpha, n_prompt_tiles=n_p),
            grid=(n_tiles,),
            in_specs=[first, first, second, full(g2), full(b2)],
            out_specs=[pl.BlockSpec((tm, D_MODEL), lambda i: (jnp.minimum(i, n_p - 1), 0)),
                       pl.BlockSpec((tm, D_MODEL), lambda i: (jnp.maximum(i - n_p, 0), 0))],
            out_shape=[jax.ShapeDtypeStruct((split_at, D_MODEL), F32),
                       jax.ShapeDtypeStruct((nt - split_at, D_MODEL), F32)],
            compiler_params=_params(("arbitrary",)),
            name="combine_ln2_final",
        )(x1, y_pairs, y_pairs, g2, b2)
    return pl.pallas_call(
        functools.partial(_ln2_kernel, alpha=alpha),
        grid=(n_tiles,),
        in_specs=[first, first, second, full(g2), full(b2)],
        out_specs=row,
        out_shape=jax.ShapeDtypeStruct((nt, D_MODEL), F32),
        compiler_params=_params(("parallel",)),
        name="combine_ln2",
    )(x1, y_pairs, y_pairs, g2, b2)


def _route_slots(ids, wts, n_slots):
    tm = MOE_TM
    nt = ids.shape[0]
    e_flat = ids[:, :2].reshape(-1)
    w_flat = wts[:, :2].reshape(-1)
    n_pairs = e_flat.shape[0]
    take = lambda table, idx: table.at[idx].get(mode="promise_in_bounds")
    experts = jnp.arange(N_EXPERTS, dtype=jnp.int32)
    order = jnp.argsort(e_flat, stable=True).astype(jnp.int32)
    counts = jnp.sum(e_flat[:, None] == experts[None, :], axis=0).astype(jnp.int32)
    padded = (counts + tm - 1) // tm * tm
    ends = jnp.cumsum(padded)
    offs = ends - padded
    starts = jnp.cumsum(counts) - counts
    tile_start = jnp.arange(n_slots // tm, dtype=jnp.int32) * tm
    total = ends[-1]
    tile_valid = (tile_start < total).astype(jnp.int32)
    last_start = jnp.maximum(total - tm, 0)
    tile_expert = jnp.sum(jnp.minimum(tile_start, last_start)[:, None] >= ends[None, :], axis=1).astype(jnp.int32)
    tile_expert = jnp.minimum(tile_expert, N_EXPERTS - 1)
    in_run = tile_start[:, None] + jnp.arange(tm, dtype=jnp.int32)[None, :] - take(offs, tile_expert)[:, None]
    filled = (in_run < take(counts, tile_expert)[:, None]) & (tile_valid[:, None] > 0)
    src = jnp.clip(take(starts, tile_expert)[:, None] + in_run, 0, n_pairs - 1).reshape(-1)
    pair = take(order, src)
    filled = filled.reshape(-1)
    slot_token = jnp.where(filled, pair // 2, 0)
    slot_w = jnp.where(filled, take(w_flat, pair), 0.0)
    slot_idx = jnp.arange(n_slots, dtype=jnp.int32)
    slot_dest = jnp.where(filled, (pair % 2) * nt + pair // 2, 2 * nt + slot_idx % (2 * tm))
    prev_expert = jnp.concatenate([jnp.full((1,), -1, jnp.int32), tile_expert[:-1]])
    tile_first = (tile_expert != prev_expert).astype(jnp.int32)
    later = (experts[None, :] > experts[:, None]) & (counts[None, :] > 0)
    next_expert = jnp.min(jnp.where(later, experts[None, :], N_EXPERTS), axis=1)
    next_expert = jnp.where(next_expert == N_EXPERTS, -1, next_expert).astype(jnp.int32)
    return dict(slot_token=slot_token, slot_w=slot_w, slot_dest=slot_dest, tile_expert=tile_expert,
                tile_valid=tile_valid, tile_first=tile_first, tile_next_expert=take(next_expert, tile_expert))


def _transpose_kernel(x_ref, o_ref):
    o_ref[...] = x_ref[...].T


def _kv_rows_minor(h, batch, seq):
    tt = KEY_CHUNK
    nt = seq // tt
    return pl.pallas_call(
        _transpose_kernel,
        grid=(batch, 3, nt),
        in_specs=[pl.BlockSpec((tt, 2 * LANES), lambda b, br, t: (b * nt + t, COL_KV // (2 * LANES) + br))],
        out_specs=pl.BlockSpec((None, None, 2 * LANES, tt), lambda b, br, t: (b, br, 0, t)),
        out_shape=jax.ShapeDtypeStruct((batch, 3, 2 * LANES, seq), F32),
        compiler_params=_params(("parallel", "parallel", "parallel")),
        name="kv_rows_minor",
    )(h)


def _layer(x, layer, n_prompt_rows, batch, seq, bs, past_len, page_table, caches, states, weights):
    (cache_cmp, cache_sel, state_win, state_conv_a, state_conv_c) = (caches[0], caches[1], states[0], states[1],
                                                                      states[2])
    w = weights
    nt = x.shape[0]
    n_tok = n_prompt_rows + bs
    depth = w["w_in"].shape[0]
    alpha = (2.0 * depth) ** 0.25
    rep_g = lambda a: jnp.repeat(a, HEAD_DIM, axis=1)
    bd = lambda a: jnp.zeros((LANES, LANES), F32).at[:HEAD_DIM, :HEAD_DIM].set(a[0]).at[HEAD_DIM:, HEAD_DIM:].set(a[1])
    row = lambda a: a[layer][None, :]
    p = {
        "conv_a_w": w["conv_a_w"][layer], "conv_a_b": row(w["conv_a_b"]),
        "conv_a_w8": jnp.repeat(w["conv_a_w"][layer][:, None, :], SUBLANES, axis=1),
        "ln_a_g": row(w["ln_a_g"]), "ln_a_b": row(w["ln_a_b"]), "ln_v_g": row(w["ln_v_g"]), "ln_v_b": row(w["ln_v_b"]),
        "spatial_w": w["spatial_w"][layer],
        "spatial_b_rows": jnp.repeat(w["spatial_b"][layer].T, CHUNK, axis=1),
        "spatial_w00": jnp.repeat(w["spatial_w"][layer][:, 0, 0], CHUNK)[None, :],
        "spatial_b0": jnp.repeat(w["spatial_b"][layer][:, 0], CHUNK)[None, :],
        "conv_c_w": w["conv_c_w"][layer],
        "posk_rows": rep_g(w["cmp_pos_k"][layer]), "posv_rows": rep_g(w["cmp_pos_v"][layer]),
        "projk_bd": bd(w["cmp_proj_k"][layer]), "projv_bd": bd(w["cmp_proj_v"][layer]),
        "slope_rows": jnp.broadcast_to(jnp.asarray(SLOPES, F32)[:, None], (NSA_HEADS, LANES)),
    }

    h = _inproj(x, w["w_in"], w["b_in"], layer)

    mabc_p, sta_p, stc_p = _mix_abc_prompt(h, p, batch, seq)
    md_p = _nsa_prompt(h, p, batch, seq)

    hs = h[n_prompt_rows:n_tok]
    sta_t = jnp.transpose(state_conv_a[layer], (1, 0, 2))
    stc_t = jnp.transpose(state_conv_c[layer], (1, 0, 2))
    mabc_s, a_in_s, ccx_s, v_s = _mix_abc_sample(hs[:, :7 * D_GRP], sta_t, stc_t, p)
    q_s = hs[:, COL_Q:COL_KV].reshape(bs, NSA_HEADS, HEAD_DIM)
    grp = jnp.arange(NSA_HEADS) // HEADS_PER_KV
    qm = jnp.where((jnp.arange(LANES)[None, :] // HEAD_DIM == grp[:, None])[None],
                   jnp.tile(q_s, (1, 1, KV_HEADS)), 0.0)
    kv_s = hs[:, COL_KV:COL_G].reshape(bs, 3, 2, LANES)
    graw = jnp.pad(jnp.transpose(hs[:, COL_G:N_IN].reshape(bs, 3, NSA_HEADS), (0, 2, 1)),
                   ((0, 0), (0, 0), (0, LANES - 3)))
    rows_minor = lambda a: jnp.transpose(a, (0, 1, 3, 4, 5, 2))
    cache_cmp_r, cache_sel_r, win_r = rows_minor(cache_cmp), rows_minor(cache_sel), rows_minor(state_win)
    oc, sel = _nsa_dec_cmp(page_table, qm, cache_cmp_r, p, layer, past_len)
    n_steps = past_len // PAGE_SIZE // _pages_per_step(past_len)
    n_past_blocks = past_len // SEL_BLOCK
    blocks_per_step = n_past_blocks // n_steps
    selr = jnp.transpose(sel[:, :, :n_past_blocks].reshape(bs, NSA_HEADS, n_steps, blocks_per_step), (0, 2, 1, 3))
    selr = jnp.pad(selr, ((0, 0), (0, 0), (0, 0), (0, LANES - blocks_per_step)))
    selnew = jnp.broadcast_to(sel[:, :, n_past_blocks:n_past_blocks + 1], (bs, NSA_HEADS, LANES))
    d_s = _nsa_dec_sel(page_table, qm, selr, selnew, kv_s[:, 1, 0][:, None], kv_s[:, 1, 1][:, None],
                       kv_s[:, 2, 0][:, None], kv_s[:, 2, 1][:, None], win_r, oc, graw, cache_sel_r, p, layer,
                       past_len)
    d_s = d_s.reshape(bs, NSA_HEADS, KV_HEADS, HEAD_DIM)
    md_s = jnp.concatenate([d_s[:, :HEADS_PER_KV, 0], d_s[:, HEADS_PER_KV:, 1]], axis=1).reshape(bs, D_GRP)

    pad_rows = nt - n_tok
    s_rows = nt - n_prompt_rows
    mabc_s = jnp.pad(mabc_s, ((0, s_rows - bs), (0, 0)))
    md_s = jnp.pad(md_s.astype(BF16), ((0, s_rows - bs), (0, 0)))
    wr = jnp.concatenate([w["w_group"][layer], w["w_router"][layer]], axis=1)
    wr = jnp.pad(wr, ((0, 0), (0, LANES - wr.shape[1])))
    br = jnp.pad(jnp.concatenate([w["b_group"][layer], w["b_router"][layer]]), (0, LANES - N_GROUPS - N_EXPERTS))[None]
    x1, ids, wts = _outproj(mabc_p, md_p, mabc_s, md_s, x, w["w_out"][layer].astype(BF16), row(w["ln1_g"]),
                            row(w["ln1_b"]), wr, br, alpha)

    n_slots = (-(-(2 * nt) // MOE_TM) + N_EXPERTS) * MOE_TM
    route = _route_slots(ids, wts, n_slots)
    y_pairs = _moe(route, x1, w["w_gate_e"], w["w_up_e"], w["w_down_e"], layer)
    x2 = _ln2(x1, y_pairs, row(w["ln2_g"]), row(w["ln2_b"]), alpha,
              split_at=n_prompt_rows if layer == depth - 1 else None)

    kv_t = _kv_rows_minor(h, batch, seq).reshape(batch, 3, 2, KV_HEADS, HEAD_DIM, seq)
    kv_p = jnp.transpose(kv_t, (0, 5, 1, 2, 3, 4))
    w_buf = state_win.shape[2]
    kv_s6 = hs[:, COL_KV:COL_G].reshape(bs, 1, 3, 2, KV_HEADS, HEAD_DIM)
    st_p = (kv_p[:, :, 0], kv_p[:, :, 1], kv_p[:, seq - w_buf:, 2],
            sta_p[:, 32 - (CONV_A_WIDTH - 1):], stc_p[:, 8 - (CONV_C_WIDTH - 1):])
    st_s = (kv_s6[:, :, 0], kv_s6[:, :, 1],
            jnp.concatenate([state_win[layer][:, 1:], kv_s6[:, :, 2]], axis=1),
            jnp.concatenate([state_conv_a[layer][:, 1:], a_in_s[:, None]], axis=1),
            jnp.concatenate([state_conv_c[layer][:, 1:], ccx_s[:, None]], axis=1),
            v_s[:, None])
    return x2, st_p, st_s


def kernel(x_prompt, x_sample, cache_cmp_kv, cache_sel_kv, state_win_kv, state_conv_a, state_conv_c, page_table, w_in, b_in, conv_a_w, conv_a_b, ln_a_g, ln_a_b, ln_v_g, ln_v_b, spatial_w, spatial_b, conv_c_w, cmp_pos_k, cmp_pos_v, cmp_proj_k, cmp_proj_v, w_out, ln1_g, ln1_b, ln2_g, ln2_b, w_group, b_group, w_router, b_router, w_gate_e, w_up_e, w_down_e):
    batch, seq, d_model = x_prompt.shape
    bs, dec_seq, _ = x_sample.shape
    depth = w_in.shape[0]
    past_len = page_table.shape[1] * PAGE_SIZE
    assert d_model == D_MODEL and dec_seq == 1 and w_in.shape[2] == N_IN
    assert seq % KEY_CHUNK == 0 and seq % MIX_TT == 0
    assert past_len % (PAGE_SIZE * _pages_per_step(past_len)) == 0
    assert state_win_kv.shape[2] == WINDOW and past_len >= WINDOW
    n_prompt_rows = batch * seq
    n_tok = n_prompt_rows + bs
    nt = -(-n_tok // UNIFIED_ROW_MULTIPLE) * UNIFIED_ROW_MULTIPLE
    x = jnp.concatenate([x_prompt.reshape(n_prompt_rows, d_model), x_sample.reshape(bs, d_model),
                         jnp.zeros((nt - n_tok, d_model), F32)], axis=0)
    weights = dict(w_in=jnp.transpose(w_in, (0, 2, 1)), b_in=b_in, conv_a_w=conv_a_w, conv_a_b=conv_a_b, ln_a_g=ln_a_g, ln_a_b=ln_a_b,
                   ln_v_g=ln_v_g, ln_v_b=ln_v_b, spatial_w=spatial_w, spatial_b=spatial_b, conv_c_w=conv_c_w,
                   cmp_pos_k=cmp_pos_k, cmp_pos_v=cmp_pos_v, cmp_proj_k=cmp_proj_k, cmp_proj_v=cmp_proj_v,
                   w_out=w_out, ln1_g=ln1_g, ln1_b=ln1_b, ln2_g=ln2_g, ln2_b=ln2_b, w_group=w_group,
                   b_group=b_group, w_router=w_router, b_router=b_router, w_gate_e=w_gate_e, w_up_e=w_up_e,
                   w_down_e=w_down_e)
    st_p, st_s = [], []
    for layer in range(depth):
        x, sp, ss = _layer(x, layer, n_prompt_rows, batch, seq, bs, past_len, page_table,
                           (cache_cmp_kv, cache_sel_kv), (state_win_kv, state_conv_a, state_conv_c), weights)
        st_p.append(sp)
        st_s.append(ss)
    x_prompt_rows, x_rest_rows = x
    y_prompt = x_prompt_rows.reshape(batch, seq, d_model)
    y_sample = x_rest_rows[:bs].reshape(bs, 1, d_model)
    return (y_prompt, y_sample,
            jnp.stack([s[0] for s in st_p]), jnp.stack([s[1] for s in st_p]), jnp.stack([s[2] for s in st_p]),
            jnp.stack([s[3] for s in st_p]), jnp.stack([s[4] for s in st_p]),
            jnp.stack([s[0] for s in st_s]), jnp.stack([s[1] for s in st_s]), jnp.stack([s[2] for s in st_s]),
            jnp.stack([s[3] for s in st_s]), jnp.stack([s[4] for s in st_s]), jnp.stack([s[5] for s in st_s]))
```
